```python
import math
import jax, jax.numpy as jnp
from jax import lax
import numpy as np

D_MODEL = 1024
BATCH = 2
SEQ = 8192
DEPTH = 4

GRID_W = 64
CTX_LEN = 256
N_MIXERS = 3
NORM_EPS = 1e-6
DA_HEADS = 8
DA_HEAD_DIM = 64
DA_QBLOCK = 128
ROPE_BASE = 10000.0
SSM_D_INNER = 2 * D_MODEL
SSM_HEAD_DIM = 64
SSM_HEADS = SSM_D_INNER // SSM_HEAD_DIM
SSM_GROUPS = 4
SSM_STATE = 128
SSM_CONV = 3
SSM_CHUNK = 128
SSM_CONV_DIM = SSM_D_INNER + 2 * SSM_GROUPS * SSM_STATE
SSM_IN_DIM = SSM_D_INNER + SSM_CONV_DIM + 2 * SSM_HEADS
NA_HEADS = 16
NA_HEAD_DIM = D_MODEL // NA_HEADS
NA_ROWS = 8
NA_COLS = 16
N_EXPERTS = 16
EC_CAPACITY_FACTOR = 2
EXPERT_FF = 2 * D_MODEL
N_A_LAYERS = (DEPTH + 2) // 3
N_B_LAYERS = (DEPTH + 1) // 3
N_C_LAYERS = DEPTH // 3

kernel_name = "hybrid_diffattn_ssd_natten_ecmoe_trunk"

F32 = jnp.float32


def rms_norm(x, w):
    xf = x.astype(F32)
    y = xf * lax.rsqrt(jnp.mean(xf * xf, axis=-1, keepdims=True) + NORM_EPS)
    return (y * w.astype(F32)).astype(x.dtype)


def modulate(h, shift, scale):
    return h * (1 + scale) + shift


def lambda_init(layer):
    return 0.8 - 0.6 * math.exp(-0.3 * layer)


def dense_attention(q, k, v, scale):
    s = jnp.einsum("bqhd,bkhd->bhqk", q, k).astype(F32) * scale
    p = jax.nn.softmax(s, axis=-1).astype(v.dtype)
    return jnp.einsum("bhqk,bkhd->bqhd", p, v)


def axial_rope_tables(n_tokens, dim, dtype):
    t = jnp.arange(n_tokens)
    rows = (t // GRID_W).astype(F32)
    cols = (t % GRID_W).astype(F32)
    quarter = dim // 4
    freqs = ROPE_BASE ** (-jnp.arange(quarter, dtype=F32) / quarter)
    ang_r = rows[:, None] * freqs[None, :]
    ang_c = cols[:, None] * freqs[None, :]
    ang = jnp.concatenate([ang_r, ang_r, ang_c, ang_c], axis=-1)
    return jnp.cos(ang).astype(dtype), jnp.sin(ang).astype(dtype)


def apply_axial_rope(x, cos, sin):
    d = x.shape[-1]
    xs = x.reshape(x.shape[:-1] + (2, 2, d // 4))
    rot = jnp.stack([-xs[..., 1, :], xs[..., 0, :]], axis=-2).reshape(x.shape)
    return x * cos[:, None, None, :] + rot * sin[:, None, None, :]


def diff_attention(h_ctx, h_lat, wqkv, wo, lq1, lk1, lq2, lk2, subln, lam_init, need_ctx):
    H, d = DA_HEADS, DA_HEAD_DIM
    scale = d ** -0.5

    def project(h):
        b, n, _ = h.shape
        q, k, v = jnp.split(h @ wqkv, [H * 2 * d, 2 * H * 2 * d], axis=-1)
        return q.reshape(b, n, H, 2, d), k.reshape(b, n, H, 2, d), v.reshape(b, n, H, 2 * d)

    q_c, k_c, v_c = project(h_ctx)
    q_l, k_l, v_l = project(h_lat)
    b, s = h_lat.shape[:2]
    cos, sin = axial_rope_tables(s, d, h_lat.dtype)
    q_l = apply_axial_rope(q_l, cos, sin)
    k_l = apply_axial_rope(k_l, cos, sin)
    lam = (jnp.exp(jnp.sum(lq1.astype(F32) * lk1.astype(F32)))
           - jnp.exp(jnp.sum(lq2.astype(F32) * lk2.astype(F32))) + lam_init)

    def attend(q, k, v):
        sc = jnp.einsum("bqhmd,bkhmd->bhmqk", q, k).astype(F32) * scale
        p = jax.nn.softmax(sc, axis=-1)
        w = (p[:, :, 0] - lam * p[:, :, 1]).astype(v.dtype)
        return jnp.einsum("bhqk,bkhe->bqhe", w, v)

    def post(o):
        o = rms_norm(o, subln) * (1 - lam_init)
        return o.reshape(o.shape[0], o.shape[1], H * 2 * d) @ wo

    k_all = jnp.concatenate([k_c, k_l], axis=1)
    v_all = jnp.concatenate([v_c, v_l], axis=1)
    nb = s // DA_QBLOCK
    q_blocks = jnp.moveaxis(q_l.reshape(b, nb, DA_QBLOCK, H, 2, d), 1, 0)
    o_l = lax.map(lambda qb: attend(qb, k_all, v_all), q_blocks)
    o_l = jnp.moveaxis(o_l, 0, 1).reshape(b, s, H, 2 * d)
    out_l = post(o_l)
    out_c = post(attend(q_c, k_c, v_c)) if need_ctx else None
    return out_c, out_l


def centred_depthwise_conv(u, w, bias):
    k = w.shape[0]
    out = lax.conv_general_dilated(u, w[:, None, :], window_strides=(1,),
                                   padding=[((k - 1) // 2, k // 2)],
                                   dimension_numbers=("NWC", "WIO", "NWC"),
                                   feature_group_count=u.shape[-1])
    return out + bias


def ssd_chunked(x, dt, Bm, Cm, A, h0):
    b, l, nh, p = x.shape
    g, n = Bm.shape[-2:]
    r = nh // g
    L = SSM_CHUNK
    nc = l // L
    xd = (x.astype(F32) * dt[..., None]).reshape(b, nc, L, g, r, p)
    a = (dt * A).reshape(b, nc, L, g, r)
    acs = jnp.cumsum(a, axis=2)
    Bc = Bm.astype(F32).reshape(b, nc, L, g, n)
    Cc = Cm.astype(F32).reshape(b, nc, L, g, n)
    lower = jnp.tril(jnp.ones((L, L), dtype=bool))
    seg = acs[:, :, :, None] - acs[:, :, None, :]
    decay = jnp.exp(jnp.where(lower[:, :, None, None], seg, -jnp.inf))
    cb = jnp.einsum("bclgn,bcsgn->bclsg", Cc, Bc)
    y_diag = jnp.einsum("bclsgr,bcsgrp->bclgrp", cb[..., None] * decay, xd)
    decay_to_end = jnp.exp(acs[:, :, -1:] - acs)
    chunk_states = jnp.einsum("bclgn,bclgrp->bcgrpn", Bc, xd * decay_to_end[..., None])
    chunk_decay = jnp.exp(acs[:, :, -1])

    def step(state, inp):
        dec, st = inp
        return state * dec[..., None, None] + st, state

    h_final, h_prev = lax.scan(step, h0, (jnp.moveaxis(chunk_decay, 1, 0),
                                          jnp.moveaxis(chunk_states, 1, 0)))
    h_prev = jnp.moveaxis(h_prev, 0, 1)
    y_off = jnp.einsum("bclgn,bcgrpn->bclgrp", Cc, h_prev) * jnp.exp(acs)[..., None]
    return (y_diag + y_off).reshape(b, l, nh, p), h_final


def mamba2_mixer(h_ctx, h_lat, w_in, conv_w, conv_b, dt_bias, A_log, d_skip, norm_w, w_out, need_ctx):
    NH, P, G, N = SSM_HEADS, SSM_HEAD_DIM, SSM_GROUPS, SSM_STATE

    def in_proj(h):
        b, n, _ = h.shape
        z, xbc, dt_raw = jnp.split(h @ w_in, [SSM_D_INNER, SSM_D_INNER + SSM_CONV_DIM], axis=-1)
        xbc = jax.nn.silu(centred_depthwise_conv(xbc, conv_w, conv_b))
        xs, Bm, Cm = jnp.split(xbc, [SSM_D_INNER, SSM_D_INNER + G * N], axis=-1)
        return (z, xs.reshape(b, n, NH, P), Bm.reshape(b, n, G, N), Cm.reshape(b, n, G, N),
                dt_raw.reshape(b, n, 2, NH))

    zc, xc, Bc, Cc, dtc = in_proj(h_ctx)
    zl, xl, Bl, Cl, dtl = in_proj(h_lat)
    b = h_lat.shape[0]
    y_c = jnp.zeros(xc.shape, F32)
    y_l = jnp.zeros(xl.shape, F32)
    for direction in range(2):
        A = -jnp.exp(A_log[direction].astype(F32))
        dskip = d_skip[direction].astype(F32)[:, None]

        def prep(xs, Bm, Cm, dt_raw):
            dt = jax.nn.softplus((dt_raw[:, :, direction] + dt_bias[direction]).astype(F32))
            seqs = (xs, dt, Bm, Cm)
            if direction == 1:
                seqs = tuple(jnp.flip(t, axis=1) for t in seqs)
            return seqs

        h0 = jnp.zeros((b, G, NH // G, P, N), F32)
        xs_c, dt_c, B_c, C_c = prep(xc, Bc, Cc, dtc)
        yd_c, h_ctx_final = ssd_chunked(xs_c, dt_c, B_c, C_c, A, h0)
        xs_l, dt_l, B_l, C_l = prep(xl, Bl, Cl, dtl)
        yd_l, _ = ssd_chunked(xs_l, dt_l, B_l, C_l, A, h_ctx_final)
        if direction == 1:
            yd_c = jnp.flip(yd_c, axis=1)
            yd_l = jnp.flip(yd_l, axis=1)
        y_c = y_c + yd_c + dskip * xc.astype(F32)
        y_l = y_l + yd_l + dskip * xl.astype(F32)

    def post(y, z):
        bb, n = y.shape[:2]
        y = y.reshape(bb, n, SSM_D_INNER).astype(z.dtype)
        return rms_norm(y * jax.nn.silu(z), norm_w) @ w_out

    out_l = post(y_l, zl)
    out_c = post(y_c, zc) if need_ctx else None
    return out_c, out_l


def neighbourhood_attention(h_ctx, h_lat, wqkv, wo, rpb, need_ctx):
    H, d = NA_HEADS, NA_HEAD_DIM
    scale = d ** -0.5

    def project(h):
        b, n, _ = h.shape
        q, k, v = jnp.split(h @ wqkv, 3, axis=-1)
        return q.reshape(b, n, H, d), k.reshape(b, n, H, d), v.reshape(b, n, H, d)

    q_c, k_c, v_c = project(h_ctx)
    q_l, k_l, v_l = project(h_lat)
    b, s = h_lat.shape[:2]
    rows = s // GRID_W
    wr = min(NA_ROWS, rows)
    n_w = wr * NA_COLS
    k_g = k_l.reshape(b, rows, GRID_W, H, d)
    v_g = v_l.reshape(b, rows, GRID_W, H, d)
    qcol = jnp.arange(GRID_W)
    col_start = jnp.clip(qcol - NA_COLS // 2, 0, GRID_W - NA_COLS)
    col_idx = col_start[:, None] + jnp.arange(NA_COLS)[None, :]
    dc = col_idx - qcol[:, None] + NA_COLS - 1
    q_rows = jnp.moveaxis(q_l.reshape(b, rows, GRID_W, H, d), 1, 0)

    def row_block(args):
        r, q_r = args
        rs = jnp.clip(r - wr // 2, 0, rows - wr)
        k_w = lax.dynamic_slice_in_dim(k_g, rs, wr, axis=1)[:, :, col_idx]
        v_w = lax.dynamic_slice_in_dim(v_g, rs, wr, axis=1)[:, :, col_idx]
        dr = rs + jnp.arange(wr) - r + NA_ROWS - 1
        bias = rpb[:, dr[:, None, None], dc[None, :, :]].transpose(0, 2, 1, 3).astype(F32)
        s_w = jnp.einsum("bqhd,bwqjhd->bhqwj", q_r, k_w).astype(F32) * scale + bias[None]
        s_c = jnp.einsum("bqhd,bkhd->bhqk", q_r, k_c).astype(F32) * scale
        sc = jnp.concatenate([s_w.reshape(b, H, GRID_W, n_w), s_c], axis=-1)
        p = jax.nn.softmax(sc, axis=-1).astype(v_l.dtype)
        p_w = p[..., :n_w].reshape(b, H, GRID_W, wr, NA_COLS)
        return (jnp.einsum("bhqwj,bwqjhd->bqhd", p_w, v_w)
                + jnp.einsum("bhqk,bkhd->bqhd", p[..., n_w:], v_c))

    o_l = lax.map(row_block, (jnp.arange(rows), q_rows))
    out_l = jnp.moveaxis(o_l, 0, 1).reshape(b, s, H * d) @ wo
    out_c = None
    if need_ctx:
        o_c = dense_attention(q_c, k_c, v_c, scale)
        out_c = o_c.reshape(o_c.shape[0], o_c.shape[1], H * d) @ wo
    return out_c, out_l


def expert_choice_moe(h, router_w, w1, w3, w2):
    b, n, _ = h.shape
    cap = EC_CAPACITY_FACTOR * n // N_EXPERTS
    aff = jax.nn.softmax((h @ router_w).astype(F32), axis=-1)
    gate, idx = lax.top_k(jnp.swapaxes(aff, 1, 2), cap)
    bidx = jnp.arange(b)[:, None, None]
    xs = h[bidx, idx]
    hid = jax.nn.silu(jnp.einsum("becd,edf->becf", xs, w1)) * jnp.einsum("becd,edf->becf", xs, w3)
    y = jnp.einsum("becf,efd->becd", hid, w2)
    y = y * gate[..., None].astype(y.dtype)
    return jnp.zeros_like(h).at[bidx, idx].add(y.astype(h.dtype))


def setup_inputs(seed: int = 0) -> dict:
    key = jax.random.key(seed)
    ks = iter(jax.random.split(key, 48))

    def nrm(shape, scale):
        return jax.random.normal(next(ks), shape, F32) * scale

    D = D_MODEL
    qkv_a = 3 * DA_HEADS * 2 * DA_HEAD_DIM
    dt0 = jnp.exp(jax.random.uniform(next(ks), (N_B_LAYERS, 2, SSM_HEADS), F32,
                                     minval=math.log(1e-3), maxval=math.log(1e-1)))
    return {
        "x": nrm((BATCH, SEQ, D), 1.0),
        "c": nrm((BATCH, D), 1.0),
        "ctx": nrm((BATCH, CTX_LEN, D), 1.0),
        "c_ctx": nrm((D,), 1.0),
        "ada_w": nrm((DEPTH, D, 6 * D), 0.02),
        "ada_b": nrm((DEPTH, 6 * D), 0.01),
        "norm_mix": 1.0 + nrm((DEPTH, D), 0.02),
        "norm_ffn": 1.0 + nrm((DEPTH, D), 0.02),
        "final_norm": 1.0 + nrm((D,), 0.02),
        "da_wqkv": nrm((N_A_LAYERS, D, qkv_a), D ** -0.5),
        "da_wo": nrm((N_A_LAYERS, DA_HEADS * 2 * DA_HEAD_DIM, D), (DA_HEADS * 2 * DA_HEAD_DIM) ** -0.5),
        "da_lam_q1": nrm((N_A_LAYERS, DA_HEAD_DIM), 0.1),
        "da_lam_k1": nrm((N_A_LAYERS, DA_HEAD_DIM), 0.1),
        "da_lam_q2": nrm((N_A_LAYERS, DA_HEAD_DIM), 0.1),
        "da_lam_k2": nrm((N_A_LAYERS, DA_HEAD_DIM), 0.1),
        "da_subln": 1.0 + nrm((N_A_LAYERS, 2 * DA_HEAD_DIM), 0.02),
        "ssm_w_in": nrm((N_B_LAYERS, D, SSM_IN_DIM), D ** -0.5),
        "ssm_conv_w": nrm((N_B_LAYERS, SSM_CONV, SSM_CONV_DIM), SSM_CONV ** -0.5),
        "ssm_conv_b": nrm((N_B_LAYERS, SSM_CONV_DIM), 0.01),
        "ssm_dt_bias": dt0 + jnp.log(-jnp.expm1(-dt0)),
        "ssm_A_log": jnp.log(jax.random.uniform(next(ks), (N_B_LAYERS, 2, SSM_HEADS), F32, minval=1.0, maxval=16.0)),
        "ssm_D": 1.0 + nrm((N_B_LAYERS, 2, SSM_HEADS), 0.02),
        "ssm_norm": 1.0 + nrm((N_B_LAYERS, SSM_D_INNER), 0.02),
        "ssm_w_out": nrm((N_B_LAYERS, SSM_D_INNER, D), SSM_D_INNER ** -0.5),
        "na_wqkv": nrm((N_C_LAYERS, D, 3 * NA_HEADS * NA_HEAD_DIM), D ** -0.5),
        "na_wo": nrm((N_C_LAYERS, NA_HEADS * NA_HEAD_DIM, D), (NA_HEADS * NA_HEAD_DIM) ** -0.5),
        "na_rpb": nrm((N_C_LAYERS, NA_HEADS, 2 * NA_ROWS - 1, 2 * NA_COLS - 1), 0.02),
        "router_w": nrm((DEPTH, D, N_EXPERTS), D ** -0.5),
        "exp_w1": nrm((DEPTH, N_EXPERTS, D, EXPERT_FF), D ** -0.5),
        "exp_w3": nrm((DEPTH, N_EXPERTS, D, EXPERT_FF), D ** -0.5),
        "exp_w2": nrm((DEPTH, N_EXPERTS, EXPERT_FF, D), EXPERT_FF ** -0.5),
    }


def reference(x, c, ctx, c_ctx, ada_w, ada_b, norm_mix, norm_ffn, final_norm,
              da_wqkv, da_wo, da_lam_q1, da_lam_k1, da_lam_q2, da_lam_k2, da_subln,
              ssm_w_in, ssm_conv_w, ssm_conv_b, ssm_dt_bias, ssm_A_log, ssm_D, ssm_norm, ssm_w_out,
              na_wqkv, na_wo, na_rpb, router_w, exp_w1, exp_w3, exp_w2):
    cond_lat = jax.nn.silu(c)
    cond_ctx = jax.nn.silu(c_ctx)
    ia = ib = ic = 0
    for layer in range(DEPTH):
        need_ctx = layer < DEPTH - 1
        mod_l = (cond_lat @ ada_w[layer] + ada_b[layer])[:, None, :]
        mod_c = (cond_ctx @ ada_w[layer] + ada_b[layer])[None, None, :]
        sh_ml, sc_ml, g_ml, sh_fl, sc_fl, g_fl = jnp.split(mod_l, 6, axis=-1)
        sh_mc, sc_mc, g_mc, sh_fc, sc_fc, g_fc = jnp.split(mod_c, 6, axis=-1)
        h_l = modulate(rms_norm(x, norm_mix[layer]), sh_ml, sc_ml)
        h_c = modulate(rms_norm(ctx, norm_mix[layer]), sh_mc, sc_mc)
        kind = layer % N_MIXERS
        if kind == 0:
            o_c, o_l = diff_attention(h_c, h_l, da_wqkv[ia], da_wo[ia], da_lam_q1[ia], da_lam_k1[ia],
                                      da_lam_q2[ia], da_lam_k2[ia], da_subln[ia], lambda_init(layer), need_ctx)
            ia += 1
        elif kind == 1:
            o_c, o_l = mamba2_mixer(h_c, h_l, ssm_w_in[ib], ssm_conv_w[ib], ssm_conv_b[ib], ssm_dt_bias[ib],
                                    ssm_A_log[ib], ssm_D[ib], ssm_norm[ib], ssm_w_out[ib], need_ctx)
            ib += 1
        else:
            o_c, o_l = neighbourhood_attention(h_c, h_l, na_wqkv[ic], na_wo[ic], na_rpb[ic], need_ctx)
            ic += 1
        x = x + g_ml * o_l
        h_l = modulate(rms_norm(x, norm_ffn[layer]), sh_fl, sc_fl)
        x = x + g_fl * expert_choice_moe(h_l, router_w[layer], exp_w1[layer], exp_w3[layer], exp_w2[layer])
        if need_ctx:
            ctx = ctx + g_mc * o_c
            h_c = modulate(rms_norm(ctx, norm_ffn[layer]), sh_fc, sc_fc)
            ctx = ctx + g_fc * expert_choice_moe(h_c, router_w[layer], exp_w1[layer], exp_w3[layer], exp_w2[layer])
    return rms_norm(x, final_norm)
```

```python
import functools
import math

import jax
import jax.numpy as jnp
from jax import lax
from jax.experimental import pallas as pl
from jax.experimental.pallas import tpu as pltpu

F32 = jnp.float32
BF16 = jnp.bfloat16

GRID_W = 64
NORM_EPS = 1e-6
ROPE_BASE = 10000.0
DA_HEADS = 8
DA_HEAD_DIM = 64
SSM_HEAD_DIM = 64
SSM_GROUPS = 4
SSM_STATE = 128
SSM_CHUNK = 128
NA_HEADS = 16
NA_ROWS = 8
NA_COLS = 16
N_MIXERS = 3
EC_CAPACITY_FACTOR = 2
MASK_VALUE = -1e30
LANES = 128
VMEM_LIMIT = 56 * 1024 * 1024


def _params(*sem):
    return pltpu.CompilerParams(dimension_semantics=sem, vmem_limit_bytes=VMEM_LIMIT)


def _split3(a):
    hi = a.astype(BF16)
    r1 = a - hi.astype(F32)
    mid = r1.astype(BF16)
    lo = (r1 - mid.astype(F32)).astype(BF16)
    return hi, mid, lo


def _dot(a, b):
    return jnp.dot(a, b, preferred_element_type=F32)


def _dot_nt(a, b):
    return lax.dot_general(a, b, (((1,), (1,)), ((), ())), preferred_element_type=F32)


def _dot_tn(a, b):
    return lax.dot_general(a, b, (((0,), (0,)), ((), ())), preferred_element_type=F32)


def _ada_kernel(c_ref, w_ref, b_ref, o_ref):
    c = c_ref[...]
    cond = c * jax.nn.sigmoid(c)
    o_ref[0] = jnp.dot(cond, w_ref[0], preferred_element_type=F32,
                       precision=lax.Precision.HIGHEST) + b_ref[0]


def ada_modulation(cond_rows, ada_w, ada_b):
    depth, d, n6 = ada_w.shape
    tn = 1536
    return pl.pallas_call(
        _ada_kernel,
        grid=(depth, n6 // tn),
        in_specs=[pl.BlockSpec((8, d), lambda l, j: (0, 0)),
                  pl.BlockSpec((1, d, tn), lambda l, j: (l, 0, j)),
                  pl.BlockSpec((1, 1, tn), lambda l, j: (l, 0, j))],
        out_specs=pl.BlockSpec((1, 8, tn), lambda l, j: (l, 0, j)),
        out_shape=jax.ShapeDtypeStruct((depth, 8, n6), F32),
        compiler_params=_params("parallel", "parallel"),
        name="ada_modulation",
    )(cond_rows, ada_w, ada_b.reshape(depth, 1, n6))


def _row_select(i, tm, n_ctx, vec2):
    row = i * tm + lax.broadcasted_iota(jnp.int32, (tm, 1), 0)
    return jnp.where(row >= n_ctx, vec2[1:2, :], vec2[0:1, :])


def _norm_mod_kernel(x_ref, w_ref, sh_ref, sc_ref, *rest, tm, n_ctx, with_router):
    if with_router:
        rw_ref, h_ref, aff_ref = rest
    else:
        (h_ref,) = rest
    i = pl.program_id(1)
    xf = x_ref[0]
    y = xf * lax.rsqrt(jnp.mean(xf * xf, axis=-1, keepdims=True) + NORM_EPS) * w_ref[...]
    sc = _row_select(i, tm, n_ctx, sc_ref[0])
    sh = _row_select(i, tm, n_ctx, sh_ref[0])
    h = y * (1.0 + sc) + sh
    h_ref[0] = h.astype(h_ref.dtype)
    if with_router:
        logits = lax.dot_general(rw_ref[...], h, (((1,), (1,)), ((), ())),
                                 preferred_element_type=F32, precision=lax.Precision.HIGHEST)
        m = jnp.max(logits, axis=0, keepdims=True)
        e = jnp.exp(logits - m)
        aff_ref[0] = e / jnp.sum(e, axis=0, keepdims=True)


def norm_mod(x, w, shift, scale, n_ctx, router_wt=None, tm=256):
    b, n, d = x.shape
    with_router = router_wt is not None
    in_specs = [pl.BlockSpec((1, tm, d), lambda bi, i: (bi, i, 0)),
                pl.BlockSpec((1, d), lambda bi, i: (0, 0)),
                pl.BlockSpec((1, 2, d), lambda bi, i: (bi, 0, 0)),
                pl.BlockSpec((1, 2, d), lambda bi, i: (bi, 0, 0))]
    args = [x, w.reshape(1, d), shift, scale]
    out_specs = [pl.BlockSpec((1, tm, d), lambda bi, i: (bi, i, 0))]
    out_shape = [jax.ShapeDtypeStruct((b, n, d), BF16)]
    if with_router:
        e = router_wt.shape[0]
        in_specs.append(pl.BlockSpec((e, d), lambda bi, i: (0, 0)))
        args.append(router_wt)
        out_specs.append(pl.BlockSpec((1, e, tm), lambda bi, i: (bi, 0, i)))
        out_shape.append(jax.ShapeDtypeStruct((b, e, n), F32))
    outs = pl.pallas_call(
        functools.partial(_norm_mod_kernel, tm=tm, n_ctx=n_ctx, with_router=with_router),
        grid=(b, n // tm),
        in_specs=in_specs, out_specs=out_specs, out_shape=out_shape,
        compiler_params=_params("parallel", "parallel"),
        name="norm_mod_router" if with_router else "norm_mod",
    )(*args)
    return outs if with_router else outs[0]


def _final_norm_kernel(x_ref, w_ref, o_ref):
    xf = x_ref[0]
    o_ref[0] = xf * lax.rsqrt(jnp.mean(xf * xf, axis=-1, keepdims=True) + NORM_EPS) * w_ref[...]


def final_norm(x, w, n_ctx, tm=256):
    b, n, d = x.shape
    off = n_ctx // tm
    return pl.pallas_call(
        _final_norm_kernel,
        grid=(b, (n - n_ctx) // tm),
        in_specs=[pl.BlockSpec((1, tm, d), lambda bi, i: (bi, i + off, 0)),
                  pl.BlockSpec((1, d), lambda bi, i: (0, 0))],
        out_specs=pl.BlockSpec((1, tm, d), lambda bi, i: (bi, i, 0)),
        out_shape=jax.ShapeDtypeStruct((b, n - n_ctx, d), F32),
        compiler_params=_params("parallel", "parallel"),
        name="final_norm",
    )(x, w.reshape(1, d))


def _matmul_kernel(a_ref, w_ref, *rest, tm, n_ctx, residual):
    if residual:
        res_ref, g_ref, o_ref = rest
    else:
        (o_ref,) = rest
    acc = _dot(a_ref[0], w_ref[...])
    if residual:
        g = _row_select(pl.program_id(1), tm, n_ctx, g_ref[0])
        acc = res_ref[0] + g * acc
    o_ref[0] = acc.astype(o_ref.dtype)


def matmul(a, w, out_dtype, res=None, gate=None, n_ctx=0, tm=256, tn=None):
    b, n, k = a.shape
    m = w.shape[1]
    tn = m if tn is None else tn
    residual = res is not None
    in_specs = [pl.BlockSpec((1, tm, k), lambda bi, i, j: (bi, i, 0)),
                pl.BlockSpec((k, tn), lambda bi, i, j: (0, j))]
    args = [a, w]
    if residual:
        in_specs += [pl.BlockSpec((1, tm, tn), lambda bi, i, j: (bi, i, j)),
                     pl.BlockSpec((1, 2, tn), lambda bi, i, j: (bi, 0, j))]
        args += [res, gate]
    return pl.pallas_call(
        functools.partial(_matmul_kernel, tm=tm, n_ctx=n_ctx, residual=residual),
        grid=(b, n // tm, m // tn),
        in_specs=in_specs,
        out_specs=pl.BlockSpec((1, tm, tn), lambda bi, i, j: (bi, i, j)),
        out_shape=jax.ShapeDtypeStruct((b, n, m), out_dtype),
        compiler_params=_params("parallel", "parallel", "arbitrary"),
        name="matmul_res" if residual else "matmul",
    )(*args)


def _rope_kernel(x_ref, cos_ref, sin_ref, o_ref, *, n_rope_blocks):
    j = pl.program_id(2)
    x = x_ref[0]

    @pl.when(j < n_rope_blocks)
    def _():
        lane = lax.broadcasted_iota(jnp.int32, x.shape, 1)
        even = ((lane // 16) % 2) == 0
        rot = jnp.where(even, pltpu.roll(x, LANES - 16, 1), pltpu.roll(x, 16, 1))
        o_ref[0] = (x * cos_ref[0] + rot * sin_ref[0]).astype(o_ref.dtype)

    @pl.when(j >= n_rope_blocks)
    def _():
        o_ref[0] = x.astype(o_ref.dtype)


def rope_qkv(qkv, cos_tab, sin_tab, heads, tm=256):
    b, n, m = qkv.shape
    nblk = m // LANES
    sel = lambda j: jnp.minimum(j // heads, 1)
    return pl.pallas_call(
        functools.partial(_rope_kernel, n_rope_blocks=2 * heads),
        grid=(b, n // tm, nblk),
        in_specs=[pl.BlockSpec((1, tm, LANES), lambda bi, i, j: (bi, i, j)),
                  pl.BlockSpec((1, tm, LANES), lambda bi, i, j: (sel(j), i, 0)),
                  pl.BlockSpec((1, tm, LANES), lambda bi, i, j: (sel(j), i, 0))],
        out_specs=pl.BlockSpec((1, tm, LANES), lambda bi, i, j: (bi, i, j)),
        out_shape=jax.ShapeDtypeStruct((b, n, m), BF16),
        compiler_params=_params("parallel", "parallel", "arbitrary"),
        name="rope_qkv",
    )(qkv, cos_tab, sin_tab)


def rope_tables(n_ctx, seq, scale):
    d = DA_HEAD_DIM
    t = jnp.arange(seq)
    rows = (t // GRID_W).astype(F32)
    cols = (t % GRID_W).astype(F32)
    quarter = d // 4
    freqs = ROPE_BASE ** (-jnp.arange(quarter, dtype=F32) / quarter)
    ang_r = rows[:, None] * freqs[None, :]
    ang_c = cols[:, None] * freqs[None, :]
    ang = jnp.concatenate([ang_r, ang_r, ang_c, ang_c], axis=-1)
    cos = jnp.cos(ang)
    sign = jnp.concatenate([-jnp.ones((quarter,), F32), jnp.ones((quarter,), F32)] * 2)
    sin = jnp.sin(ang) * sign[None, :]
    cos = jnp.concatenate([jnp.ones((n_ctx, d), F32), cos], axis=0)
    sin = jnp.concatenate([jnp.zeros((n_ctx, d), F32), sin], axis=0)
    cos = jnp.concatenate([cos, cos], axis=-1)
    sin = jnp.concatenate([sin, sin], axis=-1)
    return jnp.stack([cos * scale, cos]), jnp.stack([sin * scale, sin])


def _diff_attn_kernel(q_ref, k_ref, v_ref, lq1_ref, lk1_ref, lq2_ref, lk2_ref, sub_ref, o_ref,
                      m_ref, l_ref, acc_ref, *, tq, tk, nk, lam_init):
    q = q_ref[0]
    lane = lax.broadcasted_iota(jnp.int32, q.shape, 1)
    zero = jnp.zeros_like(q)
    qs = jnp.concatenate([jnp.where(lane < DA_HEAD_DIM, q, zero),
                          jnp.where(lane >= DA_HEAD_DIM, q, zero)], axis=0)
    m_ref[...] = jnp.full(m_ref.shape, -jnp.inf, F32)
    l_ref[...] = jnp.zeros(l_ref.shape, F32)
    acc_ref[...] = jnp.zeros(acc_ref.shape, F32)

    def body(c, carry):
        start = pl.multiple_of(c * tk, tk)
        k = k_ref[0, pl.ds(start, tk), :]
        v = v_ref[0, pl.ds(start, tk), :]
        s = _dot_nt(qs, k)
        m_old = m_ref[...]
        m_new = jnp.maximum(m_old, jnp.max(s, axis=-1, keepdims=True))
        alpha = jnp.exp(m_old - m_new)
        p = jnp.exp(s - m_new)
        l_ref[...] = alpha * l_ref[...] + jnp.sum(p, axis=-1, keepdims=True)
        acc_ref[...] = alpha * acc_ref[...] + _dot(p.astype(BF16), v)
        m_ref[...] = m_new
        return carry

    lax.fori_loop(0, nk // tk, body, 0)
    o = acc_ref[...] / l_ref[...]
    lam = (jnp.exp(jnp.sum(lq1_ref[...] * lk1_ref[...], axis=-1, keepdims=True))
           - jnp.exp(jnp.sum(lq2_ref[...] * lk2_ref[...], axis=-1, keepdims=True)) + lam_init)
    od = o[:tq] - lam * o[tq:]
    y = od * lax.rsqrt(jnp.mean(od * od, axis=-1, keepdims=True) + NORM_EPS) * sub_ref[...]
    o_ref[0] = (y * (1.0 - lam_init)).astype(o_ref.dtype)


def diff_attention(qkv, lam_params, subln, lam_init, q_start, n_q, n_k, tq=256, tk=768):
    b, n, _ = qkv.shape
    h = DA_HEADS
    tk = min(tk, n_k)
    qoff = q_start // tq
    vec = lambda a: a.reshape(1, -1).astype(F32)
    small = pl.BlockSpec((1, DA_HEAD_DIM), lambda bi, hi, i: (0, 0))
    return pl.pallas_call(
        functools.partial(_diff_attn_kernel, tq=tq, tk=tk, nk=n_k, lam_init=lam_init),
        grid=(b, h, n_q // tq),
        in_specs=[pl.BlockSpec((1, tq, LANES), lambda bi, hi, i: (bi, i + qoff, hi)),
                  pl.BlockSpec((1, n_k, LANES), lambda bi, hi, i: (bi, 0, h + hi)),
                  pl.BlockSpec((1, n_k, LANES), lambda bi, hi, i: (bi, 0, 2 * h + hi)),
                  small, small, small, small,
                  pl.BlockSpec((1, LANES), lambda bi, hi, i: (0, 0))],
        out_specs=pl.BlockSpec((1, tq, LANES), lambda bi, hi, i: (bi, i, hi)),
        out_shape=jax.ShapeDtypeStruct((b, n_q, h * LANES), BF16),
        scratch_shapes=[pltpu.VMEM((2 * tq, 1), F32), pltpu.VMEM((2 * tq, 1), F32),
                        pltpu.VMEM((2 * tq, LANES), F32)],
        compiler_params=_params("parallel", "parallel", "arbitrary"),
        name="diff_attention",
    )(qkv, qkv, qkv, *[vec(p) for p in lam_params], vec(subln))


def _pair_attention(qp, kp, vp, bias, n_bias):
    tq = qp.shape[0]
    lane = lax.broadcasted_iota(jnp.int32, qp.shape, 1)
    zero = jnp.zeros_like(qp)
    qs = jnp.concatenate([jnp.where(lane < 64, qp, zero), jnp.where(lane >= 64, qp, zero)], axis=0)
    s = _dot_nt(qs, kp)
    if n_bias:
        s_w = s[:, :n_bias] + bias
        s_c = s[:, n_bias:]
        m = jnp.maximum(jnp.max(s_w, axis=-1, keepdims=True), jnp.max(s_c, axis=-1, keepdims=True))
        p_w = jnp.exp(s_w - m)
        p_c = jnp.exp(s_c - m)
        l = jnp.sum(p_w, axis=-1, keepdims=True) + jnp.sum(p_c, axis=-1, keepdims=True)
        pv = _dot(p_w.astype(BF16), vp[:n_bias]) + _dot(p_c.astype(BF16), vp[n_bias:])
    else:
        m = jnp.max(s, axis=-1, keepdims=True)
        p = jnp.exp(s - m)
        l = jnp.sum(p, axis=-1, keepdims=True)
        pv = _dot(p.astype(BF16), vp)
    o = pv / l
    lane_o = lax.broadcasted_iota(jnp.int32, (tq, LANES), 1)
    return jnp.where(lane_o < 64, o[:tq], o[tq:])


def _na_kernel(*refs, wr, n_ctx, scale):
    q_ref = refs[0]
    k_refs = refs[1:1 + wr]
    v_refs = refs[1 + wr:1 + 2 * wr]
    kc_ref, vc_ref, bias_ref, o_ref, kbuf, vbuf = refs[1 + 2 * wr:]
    nw = wr * GRID_W
    for w in range(wr):
        kbuf[w * GRID_W:(w + 1) * GRID_W, :] = k_refs[w][0]
        vbuf[w * GRID_W:(w + 1) * GRID_W, :] = v_refs[w][0]
    kbuf[nw:nw + n_ctx, :] = kc_ref[0]
    vbuf[nw:nw + n_ctx, :] = vc_ref[0]
    for hp in range(NA_HEADS // 2):
        cs = slice(hp * LANES, (hp + 1) * LANES)
        qp = q_ref[0, :, cs] * scale
        bias = bias_ref[0, 2 * hp:2 * hp + 2].reshape(2 * GRID_W, nw)
        o = _pair_attention(qp, kbuf[:, cs], vbuf[:, cs], bias, nw)
        o_ref[0, :, cs] = o.astype(o_ref.dtype)


def na_bias_table(rpb, rows):
    wr = min(NA_ROWS, rows)
    qcol = jnp.arange(GRID_W)
    col_start = jnp.clip(qcol - NA_COLS // 2, 0, GRID_W - NA_COLS)
    kc = jnp.arange(GRID_W)
    inside = (kc[None, :] >= col_start[:, None]) & (kc[None, :] < col_start[:, None] + NA_COLS)
    dc = jnp.clip(kc[None, :] - qcol[:, None] + NA_COLS - 1, 0, 2 * NA_COLS - 2)
    case = jnp.arange(wr)
    w = jnp.arange(wr)
    dr = w[None, :] - case[:, None] + NA_ROWS - 1
    tab = rpb[:, dr[:, :, None, None], dc[None, None, :, :]]
    tab = jnp.where(inside[None, None, None], tab.astype(F32), MASK_VALUE)
    tab = jnp.transpose(tab, (1, 0, 3, 2, 4))
    return tab.reshape(wr, rpb.shape[0], GRID_W, wr * GRID_W)


def neighbourhood_attention(qkv, bias_tab, n_ctx):
    b, n, m3 = qkv.shape
    m = m3 // 3
    rows = (n - n_ctx) // GRID_W
    wr = min(NA_ROWS, rows)
    cb = n_ctx // GRID_W
    rs = lambda r: jnp.clip(r - wr // 2, 0, rows - wr)
    blk = lambda f: pl.BlockSpec((1, GRID_W, m), f)
    in_specs = [blk(lambda bi, r: (bi, cb + r, 0))]
    in_specs += [blk(functools.partial(lambda bi, r, w: (bi, cb + rs(r) + w, 1), w=w)) for w in range(wr)]
    in_specs += [blk(functools.partial(lambda bi, r, w: (bi, cb + rs(r) + w, 2), w=w)) for w in range(wr)]
    in_specs += [pl.BlockSpec((1, n_ctx, m), lambda bi, r: (bi, 0, 1)),
                 pl.BlockSpec((1, n_ctx, m), lambda bi, r: (bi, 0, 2)),
                 pl.BlockSpec((1, NA_HEADS, GRID_W, wr * GRID_W), lambda bi, r: (r - rs(r), 0, 0, 0))]
    nk = wr * GRID_W + n_ctx
    return pl.pallas_call(
        functools.partial(_na_kernel, wr=wr, n_ctx=n_ctx, scale=(m // NA_HEADS) ** -0.5),
        grid=(b, rows),
        in_specs=in_specs,
        out_specs=pl.BlockSpec((1, GRID_W, m), lambda bi, r: (bi, r, 0)),
        out_shape=jax.ShapeDtypeStruct((b, n - n_ctx, m), BF16),
        scratch_shapes=[pltpu.VMEM((nk, m), BF16), pltpu.VMEM((nk, m), BF16)],
        compiler_params=_params("parallel", "arbitrary"),
        name="neighbourhood_attention",
    )(*([qkv] * (1 + 2 * wr + 2)), bias_tab)


def _ctx_attn_kernel(q_ref, k_ref, v_ref, o_ref, *, scale):
    for hp in range(NA_HEADS // 2):
        cs = slice(hp * LANES, (hp + 1) * LANES)
        o = _pair_attention(q_ref[0, :, cs] * scale, k_ref[0, :, cs], v_ref[0, :, cs], None, 0)
        o_ref[0, :, cs] = o.astype(o_ref.dtype)


def context_attention(qkv, n_ctx):
    b, n, m3 = qkv.shape
    m = m3 // 3
    return pl.pallas_call(
        functools.partial(_ctx_attn_kernel, scale=(m // NA_HEADS) ** -0.5),
        grid=(b,),
        in_specs=[pl.BlockSpec((1, n_ctx, m), lambda bi: (bi, 0, 0)),
                  pl.BlockSpec((1, n_ctx, m), lambda bi: (bi, 0, 1)),
                  pl.BlockSpec((1, n_ctx, m), lambda bi: (bi, 0, 2))],
        out_specs=pl.BlockSpec((1, n_ctx, m), lambda bi: (bi, 0, 0)),
        out_shape=jax.ShapeDtypeStruct((b, n_ctx, m), BF16),
        compiler_params=_params("parallel"),
        name="context_attention",
    )(qkv, qkv, qkv)


def _conv_silu_kernel(x_ref, w_ref, b_ref, o_ref, *, n, n_ctx):
    x = x_ref[0]
    row = lax.broadcasted_iota(jnp.int32, x.shape, 0)
    xm = jnp.where((row == 0) | (row == n_ctx), 0.0, pltpu.roll(x, 1, 0))
    xp = jnp.where((row == n_ctx - 1) | (row == n - 1), 0.0, pltpu.roll(x, n - 1, 0))
    y = w_ref[0:1, :] * xm + w_ref[1:2, :] * x + w_ref[2:3, :] * xp + b_ref[...]
    o_ref[0] = y * jax.nn.sigmoid(y)


def conv_silu(x, w, bias, n_ctx):
    b, n, c = x.shape
    return pl.pallas_call(
        functools.partial(_conv_silu_kernel, n=n, n_ctx=n_ctx),
        grid=(b, c // LANES),
        in_specs=[pl.BlockSpec((1, n, LANES), lambda bi, j: (bi, 0, j)),
                  pl.BlockSpec((3, LANES), lambda bi, j: (0, j)),
                  pl.BlockSpec((1, LANES), lambda bi, j: (0, j))],
        out_specs=pl.BlockSpec((1, n, LANES), lambda bi, j: (bi, 0, j)),
        out_shape=jax.ShapeDtypeStruct((b, n, c), F32),
        compiler_params=_params("parallel", "parallel"),
        name="conv_silu",
    )(x, w, bias.reshape(1, c))


def _softplus(x):
    return jnp.maximum(x, 0.0) + jnp.log1p(jnp.exp(-jnp.abs(x)))


def _ssd_kernel(x_ref, b_ref, c_ref, dt_ref, dtb_ref, alog_ref, e_ref, y_ref, state_ref,
                *, direction, nheads):
    L = SSM_CHUNK
    P = SSM_HEAD_DIM
    G = SSM_GROUPS
    hpg = nheads // G
    reverse = direction == 1

    @pl.when(pl.program_id(1) == 0)
    def _():
        state_ref[...] = jnp.zeros(state_ref.shape, F32)

    li = lax.broadcasted_iota(jnp.int32, (L, L), 0)
    si = lax.broadcasted_iota(jnp.int32, (L, L), 1)
    allowed = (si >= li) if reverse else (si <= li)
    tri = jnp.where(allowed, 1.0, 0.0).astype(BF16)
    tri_t = jnp.where((li >= si) if reverse else (li <= si), 1.0, 0.0).astype(BF16)

    dt_raw = dt_ref[0][:, direction * nheads:(direction + 1) * nheads]
    dt = _softplus(dt_raw + dtb_ref[...])
    a = dt * (-jnp.exp(alog_ref[...]))
    a3 = _split3(a)
    acs = sum(_dot(tri, p) for p in a3)
    acs_t = sum(_dot_tn(p, tri_t) for p in a3)
    expand = e_ref[...]
    acs_x = sum(_dot(p, expand) for p in _split3(acs))
    dt_x = sum(_dot(p, expand) for p in _split3(dt))
    end = 0 if reverse else L - 1
    total_x = acs_x[end:end + 1, :]

    xd = x_ref[0] * dt_x
    xd_b = xd.astype(BF16)
    x_state = (xd * jnp.exp(total_x - acs_x)).astype(BF16)
    exp_acs_x = jnp.exp(acs_x)
    chunk_decay_x = jnp.exp(total_x)
    lane = lax.broadcasted_iota(jnp.int32, (L, LANES), 1)

    for g in range(G):
        gs = slice(g * SSM_STATE, (g + 1) * SSM_STATE)
        hs = slice(g * hpg * P, (g + 1) * hpg * P)
        cg = c_ref[0][:, gs].astype(BF16)
        bg = b_ref[0][:, gs].astype(BF16)
        cb = _dot_nt(cg, bg)
        h_prev = state_ref[:, hs]
        y_off = _dot(cg, h_prev.astype(BF16)) * exp_acs_x[:, hs]
        state_ref[:, hs] = h_prev * chunk_decay_x[:, hs] + _dot_tn(bg, x_state[:, hs])
        for rp in range(hpg // 2):
            mats = []
            for r in (2 * rp, 2 * rp + 1):
                hidx = g * hpg + r
                seg = acs[:, hidx:hidx + 1] - acs_t[hidx:hidx + 1, :]
                mats.append((cb * jnp.exp(jnp.where(allowed, seg, -jnp.inf))).astype(BF16))
            ps = slice((g * hpg + 2 * rp) * P, (g * hpg + 2 * rp + 2) * P)
            yd = _dot(jnp.concatenate(mats, axis=0), xd_b[:, ps])
            y_ref[0, :, ps] = jnp.where(lane < P, yd[:L], yd[L:]) + y_off[:, 2 * rp * P:(2 * rp + 2) * P]


def ssd_scan(xbc, dt_raw, dt_bias, a_log, expand, direction, nheads, n_ctx):
    b, n, _ = xbc.shape
    d_inner = nheads * SSM_HEAD_DIM
    gn = SSM_GROUPS * SSM_STATE
    L = SSM_CHUNK
    nc = n // L
    ncc = n_ctx // L
    if direction == 0:
        cidx = lambda c: c
    else:
        cidx = lambda c: jnp.where(c < ncc, ncc - 1 - c, nc - 1 - (c - ncc))
    return pl.pallas_call(
        functools.partial(_ssd_kernel, direction=direction, nheads=nheads),
        grid=(b, nc),
        in_specs=[pl.BlockSpec((1, L, d_inner), lambda bi, c: (bi, cidx(c), 0)),
                  pl.BlockSpec((1, L, gn), lambda bi, c: (bi, cidx(c), d_inner // gn)),
                  pl.BlockSpec((1, L, gn), lambda bi, c: (bi, cidx(c), d_inner // gn + 1)),
                  pl.BlockSpec((1, L, LANES), lambda bi, c: (bi, cidx(c), 0)),
                  pl.BlockSpec((1, nheads), lambda bi, c: (0, 0)),
                  pl.BlockSpec((1, nheads), lambda bi, c: (0, 0)),
                  pl.BlockSpec((nheads, d_inner), lambda bi, c: (0, 0))],
        out_specs=pl.BlockSpec((1, L, d_inner), lambda bi, c: (bi, cidx(c), 0)),
        out_shape=jax.ShapeDtypeStruct((b, n, d_inner), F32),
        scratch_shapes=[pltpu.VMEM((SSM_STATE, d_inner), F32)],
        compiler_params=_params("parallel", "arbitrary"),
        name="ssd_scan_%d" % direction,
    )(xbc, xbc, xbc, dt_raw, dt_bias[direction:direction + 1], a_log[direction:direction + 1], expand)


def _ssm_out_kernel(y0_ref, y1_ref, x_ref, z_ref, d_ref, nw_ref, w_ref, res_ref, g_ref, o_ref,
                    *, tm, n_ctx):
    z = z_ref[0]
    u = (y0_ref[0] + y1_ref[0] + d_ref[...] * x_ref[0]) * (z * jax.nn.sigmoid(z))
    un = u * lax.rsqrt(jnp.mean(u * u, axis=-1, keepdims=True) + NORM_EPS) * nw_ref[...]
    g = _row_select(pl.program_id(1), tm, n_ctx, g_ref[0])
    o_ref[0] = res_ref[0] + g * _dot(un.astype(BF16), w_ref[...])


def ssm_out(y0, y1, xbc, z, d_chan, norm_w, w_out, res, gate, n_ctx, tm=256):
    b, n, di = y0.shape
    d = w_out.shape[1]
    row = lambda c: pl.BlockSpec((1, tm, c), lambda bi, i: (bi, i, 0))
    vec = lambda c: pl.BlockSpec((1, c), lambda bi, i: (0, 0))
    return pl.pallas_call(
        functools.partial(_ssm_out_kernel, tm=tm, n_ctx=n_ctx),
        grid=(b, n // tm),
        in_specs=[row(di), row(di), row(di), row(di), vec(di), vec(di),
                  pl.BlockSpec((di, d), lambda bi, i: (0, 0)), row(d),
                  pl.BlockSpec((1, 2, d), lambda bi, i: (bi, 0, 0))],
        out_specs=row(d),
        out_shape=jax.ShapeDtypeStruct((b, n, d), F32),
        compiler_params=_params("parallel", "parallel"),
        name="ssm_out",
    )(y0, y1, xbc, z, d_chan, norm_w.reshape(1, di), w_out, res, gate)


def _expert_ffn_kernel(x_ref, g_ref, w1_ref, w3_ref, w2_ref, o_ref):
    f = pl.program_id(1)
    x = x_ref[0]
    h1 = _dot(x, w1_ref[0].astype(BF16))
    h3 = _dot(x, w3_ref[0].astype(BF16))
    hid = (h1 * jax.nn.sigmoid(h1) * h3).astype(BF16)
    part = _dot(hid, w2_ref[0].astype(BF16))

    @pl.when(f == 0)
    def _():
        o_ref[0] = part

    @pl.when(f > 0)
    def _():
        o_ref[0] = o_ref[0] + part

    @pl.when(f == pl.num_programs(1) - 1)
    def _():
        o_ref[0] = o_ref[0] * g_ref[0]


def expert_ffn(xs, gates, w1, w3, w2, tf=256):
    e, m, d = xs.shape
    ff = w1.shape[2]
    return pl.pallas_call(
        _expert_ffn_kernel,
        grid=(e, ff // tf),
        in_specs=[pl.BlockSpec((1, m, d), lambda ei, f: (ei, 0, 0)),
                  pl.BlockSpec((1, m, 1), lambda ei, f: (ei, 0, 0)),
                  pl.BlockSpec((1, d, tf), lambda ei, f: (ei, 0, f)),
                  pl.BlockSpec((1, d, tf), lambda ei, f: (ei, 0, f)),
                  pl.BlockSpec((1, tf, d), lambda ei, f: (ei, f, 0))],
        out_specs=pl.BlockSpec((1, m, d), lambda ei, f: (ei, 0, 0)),
        out_shape=jax.ShapeDtypeStruct((e, m, d), F32),
        compiler_params=_params("parallel", "arbitrary"),
        name="expert_ffn",
    )(xs, gates, w1, w3, w2)


def _lambda_init(layer):
    return 0.8 - 0.6 * math.exp(-0.3 * layer)


def kernel(x, c, ctx, c_ctx, ada_w, ada_b, norm_mix, norm_ffn, final_norm_w,
           da_wqkv, da_wo, da_lam_q1, da_lam_k1, da_lam_q2, da_lam_k2, da_subln,
           ssm_w_in, ssm_conv_w, ssm_conv_b, ssm_dt_bias, ssm_A_log, ssm_D, ssm_norm, ssm_w_out,
           na_wqkv, na_wo, na_rpb, router_w, exp_w1, exp_w3, exp_w2):
    b, seq, d = x.shape
    n_ctx = ctx.shape[1]
    n = n_ctx + seq
    depth = ada_w.shape[0]
    n_exp = router_w.shape[-1]

    cond_rows = jnp.zeros((8, d), F32).at[:b].set(c).at[b].set(c_ctx)
    mods = ada_modulation(cond_rows, ada_w, ada_b).reshape(depth, 8, 6, d)
    mods = jnp.stack([jnp.broadcast_to(mods[:, b:b + 1], (depth, b, 6, d)), mods[:, :b]], axis=2)

    xj = jnp.concatenate([ctx, x], axis=1)
    rope_cos, rope_sin = rope_tables(n_ctx, seq, DA_HEAD_DIM ** -0.5)

    ia = ib = ic = 0
    for layer in range(depth):
        need_ctx = layer < depth - 1
        sh_m, sc_m, g_m, sh_f, sc_f, g_f = [mods[layer, :, :, k] for k in range(6)]
        h = norm_mod(xj, norm_mix[layer], sh_m, sc_m, n_ctx)
        kind = layer % N_MIXERS
        if kind == 0:
            qkv = matmul(h, da_wqkv[ia].astype(BF16), F32, tn=1024)
            qkv = rope_qkv(qkv, rope_cos, rope_sin, DA_HEADS)
            lam_params = (da_lam_q1[ia], da_lam_k1[ia], da_lam_q2[ia], da_lam_k2[ia])
            li = _lambda_init(layer)
            o_l = diff_attention(qkv, lam_params, da_subln[ia], li, n_ctx, seq, n)
            o_c = diff_attention(qkv, lam_params, da_subln[ia], li, 0, n_ctx, n_ctx)
            o = jnp.concatenate([o_c, o_l], axis=1)
            xj = matmul(o, da_wo[ia].astype(BF16), F32, res=xj, gate=g_m, n_ctx=n_ctx)
            ia += 1
        elif kind == 1:
            nh = ssm_A_log.shape[-1]
            di = nh * SSM_HEAD_DIM
            conv_dim = ssm_conv_w.shape[-1]
            w_in = ssm_w_in[ib].astype(BF16)
            z = matmul(h, w_in[:, :di], F32, tn=1024)
            xbc = matmul(h, w_in[:, di:di + conv_dim], F32, tn=1024)
            w_dt = jnp.zeros((d, LANES), BF16).at[:, :2 * nh].set(w_in[:, di + conv_dim:])
            dt_raw = matmul(h, w_dt, F32)
            xbc = conv_silu(xbc, ssm_conv_w[ib], ssm_conv_b[ib], n_ctx)
            expand = (jnp.arange(di)[None, :] // SSM_HEAD_DIM == jnp.arange(nh)[:, None]).astype(BF16)
            ys = [ssd_scan(xbc, dt_raw, ssm_dt_bias[ib], ssm_A_log[ib], expand, dirn, nh, n_ctx)
                  for dirn in range(2)]
            d_chan = jnp.repeat(ssm_D[ib, 0] + ssm_D[ib, 1], SSM_HEAD_DIM).reshape(1, di)
            xj = ssm_out(ys[0], ys[1], xbc, z, d_chan, ssm_norm[ib], ssm_w_out[ib].astype(BF16),
                         xj, g_m, n_ctx)
            ib += 1
        else:
            qkv = matmul(h, na_wqkv[ic].astype(BF16), BF16, tn=1024)
            bias_tab = na_bias_table(na_rpb[ic], seq // GRID_W)
            o_l = neighbourhood_attention(qkv, bias_tab, n_ctx)
            o_c = context_attention(qkv, n_ctx)
            o = jnp.concatenate([o_c, o_l], axis=1)
            xj = matmul(o, na_wo[ic].astype(BF16), F32, res=xj, gate=g_m, n_ctx=n_ctx)
            ic += 1

        h2, aff = norm_mod(xj, norm_ffn[layer], sh_f, sc_f, n_ctx, router_wt=router_w[layer].T)
        cap_l = EC_CAPACITY_FACTOR * seq // n_exp
        gate_l, idx_l = lax.top_k(aff[:, :, n_ctx:], cap_l)
        gate, idx = gate_l, idx_l + n_ctx
        if need_ctx:
            cap_c = EC_CAPACITY_FACTOR * n_ctx // n_exp
            gate_c, idx_c = lax.top_k(aff[:, :, :n_ctx], cap_c)
            gate = jnp.concatenate([gate_l, gate_c], axis=-1)
            idx = jnp.concatenate([idx, idx_c], axis=-1)
        cap = idx.shape[-1]
        bidx = jnp.arange(b)[:, None, None]
        xs = h2[bidx, idx]
        xs = jnp.swapaxes(xs, 0, 1).reshape(n_exp, b * cap, d)
        gates = jnp.swapaxes(gate, 0, 1).reshape(n_exp, b * cap, 1)
        y = expert_ffn(xs, gates, exp_w1[layer], exp_w3[layer], exp_w2[layer])
        y = jnp.swapaxes(y.reshape(n_exp, b, cap, d), 0, 1)
        moe = jnp.zeros_like(xj).at[bidx, idx].add(y)
        g_rows = jnp.concatenate([jnp.broadcast_to(g_f[:, 0:1], (b, n_ctx, d)),
                                  jnp.broadcast_to(g_f[:, 1:2], (b, seq, d))], axis=1)
        xj = xj + g_rows * moe

    return final_norm(xj, final_norm_w, n_ctx)
```

```python
import functools
import math

import jax
import jax.numpy as jnp
import numpy as np
from jax import lax
from jax.experimental import pallas as pl
from jax.experimental.pallas import tpu as pltpu

F32 = jnp.float32
BF16 = jnp.bfloat16

GRID_W = 64
NORM_EPS = 1e-6
ROPE_BASE = 10000.0
DA_HEADS = 8
DA_HEAD_DIM = 64
DA_KV_CHUNK = 768
SSM_HEAD_DIM = 64
SSM_GROUPS = 4
SSM_STATE = 128
SSM_CHUNK = 128
NA_HEADS = 16
NA_ROWS = 8
NA_COLS = 16
N_MIXERS = 3
EC_CAPACITY_FACTOR = 2
MASK_VALUE = -1e30
LANES = 128
VMEM_LIMIT = 56 * 1024 * 1024


def _params(*sem):
    return pltpu.CompilerParams(dimension_semantics=sem, vmem_limit_bytes=VMEM_LIMIT)


def _split3(a):
    hi = a.astype(BF16)
    r1 = a - hi.astype(F32)
    mid = r1.astype(BF16)
    lo = (r1 - mid.astype(F32)).astype(BF16)
    return hi, mid, lo


def _dot(a, b):
    return jnp.dot(a, b, preferred_element_type=F32)


def _dot_nt(a, b):
    return lax.dot_general(a, b, (((1,), (1,)), ((), ())), preferred_element_type=F32)


def _dot_tn(a, b):
    return lax.dot_general(a, b, (((0,), (0,)), ((), ())), preferred_element_type=F32)


def _ada_kernel(c_ref, w_ref, b_ref, o_ref):
    c = c_ref[...]
    cond = c * jax.nn.sigmoid(c)
    o_ref[0] = jnp.dot(cond, w_ref[0], preferred_element_type=F32,
                       precision=lax.Precision.HIGHEST) + b_ref[0]


def ada_modulation(cond_rows, ada_w, ada_b):
    depth, d, n6 = ada_w.shape
    tn = 1536
    return pl.pallas_call(
        _ada_kernel,
        grid=(depth, n6 // tn),
        in_specs=[pl.BlockSpec((8, d), lambda l, j: (0, 0)),
                  pl.BlockSpec((1, d, tn), lambda l, j: (l, 0, j)),
                  pl.BlockSpec((1, 1, tn), lambda l, j: (l, 0, j))],
        out_specs=pl.BlockSpec((1, 8, tn), lambda l, j: (l, 0, j)),
        out_shape=jax.ShapeDtypeStruct((depth, 8, n6), F32),
        compiler_params=_params("parallel", "parallel"),
        name="ada_modulation",
    )(cond_rows, ada_w, ada_b.reshape(depth, 1, n6))


def _row_select(i, tm, n_ctx, vec2):
    row = i * tm + lax.broadcasted_iota(jnp.int32, (tm, 1), 0)
    return jnp.where(row >= n_ctx, vec2[1:2, :], vec2[0:1, :])


def _norm_mod_kernel(x_ref, w_ref, sh_ref, sc_ref, *rest, tm, n_ctx, with_router):
    if with_router:
        rw_ref, h_ref, aff_ref = rest
    else:
        (h_ref,) = rest
    i = pl.program_id(1)
    xf = x_ref[0]
    y = xf * lax.rsqrt(jnp.mean(xf * xf, axis=-1, keepdims=True) + NORM_EPS) * w_ref[...]
    sc = _row_select(i, tm, n_ctx, sc_ref[0])
    sh = _row_select(i, tm, n_ctx, sh_ref[0])
    h = y * (1.0 + sc) + sh
    h_ref[0] = h.astype(h_ref.dtype)
    if with_router:
        logits = lax.dot_general(rw_ref[...], h, (((1,), (1,)), ((), ())),
                                 preferred_element_type=F32, precision=lax.Precision.HIGHEST)
        m = jnp.max(logits, axis=0, keepdims=True)
        e = jnp.exp(logits - m)
        aff_ref[0] = e / jnp.sum(e, axis=0, keepdims=True)


def norm_mod(x, w, shift, scale, n_ctx, router_wt=None, tm=256):
    b, n, d = x.shape
    with_router = router_wt is not None
    in_specs = [pl.BlockSpec((1, tm, d), lambda bi, i: (bi, i, 0)),
                pl.BlockSpec((1, d), lambda bi, i: (0, 0)),
                pl.BlockSpec((1, 2, d), lambda bi, i: (bi, 0, 0)),
                pl.BlockSpec((1, 2, d), lambda bi, i: (bi, 0, 0))]
    args = [x, w.reshape(1, d), shift, scale]
    out_specs = [pl.BlockSpec((1, tm, d), lambda bi, i: (bi, i, 0))]
    out_shape = [jax.ShapeDtypeStruct((b, n, d), BF16)]
    if with_router:
        e = router_wt.shape[0]
        in_specs.append(pl.BlockSpec((e, d), lambda bi, i: (0, 0)))
        args.append(router_wt)
        out_specs.append(pl.BlockSpec((1, e, tm), lambda bi, i: (bi, 0, i)))
        out_shape.append(jax.ShapeDtypeStruct((b, e, n), F32))
    outs = pl.pallas_call(
        functools.partial(_norm_mod_kernel, tm=tm, n_ctx=n_ctx, with_router=with_router),
        grid=(b, n // tm),
        in_specs=in_specs, out_specs=out_specs, out_shape=out_shape,
        compiler_params=_params("parallel", "parallel"),
        name="norm_mod_router" if with_router else "norm_mod",
    )(*args)
    return outs if with_router else outs[0]


def _final_norm_kernel(x_ref, w_ref, o_ref):
    xf = x_ref[0]
    o_ref[0] = xf * lax.rsqrt(jnp.mean(xf * xf, axis=-1, keepdims=True) + NORM_EPS) * w_ref[...]


def final_norm(x, w, n_ctx, tm=256):
    b, n, d = x.shape
    off = n_ctx // tm
    return pl.pallas_call(
        _final_norm_kernel,
        grid=(b, (n - n_ctx) // tm),
        in_specs=[pl.BlockSpec((1, tm, d), lambda bi, i: (bi, i + off, 0)),
                  pl.BlockSpec((1, d), lambda bi, i: (0, 0))],
        out_specs=pl.BlockSpec((1, tm, d), lambda bi, i: (bi, i, 0)),
        out_shape=jax.ShapeDtypeStruct((b, n - n_ctx, d), F32),
        compiler_params=_params("parallel", "parallel"),
        name="final_norm",
    )(x, w.reshape(1, d))


def _matmul_kernel(a_ref, w_ref, *rest, tm, n_ctx, residual):
    if residual:
        res_ref, g_ref, o_ref = rest
    else:
        (o_ref,) = rest
    acc = _dot(a_ref[0], w_ref[...])
    if residual:
        g = _row_select(pl.program_id(1), tm, n_ctx, g_ref[0])
        acc = res_ref[0] + g * acc
    o_ref[0] = acc.astype(o_ref.dtype)


def matmul(a, w, out_dtype, res=None, gate=None, n_ctx=0, tm=256, tn=None):
    b, n, k = a.shape
    m = w.shape[1]
    tn = m if tn is None else tn
    residual = res is not None
    in_specs = [pl.BlockSpec((1, tm, k), lambda bi, i, j: (bi, i, 0)),
                pl.BlockSpec((k, tn), lambda bi, i, j: (0, j))]
    args = [a, w]
    if residual:
        in_specs += [pl.BlockSpec((1, tm, tn), lambda bi, i, j: (bi, i, j)),
                     pl.BlockSpec((1, 2, tn), lambda bi, i, j: (bi, 0, j))]
        args += [res, gate]
    return pl.pallas_call(
        functools.partial(_matmul_kernel, tm=tm, n_ctx=n_ctx, residual=residual),
        grid=(b, n // tm, m // tn),
        in_specs=in_specs,
        out_specs=pl.BlockSpec((1, tm, tn), lambda bi, i, j: (bi, i, j)),
        out_shape=jax.ShapeDtypeStruct((b, n, m), out_dtype),
        compiler_params=_params("parallel", "parallel", "arbitrary"),
        name="matmul_res" if residual else "matmul",
    )(*args)


def _rotate_pairs(x, axis):
    seg = lax.broadcasted_iota(jnp.int32, x.shape, axis) // 16
    n = x.shape[axis]
    return jnp.where(seg % 2 == 0, pltpu.roll(x, n - 16, axis), pltpu.roll(x, 16, axis))


def _da_qkv_kernel(h_ref, wqt_ref, wk_ref, wvt_ref, cq_ref, sq_ref, ck_ref, sk_ref,
                   qt_ref, k_ref, vt_ref, *, heads):
    h = h_ref[0]
    qt = _dot_nt(wqt_ref[...], h)
    cq, sq = cq_ref[...], sq_ref[...]
    for hd in range(heads):
        rs = slice(hd * LANES, (hd + 1) * LANES)
        x = qt[rs]
        qt_ref[0, rs, :] = (x * cq + _rotate_pairs(x, 0) * sq).astype(qt_ref.dtype)
    k = _dot(h, wk_ref[...])
    ck, sk = ck_ref[...], sk_ref[...]
    for hd in range(heads):
        cs = slice(hd * LANES, (hd + 1) * LANES)
        x = k[:, cs]
        k_ref[0, :, cs] = (x * ck + _rotate_pairs(x, 1) * sk).astype(k_ref.dtype)
    vt_ref[0, 0] = _dot_nt(wvt_ref[...], h).astype(vt_ref.dtype)


def da_qkv(h, wqkv, tabs, tm):
    b, n, d = h.shape
    m = wqkv.shape[1] // 3
    cq_t, sq_t, ck, sk = tabs
    wqt = wqkv[:, :m].T.astype(BF16)
    wk = wqkv[:, m:2 * m].astype(BF16)
    wvt = wqkv[:, 2 * m:].T.astype(BF16)
    full = lambda r, c: pl.BlockSpec((r, c), lambda bi, i: (0, 0))
    return pl.pallas_call(
        functools.partial(_da_qkv_kernel, heads=m // LANES),
        grid=(b, n // tm),
        in_specs=[pl.BlockSpec((1, tm, d), lambda bi, i: (bi, i, 0)),
                  full(m, d), full(d, m), full(m, d),
                  pl.BlockSpec((LANES, tm), lambda bi, i: (0, i)),
                  pl.BlockSpec((LANES, tm), lambda bi, i: (0, i)),
                  pl.BlockSpec((tm, LANES), lambda bi, i: (i, 0)),
                  pl.BlockSpec((tm, LANES), lambda bi, i: (i, 0))],
        out_specs=[pl.BlockSpec((1, m, tm), lambda bi, i: (bi, 0, i)),
                   pl.BlockSpec((1, tm, m), lambda bi, i: (bi, i, 0)),
                   pl.BlockSpec((1, 1, m, tm), lambda bi, i: (bi, i, 0, 0))],
        out_shape=[jax.ShapeDtypeStruct((b, m, n), BF16),
                   jax.ShapeDtypeStruct((b, n, m), BF16),
                   jax.ShapeDtypeStruct((b, n // tm, m, tm), BF16)],
        compiler_params=_params("parallel", "parallel"),
        name="da_qkv",
    )(h, wqt, wk, wvt, cq_t, sq_t, ck, sk)


def rope_tables(n_ctx, seq, q_scale):
    d = DA_HEAD_DIM
    t = jnp.arange(seq)
    rows = (t // GRID_W).astype(F32)
    cols = (t % GRID_W).astype(F32)
    quarter = d // 4
    freqs = ROPE_BASE ** (-jnp.arange(quarter, dtype=F32) / quarter)
    ang_r = rows[:, None] * freqs[None, :]
    ang_c = cols[:, None] * freqs[None, :]
    ang = jnp.concatenate([ang_r, ang_r, ang_c, ang_c], axis=-1)
    cos = jnp.cos(ang)
    sign = jnp.concatenate([-jnp.ones((quarter,), F32), jnp.ones((quarter,), F32)] * 2)
    sin = jnp.sin(ang) * sign[None, :]
    cos = jnp.concatenate([jnp.ones((n_ctx, d), F32), cos], axis=0)
    sin = jnp.concatenate([jnp.zeros((n_ctx, d), F32), sin], axis=0)
    cos = jnp.concatenate([cos, cos], axis=-1)
    sin = jnp.concatenate([sin, sin], axis=-1)
    return (cos * q_scale).T, (sin * q_scale).T, cos, sin


def _diff_attn_kernel(qt_ref, k_ref, vt_ref, lq1_ref, lk1_ref, lq2_ref, lk2_ref, sub_ref, o_ref,
                      acc_ref, s_ref, *, tk, kw, nchunks, lam_init):
    qt = qt_ref[0]
    tq = qt.shape[1]
    sub = lax.broadcasted_iota(jnp.int32, qt.shape, 0)
    zero = jnp.zeros_like(qt)
    maps = (jnp.where(sub < DA_HEAD_DIM, qt, zero), jnp.where(sub >= DA_HEAD_DIM, qt, zero))
    acc_ref[...] = jnp.zeros(acc_ref.shape, F32)
    half = kw // 2

    def col_reduce(x, op):
        return op(op(x.reshape(8, x.shape[0] // 8, x.shape[1]), axis=0), axis=0, keepdims=True)

    def scores(c, slot):
        start = c * tk if isinstance(c, int) else pl.multiple_of(c * tk, tk)
        for g in range(2):
            for hf in range(2):
                k = k_ref[0, pl.ds(start + hf * half, half), :]
                s_ref[slot, g, hf * half:(hf + 1) * half, :] = _dot(k, maps[g])

    def softmax_pv(c, slot, carry):
        vt = vt_ref[0, c]
        new = []
        for g in range(2):
            m_old, l_old = carry[g]
            s = s_ref[slot, g]
            m_new = jnp.maximum(m_old, col_reduce(s, jnp.max))
            alpha = jnp.exp2(m_old - m_new)
            p = jnp.exp2(s - m_new)
            l_new = alpha * l_old + col_reduce(p, jnp.sum)
            pb = p.astype(BF16)
            acc_ref[g] = (alpha * acc_ref[g] + _dot(vt[:, :half], pb[:half])
                          + _dot(vt[:, half:kw], pb[half:]))
            new.append((m_new, l_new))
        return tuple(new)

    def body(i, carry):
        c = 2 * i
        scores(c + 1, 1)
        carry = softmax_pv(c, 0, carry)
        scores(c + 2, 0)
        return softmax_pv(c + 1, 1, carry)

    init = (jnp.full((1, tq), -jnp.inf, F32), jnp.zeros((1, tq), F32))
    scores(0, 0)
    npairs = (nchunks - 1) // 2
    carry = lax.fori_loop(0, npairs, body, (init, init))
    if (nchunks - 1) % 2:
        scores(nchunks - 1, 1)
        carry = softmax_pv(nchunks - 2, 0, carry)
        carry = softmax_pv(nchunks - 1, 1, carry)
    else:
        carry = softmax_pv(nchunks - 1, 0, carry)
    (_, l0), (_, l1) = carry
    lam = (jnp.exp(jnp.sum(lq1_ref[...] * lk1_ref[...], axis=-1, keepdims=True))
           - jnp.exp(jnp.sum(lq2_ref[...] * lk2_ref[...], axis=-1, keepdims=True)) + lam_init)
    od = acc_ref[0] / l0 - lam * (acc_ref[1] / l1)
    y = od * lax.rsqrt(jnp.mean(od * od, axis=0, keepdims=True) + NORM_EPS) * sub_ref[...]
    o_ref[0] = (y * (1.0 - lam_init)).T.astype(o_ref.dtype)


def diff_attention(qt, k, vt, lam_params, subln, lam_init, q_start, n_q, n_k, tq=256):
    b, m, n = qt.shape
    h = m // LANES
    tk = vt.shape[-1]
    kw = min(tk, n_k)
    nchunks = n_k // kw
    qoff = q_start // tq
    vec = lambda a: a.reshape(1, -1).astype(F32)
    small = pl.BlockSpec((1, DA_HEAD_DIM), lambda bi, hi, i: (0, 0))
    return pl.pallas_call(
        functools.partial(_diff_attn_kernel, tk=tk, kw=kw, nchunks=nchunks, lam_init=lam_init),
        grid=(b, h, n_q // tq),
        in_specs=[pl.BlockSpec((1, LANES, tq), lambda bi, hi, i: (bi, hi, i + qoff)),
                  pl.BlockSpec((1, n_k, LANES), lambda bi, hi, i: (bi, 0, hi)),
                  pl.BlockSpec((1, nchunks, LANES, tk), lambda bi, hi, i: (bi, 0, hi, 0)),
                  small, small, small, small,
                  pl.BlockSpec((LANES, 1), lambda bi, hi, i: (0, 0))],
        out_specs=pl.BlockSpec((1, tq, LANES), lambda bi, hi, i: (bi, i, hi)),
        out_shape=jax.ShapeDtypeStruct((b, n_q, m), BF16),
        scratch_shapes=[pltpu.VMEM((2, LANES, tq), F32), pltpu.VMEM((2, 2, kw, tq), F32)],
        compiler_params=_params("parallel", "parallel", "arbitrary"),
        name="diff_attention",
    )(qt, k, vt, *[vec(p) for p in lam_params], subln.reshape(-1, 1).astype(F32))


def _pair_attention(qp, kp, vp, bias, n_bias):
    tq = qp.shape[0]
    lane = lax.broadcasted_iota(jnp.int32, qp.shape, 1)
    zero = jnp.zeros_like(qp)
    qs = jnp.concatenate([jnp.where(lane < 64, qp, zero), jnp.where(lane >= 64, qp, zero)], axis=0)
    s = _dot_nt(qs, kp)
    if n_bias:
        s_w = s[:, :n_bias] + bias
        s_c = s[:, n_bias:]
        m = jnp.maximum(jnp.max(s_w, axis=-1, keepdims=True), jnp.max(s_c, axis=-1, keepdims=True))
        p_w = jnp.exp(s_w - m)
        p_c = jnp.exp(s_c - m)
        l = jnp.sum(p_w, axis=-1, keepdims=True) + jnp.sum(p_c, axis=-1, keepdims=True)
        pv = _dot(p_w.astype(BF16), vp[:n_bias]) + _dot(p_c.astype(BF16), vp[n_bias:])
    else:
        m = jnp.max(s, axis=-1, keepdims=True)
        p = jnp.exp(s - m)
        l = jnp.sum(p, axis=-1, keepdims=True)
        pv = _dot(p.astype(BF16), vp)
    o = pv / l
    lane_o = lax.broadcasted_iota(jnp.int32, (tq, LANES), 1)
    return jnp.where(lane_o < 64, o[:tq], o[tq:])


def _na_kernel(*refs, wr, n_ctx, scale):
    q_ref = refs[0]
    k_refs = refs[1:1 + wr]
    v_refs = refs[1 + wr:1 + 2 * wr]
    kc_ref, vc_ref, bias_ref, o_ref, kbuf, vbuf = refs[1 + 2 * wr:]
    nw = wr * GRID_W
    for w in range(wr):
        kbuf[w * GRID_W:(w + 1) * GRID_W, :] = k_refs[w][0]
        vbuf[w * GRID_W:(w + 1) * GRID_W, :] = v_refs[w][0]
    kbuf[nw:nw + n_ctx, :] = kc_ref[0]
    vbuf[nw:nw + n_ctx, :] = vc_ref[0]
    for hp in range(NA_HEADS // 2):
        cs = slice(hp * LANES, (hp + 1) * LANES)
        qp = q_ref[0, :, cs] * scale
        bias = bias_ref[0, 2 * hp:2 * hp + 2].reshape(2 * GRID_W, nw)
        o = _pair_attention(qp, kbuf[:, cs], vbuf[:, cs], bias, nw)
        o_ref[0, :, cs] = o.astype(o_ref.dtype)


def na_bias_table(rpb, rows):
    wr = min(NA_ROWS, rows)
    qcol = np.arange(GRID_W)
    col_start = np.clip(qcol - NA_COLS // 2, 0, GRID_W - NA_COLS)
    kc = np.arange(GRID_W)
    inside = (kc[None, :] >= col_start[:, None]) & (kc[None, :] < col_start[:, None] + NA_COLS)
    dc = np.clip(kc[None, :] - qcol[:, None] + NA_COLS - 1, 0, 2 * NA_COLS - 2)
    dr = np.arange(wr)[None, :] - np.arange(wr)[:, None] + NA_ROWS - 1
    sel_r = (dr[:, :, None] == np.arange(2 * NA_ROWS - 1)).astype(np.float32)
    sel_c = (dc[:, :, None] == np.arange(2 * NA_COLS - 1)).astype(np.float32)
    tab = jnp.einsum("cwa,hab,qkb->chqwk", sel_r, rpb.astype(F32), sel_c,
                     precision=lax.Precision.HIGHEST)
    tab = jnp.where(inside[None, None, :, None, :], tab, MASK_VALUE)
    return tab.reshape(wr, rpb.shape[0], GRID_W, wr * GRID_W)


def neighbourhood_attention(qkv, bias_tab, n_ctx):
    b, n, m3 = qkv.shape
    m = m3 // 3
    rows = (n - n_ctx) // GRID_W
    wr = min(NA_ROWS, rows)
    cb = n_ctx // GRID_W
    rs = lambda r: jnp.clip(r - wr // 2, 0, rows - wr)
    blk = lambda f: pl.BlockSpec((1, GRID_W, m), f)
    in_specs = [blk(lambda bi, r: (bi, cb + r, 0))]
    in_specs += [blk(functools.partial(lambda bi, r, w: (bi, cb + rs(r) + w, 1), w=w)) for w in range(wr)]
    in_specs += [blk(functools.partial(lambda bi, r, w: (bi, cb + rs(r) + w, 2), w=w)) for w in range(wr)]
    in_specs += [pl.BlockSpec((1, n_ctx, m), lambda bi, r: (bi, 0, 1)),
                 pl.BlockSpec((1, n_ctx, m), lambda bi, r: (bi, 0, 2)),
                 pl.BlockSpec((1, NA_HEADS, GRID_W, wr * GRID_W), lambda bi, r: (r - rs(r), 0, 0, 0))]
    nk = wr * GRID_W + n_ctx
    return pl.pallas_call(
        functools.partial(_na_kernel, wr=wr, n_ctx=n_ctx, scale=(m // NA_HEADS) ** -0.5),
        grid=(b, rows),
        in_specs=in_specs,
        out_specs=pl.BlockSpec((1, GRID_W, m), lambda bi, r: (bi, r, 0)),
        out_shape=jax.ShapeDtypeStruct((b, n - n_ctx, m), BF16),
        scratch_shapes=[pltpu.VMEM((nk, m), BF16), pltpu.VMEM((nk, m), BF16)],
        compiler_params=_params("parallel", "arbitrary"),
        name="neighbourhood_attention",
    )(*([qkv] * (1 + 2 * wr + 2)), bias_tab)


def _ctx_attn_kernel(q_ref, k_ref, v_ref, o_ref, *, scale):
    for hp in range(NA_HEADS // 2):
        cs = slice(hp * LANES, (hp + 1) * LANES)
        o = _pair_attention(q_ref[0, :, cs] * scale, k_ref[0, :, cs], v_ref[0, :, cs], None, 0)
        o_ref[0, :, cs] = o.astype(o_ref.dtype)


def context_attention(qkv, n_ctx):
    b, n, m3 = qkv.shape
    m = m3 // 3
    return pl.pallas_call(
        functools.partial(_ctx_attn_kernel, scale=(m // NA_HEADS) ** -0.5),
        grid=(b,),
        in_specs=[pl.BlockSpec((1, n_ctx, m), lambda bi: (bi, 0, 0)),
                  pl.BlockSpec((1, n_ctx, m), lambda bi: (bi, 0, 1)),
                  pl.BlockSpec((1, n_ctx, m), lambda bi: (bi, 0, 2))],
        out_specs=pl.BlockSpec((1, n_ctx, m), lambda bi: (bi, 0, 0)),
        out_shape=jax.ShapeDtypeStruct((b, n_ctx, m), BF16),
        compiler_params=_params("parallel"),
        name="context_attention",
    )(qkv, qkv, qkv)


def _conv_silu_kernel(x_ref, w_ref, b_ref, o_ref, *, n, n_ctx):
    x = x_ref[0]
    row = lax.broadcasted_iota(jnp.int32, x.shape, 0)
    xm = jnp.where((row == 0) | (row == n_ctx), 0.0, pltpu.roll(x, 1, 0))
    xp = jnp.where((row == n_ctx - 1) | (row == n - 1), 0.0, pltpu.roll(x, n - 1, 0))
    y = w_ref[0:1, :] * xm + w_ref[1:2, :] * x + w_ref[2:3, :] * xp + b_ref[...]
    o_ref[0] = y * jax.nn.sigmoid(y)


def conv_silu(x, w, bias, n_ctx):
    b, n, c = x.shape
    return pl.pallas_call(
        functools.partial(_conv_silu_kernel, n=n, n_ctx=n_ctx),
        grid=(b, c // LANES),
        in_specs=[pl.BlockSpec((1, n, LANES), lambda bi, j: (bi, 0, j)),
                  pl.BlockSpec((3, LANES), lambda bi, j: (0, j)),
                  pl.BlockSpec((1, LANES), lambda bi, j: (0, j))],
        out_specs=pl.BlockSpec((1, n, LANES), lambda bi, j: (bi, 0, j)),
        out_shape=jax.ShapeDtypeStruct((b, n, c), F32),
        compiler_params=_params("parallel", "parallel"),
        name="conv_silu",
    )(x, w, bias.reshape(1, c))


def _softplus(x):
    return jnp.maximum(x, 0.0) + jnp.log1p(jnp.exp(-jnp.abs(x)))


def _ssd_kernel(x_ref, b_ref, c_ref, dt_ref, dtb_ref, alog_ref, e_ref, y_ref, state_ref,
                *, direction, nheads):
    L = SSM_CHUNK
    P = SSM_HEAD_DIM
    G = SSM_GROUPS
    hpg = nheads // G
    reverse = direction == 1

    @pl.when(pl.program_id(1) == 0)
    def _():
        state_ref[...] = jnp.zeros(state_ref.shape, F32)

    li = lax.broadcasted_iota(jnp.int32, (L, L), 0)
    si = lax.broadcasted_iota(jnp.int32, (L, L), 1)
    allowed = (si >= li) if reverse else (si <= li)
    tri = jnp.where(allowed, 1.0, 0.0).astype(BF16)
    tri_t = jnp.where((li >= si) if reverse else (li <= si), 1.0, 0.0).astype(BF16)

    dt_raw = dt_ref[0][:, direction * nheads:(direction + 1) * nheads]
    dt = _softplus(dt_raw + dtb_ref[...])
    a = dt * (-jnp.exp(alog_ref[...]))
    a3 = _split3(a)
    acs = sum(_dot(tri, p) for p in a3)
    acs_t = sum(_dot_tn(p, tri_t) for p in a3)
    expand = e_ref[...]
    acs_x = sum(_dot(p, expand) for p in _split3(acs))
    dt_x = sum(_dot(p, expand) for p in _split3(dt))
    end = 0 if reverse else L - 1
    total_x = acs_x[end:end + 1, :]

    xd = x_ref[0] * dt_x
    xd_b = xd.astype(BF16)
    x_state = (xd * jnp.exp(total_x - acs_x)).astype(BF16)
    exp_acs_x = jnp.exp(acs_x)
    chunk_decay_x = jnp.exp(total_x)
    lane = lax.broadcasted_iota(jnp.int32, (L, LANES), 1)

    for g in range(G):
        gs = slice(g * SSM_STATE, (g + 1) * SSM_STATE)
        hs = slice(g * hpg * P, (g + 1) * hpg * P)
        cg = c_ref[0][:, gs].astype(BF16)
        bg = b_ref[0][:, gs].astype(BF16)
        cb = _dot_nt(cg, bg)
        h_prev = state_ref[:, hs]
        y_off = _dot(cg, h_prev.astype(BF16)) * exp_acs_x[:, hs]
        state_ref[:, hs] = h_prev * chunk_decay_x[:, hs] + _dot_tn(bg, x_state[:, hs])
        for rp in range(hpg // 2):
            mats = []
            for r in (2 * rp, 2 * rp + 1):
                hidx = g * hpg + r
                seg = acs[:, hidx:hidx + 1] - acs_t[hidx:hidx + 1, :]
                mats.append((cb * jnp.exp(jnp.where(allowed, seg, -jnp.inf))).astype(BF16))
            ps = slice((g * hpg + 2 * rp) * P, (g * hpg + 2 * rp + 2) * P)
            yd = _dot(jnp.concatenate(mats, axis=0), xd_b[:, ps])
            y_ref[0, :, ps] = jnp.where(lane < P, yd[:L], yd[L:]) + y_off[:, 2 * rp * P:(2 * rp + 2) * P]


def ssd_scan(xbc, dt_raw, dt_bias, a_log, expand, direction, nheads, n_ctx):
    b, n, _ = xbc.shape
    d_inner = nheads * SSM_HEAD_DIM
    gn = SSM_GROUPS * SSM_STATE
    L = SSM_CHUNK
    nc = n // L
    ncc = n_ctx // L
    if direction == 0:
        cidx = lambda c: c
    else:
        cidx = lambda c: jnp.where(c < ncc, ncc - 1 - c, nc - 1 - (c - ncc))
    return pl.pallas_call(
        functools.partial(_ssd_kernel, direction=direction, nheads=nheads),
        grid=(b, nc),
        in_specs=[pl.BlockSpec((1, L, d_inner), lambda bi, c: (bi, cidx(c), 0)),
                  pl.BlockSpec((1, L, gn), lambda bi, c: (bi, cidx(c), d_inner // gn)),
                  pl.BlockSpec((1, L, gn), lambda bi, c: (bi, cidx(c), d_inner // gn + 1)),
                  pl.BlockSpec((1, L, LANES), lambda bi, c: (bi, cidx(c), 0)),
                  pl.BlockSpec((1, nheads), lambda bi, c: (0, 0)),
                  pl.BlockSpec((1, nheads), lambda bi, c: (0, 0)),
                  pl.BlockSpec((nheads, d_inner), lambda bi, c: (0, 0))],
        out_specs=pl.BlockSpec((1, L, d_inner), lambda bi, c: (bi, cidx(c), 0)),
        out_shape=jax.ShapeDtypeStruct((b, n, d_inner), F32),
        scratch_shapes=[pltpu.VMEM((SSM_STATE, d_inner), F32)],
        compiler_params=_params("parallel", "arbitrary"),
        name="ssd_scan_%d" % direction,
    )(xbc, xbc, xbc, dt_raw, dt_bias[direction:direction + 1], a_log[direction:direction + 1], expand)


def _ssm_out_kernel(y0_ref, y1_ref, x_ref, z_ref, d_ref, nw_ref, w_ref, res_ref, g_ref, o_ref,
                    *, tm, n_ctx):
    z = z_ref[0]
    u = (y0_ref[0] + y1_ref[0] + d_ref[...] * x_ref[0]) * (z * jax.nn.sigmoid(z))
    un = u * lax.rsqrt(jnp.mean(u * u, axis=-1, keepdims=True) + NORM_EPS) * nw_ref[...]
    g = _row_select(pl.program_id(1), tm, n_ctx, g_ref[0])
    o_ref[0] = res_ref[0] + g * _dot(un.astype(BF16), w_ref[...])


def ssm_out(y0, y1, xbc, z, d_chan, norm_w, w_out, res, gate, n_ctx, tm=256):
    b, n, di = y0.shape
    d = w_out.shape[1]
    row = lambda c: pl.BlockSpec((1, tm, c), lambda bi, i: (bi, i, 0))
    vec = lambda c: pl.BlockSpec((1, c), lambda bi, i: (0, 0))
    return pl.pallas_call(
        functools.partial(_ssm_out_kernel, tm=tm, n_ctx=n_ctx),
        grid=(b, n // tm),
        in_specs=[row(di), row(di), row(di), row(di), vec(di), vec(di),
                  pl.BlockSpec((di, d), lambda bi, i: (0, 0)), row(d),
                  pl.BlockSpec((1, 2, d), lambda bi, i: (bi, 0, 0))],
        out_specs=row(d),
        out_shape=jax.ShapeDtypeStruct((b, n, d), F32),
        compiler_params=_params("parallel", "parallel"),
        name="ssm_out",
    )(y0, y1, xbc, z, d_chan, norm_w.reshape(1, di), w_out, res, gate)


def _expert_ffn_kernel(x_ref, g_ref, w1_ref, w3_ref, w2_ref, o_ref):
    f = pl.program_id(1)
    x = x_ref[0]
    h1 = _dot(x, w1_ref[0].astype(BF16))
    h3 = _dot(x, w3_ref[0].astype(BF16))
    hid = (h1 * jax.nn.sigmoid(h1) * h3).astype(BF16)
    part = _dot(hid, w2_ref[0].astype(BF16))

    @pl.when(f == 0)
    def _():
        o_ref[0] = part

    @pl.when(f > 0)
    def _():
        o_ref[0] = o_ref[0] + part

    @pl.when(f == pl.num_programs(1) - 1)
    def _():
        o_ref[0] = o_ref[0] * g_ref[0]


def expert_ffn(xs, gates, w1, w3, w2, layer, tf=256):
    e, m, d = xs.shape
    ff = w1.shape[3]
    return pl.pallas_call(
        _expert_ffn_kernel,
        grid=(e, ff // tf),
        in_specs=[pl.BlockSpec((1, m, d), lambda ei, f: (ei, 0, 0)),
                  pl.BlockSpec((1, m, 1), lambda ei, f: (ei, 0, 0)),
                  pl.BlockSpec((None, 1, d, tf), lambda ei, f: (layer, ei, 0, f)),
                  pl.BlockSpec((None, 1, d, tf), lambda ei, f: (layer, ei, 0, f)),
                  pl.BlockSpec((None, 1, tf, d), lambda ei, f: (layer, ei, f, 0))],
        out_specs=pl.BlockSpec((1, m, d), lambda ei, f: (ei, 0, 0)),
        out_shape=jax.ShapeDtypeStruct((e, m, d), F32),
        compiler_params=_params("parallel", "arbitrary"),
        name="expert_ffn",
    )(xs, gates, w1, w3, w2)


def _lambda_init(layer):
    return 0.8 - 0.6 * math.exp(-0.3 * layer)


def kernel(x, c, ctx, c_ctx, ada_w, ada_b, norm_mix, norm_ffn, final_norm_w,
           da_wqkv, da_wo, da_lam_q1, da_lam_k1, da_lam_q2, da_lam_k2, da_subln,
           ssm_w_in, ssm_conv_w, ssm_conv_b, ssm_dt_bias, ssm_A_log, ssm_D, ssm_norm, ssm_w_out,
           na_wqkv, na_wo, na_rpb, router_w, exp_w1, exp_w3, exp_w2):
    b, seq, d = x.shape
    n_ctx = ctx.shape[1]
    n = n_ctx + seq
    depth = ada_w.shape[0]
    n_exp = router_w.shape[-1]

    cond_rows = jnp.zeros((8, d), F32).at[:b].set(c).at[b].set(c_ctx)
    mods = ada_modulation(cond_rows, ada_w, ada_b).reshape(depth, 8, 6, d)
    mods = jnp.stack([jnp.broadcast_to(mods[:, b:b + 1], (depth, b, 6, d)), mods[:, :b]], axis=2)

    xj = jnp.concatenate([ctx, x], axis=1)
    rope_tabs = rope_tables(n_ctx, seq, DA_HEAD_DIM ** -0.5 * math.log2(math.e))

    ia = ib = ic = 0
    for layer in range(depth):
        need_ctx = layer < depth - 1
        sh_m, sc_m, g_m, sh_f, sc_f, g_f = [mods[layer, :, :, k] for k in range(6)]
        h = norm_mod(xj, norm_mix[layer], sh_m, sc_m, n_ctx)
        kind = layer % N_MIXERS
        if kind == 0:
            qt, k, vt = da_qkv(h, da_wqkv[ia], rope_tabs, DA_KV_CHUNK)
            lam_params = (da_lam_q1[ia], da_lam_k1[ia], da_lam_q2[ia], da_lam_k2[ia])
            li = _lambda_init(layer)
            o_l = diff_attention(qt, k, vt, lam_params, da_subln[ia], li, n_ctx, seq, n)
            o_c = diff_attention(qt, k, vt, lam_params, da_subln[ia], li, 0, n_ctx, n_ctx)
            o = jnp.concatenate([o_c, o_l], axis=1)
            xj = matmul(o, da_wo[ia].astype(BF16), F32, res=xj, gate=g_m, n_ctx=n_ctx)
            ia += 1
        elif kind == 1:
            nh = ssm_A_log.shape[-1]
            di = nh * SSM_HEAD_DIM
            conv_dim = ssm_conv_w.shape[-1]
            w_in = ssm_w_in[ib].astype(BF16)
            z = matmul(h, w_in[:, :di], F32, tn=1024)
            xbc = matmul(h, w_in[:, di:di + conv_dim], F32, tn=1024)
            w_dt = jnp.zeros((d, LANES), BF16).at[:, :2 * nh].set(w_in[:, di + conv_dim:])
            dt_raw = matmul(h, w_dt, F32)
            xbc = conv_silu(xbc, ssm_conv_w[ib], ssm_conv_b[ib], n_ctx)
            expand = (jnp.arange(di)[None, :] // SSM_HEAD_DIM == jnp.arange(nh)[:, None]).astype(BF16)
            ys = [ssd_scan(xbc, dt_raw, ssm_dt_bias[ib], ssm_A_log[ib], expand, dirn, nh, n_ctx)
                  for dirn in range(2)]
            d_chan = jnp.repeat(ssm_D[ib, 0] + ssm_D[ib, 1], SSM_HEAD_DIM).reshape(1, di)
            xj = ssm_out(ys[0], ys[1], xbc, z, d_chan, ssm_norm[ib], ssm_w_out[ib].astype(BF16),
                         xj, g_m, n_ctx)
            ib += 1
        else:
            qkv = matmul(h, na_wqkv[ic].astype(BF16), BF16, tn=1024)
            bias_tab = na_bias_table(na_rpb[ic], seq // GRID_W)
            o_l = neighbourhood_attention(qkv, bias_tab, n_ctx)
            o_c = context_attention(qkv, n_ctx)
            o = jnp.concatenate([o_c, o_l], axis=1)
            xj = matmul(o, na_wo[ic].astype(BF16), F32, res=xj, gate=g_m, n_ctx=n_ctx)
            ic += 1

        h2, aff = norm_mod(xj, norm_ffn[layer], sh_f, sc_f, n_ctx, router_wt=router_w[layer].T)
        cap_l = EC_CAPACITY_FACTOR * seq // n_exp
        gate_l, idx_l = lax.top_k(aff[:, :, n_ctx:], cap_l)
        gate, idx = gate_l, idx_l + n_ctx
        if need_ctx:
            cap_c = EC_CAPACITY_FACTOR * n_ctx // n_exp
            gate_c, idx_c = lax.top_k(aff[:, :, :n_ctx], cap_c)
            gate = jnp.concatenate([gate_l, gate_c], axis=-1)
            idx = jnp.concatenate([idx, idx_c], axis=-1)
        cap = idx.shape[-1]
        bidx = jnp.arange(b)[:, None, None]
        xs = h2[bidx, idx]
        xs = jnp.swapaxes(xs, 0, 1).reshape(n_exp, b * cap, d)
        gates = jnp.swapaxes(gate, 0, 1).reshape(n_exp, b * cap, 1)
        y = expert_ffn(xs, gates, exp_w1, exp_w3, exp_w2, layer)
        y = jnp.swapaxes(y.reshape(n_exp, b, cap, d), 0, 1)
        moe = jnp.zeros_like(xj).at[bidx, idx].add(y)
        g_rows = jnp.concatenate([jnp.broadcast_to(g_f[:, 0:1], (b, n_ctx, d)),
                                  jnp.broadcast_to(g_f[:, 1:2], (b, seq, d))], axis=1)
        xj = xj + g_rows * moe

    return final_norm(xj, final_norm_w, n_ctx)
```

```python
import functools
import math

import jax
import jax.numpy as jnp
import numpy as np
from jax import lax
from jax.experimental import pallas as pl
from jax.experimental.pallas import tpu as pltpu

F32 = jnp.float32
BF16 = jnp.bfloat16

GRID_W = 64
NORM_EPS = 1e-6
ROPE_BASE = 10000.0
DA_HEADS = 8
DA_HEAD_DIM = 64
DA_KV_CHUNK = 768
SSM_HEAD_DIM = 64
SSM_GROUPS = 4
SSM_STATE = 128
SSM_CHUNK = 128
NA_HEADS = 16
NA_ROWS = 8
NA_COLS = 16
N_MIXERS = 3
EC_CAPACITY_FACTOR = 2
MASK_VALUE = -1e30
ONE_BITS = 0x3F800000
LANES = 128
VMEM_LIMIT = 56 * 1024 * 1024


def _params(*sem):
    return pltpu.CompilerParams(dimension_semantics=sem, vmem_limit_bytes=VMEM_LIMIT)


def _split3(a):
    hi = a.astype(BF16)
    r1 = a - hi.astype(F32)
    mid = r1.astype(BF16)
    lo = (r1 - mid.astype(F32)).astype(BF16)
    return hi, mid, lo


def _dot(a, b):
    return jnp.dot(a, b, preferred_element_type=F32)


def _dot_nt(a, b):
    return lax.dot_general(a, b, (((1,), (1,)), ((), ())), preferred_element_type=F32)


def _dot_tn(a, b):
    return lax.dot_general(a, b, (((0,), (0,)), ((), ())), preferred_element_type=F32)


def _ada_kernel(c_ref, w_ref, b_ref, o_ref):
    c = c_ref[...]
    cond = c * jax.nn.sigmoid(c)
    o_ref[0] = jnp.dot(cond, w_ref[0], preferred_element_type=F32,
                       precision=lax.Precision.HIGHEST) + b_ref[0]


def ada_modulation(cond_rows, ada_w, ada_b):
    depth, d, n6 = ada_w.shape
    tn = 1536
    return pl.pallas_call(
        _ada_kernel,
        grid=(depth, n6 // tn),
        in_specs=[pl.BlockSpec((8, d), lambda l, j: (0, 0)),
                  pl.BlockSpec((1, d, tn), lambda l, j: (l, 0, j)),
                  pl.BlockSpec((1, 1, tn), lambda l, j: (l, 0, j))],
        out_specs=pl.BlockSpec((1, 8, tn), lambda l, j: (l, 0, j)),
        out_shape=jax.ShapeDtypeStruct((depth, 8, n6), F32),
        compiler_params=_params("parallel", "parallel"),
        name="ada_modulation",
    )(cond_rows, ada_w, ada_b.reshape(depth, 1, n6))


def _row_select(i, tm, n_ctx, vec2):
    row = i * tm + lax.broadcasted_iota(jnp.int32, (tm, 1), 0)
    return jnp.where(row >= n_ctx, vec2[1:2, :], vec2[0:1, :])


def _norm_mod_kernel(x_ref, w_ref, sh_ref, sc_ref, *rest, tm, n_ctx, n_exp):
    i = pl.program_id(1)
    xf = x_ref[0]
    y = xf * lax.rsqrt(jnp.mean(xf * xf, axis=-1, keepdims=True) + NORM_EPS) * w_ref[...]
    sc = _row_select(i, tm, n_ctx, sc_ref[0])
    sh = _row_select(i, tm, n_ctx, sh_ref[0])
    h = y * (1.0 + sc) + sh
    if not n_exp:
        (h_ref,) = rest
        h_ref[0] = h.astype(h_ref.dtype)
        return
    rwt_ref, rwp_ref, hp_ref, aff_ref, afft_ref = rest
    bits = lax.bitcast_convert_type(h.astype(BF16).astype(F32), jnp.uint32)
    dh = bits.shape[1] // 2
    hp_ref[0] = (bits[:, :dh] >> 16) | bits[:, dh:]
    logits = lax.dot_general(rwt_ref[...], h, (((1,), (1,)), ((), ())),
                             preferred_element_type=F32, precision=lax.Precision.HIGHEST)
    e = jnp.exp(logits - jnp.max(logits, axis=0, keepdims=True))
    aff_ref[0] = e / jnp.sum(e, axis=0, keepdims=True)
    logits_t = jnp.dot(h, rwp_ref[...], preferred_element_type=F32, precision=lax.Precision.HIGHEST)
    lane = lax.broadcasted_iota(jnp.int32, logits_t.shape, 1)
    logits_t = jnp.where(lane < n_exp, logits_t, MASK_VALUE)
    et = jnp.exp(logits_t - jnp.max(logits_t, axis=1, keepdims=True))
    afft_ref[0] = et / jnp.sum(et, axis=1, keepdims=True)


def norm_mod(x, w, shift, scale, n_ctx, router_w=None, tm=256):
    b, n, d = x.shape
    n_exp = 0 if router_w is None else router_w.shape[1]
    in_specs = [pl.BlockSpec((1, tm, d), lambda bi, i: (bi, i, 0)),
                pl.BlockSpec((1, d), lambda bi, i: (0, 0)),
                pl.BlockSpec((1, 2, d), lambda bi, i: (bi, 0, 0)),
                pl.BlockSpec((1, 2, d), lambda bi, i: (bi, 0, 0))]
    args = [x, w.reshape(1, d), shift, scale]
    if not n_exp:
        out_specs = [pl.BlockSpec((1, tm, d), lambda bi, i: (bi, i, 0))]
        out_shape = [jax.ShapeDtypeStruct((b, n, d), BF16)]
    else:
        in_specs += [pl.BlockSpec((n_exp, d), lambda bi, i: (0, 0)),
                     pl.BlockSpec((d, LANES), lambda bi, i: (0, 0))]
        args += [router_w.T, jnp.zeros((d, LANES), F32).at[:, :n_exp].set(router_w)]
        out_specs = [pl.BlockSpec((1, tm, d // 2), lambda bi, i: (bi, i, 0)),
                     pl.BlockSpec((1, n_exp, tm), lambda bi, i: (bi, 0, i)),
                     pl.BlockSpec((1, tm, LANES), lambda bi, i: (bi, i, 0))]
        out_shape = [jax.ShapeDtypeStruct((b, n, d // 2), jnp.uint32),
                     jax.ShapeDtypeStruct((b, n_exp, n), F32),
                     jax.ShapeDtypeStruct((b, n, LANES), F32)]
    outs = pl.pallas_call(
        functools.partial(_norm_mod_kernel, tm=tm, n_ctx=n_ctx, n_exp=n_exp),
        grid=(b, n // tm),
        in_specs=in_specs, out_specs=out_specs, out_shape=out_shape,
        compiler_params=_params("parallel", "parallel"),
        name="norm_mod_router" if n_exp else "norm_mod",
    )(*args)
    return outs if n_exp else outs[0]


def _final_norm_kernel(x_ref, w_ref, o_ref):
    xf = x_ref[0]
    o_ref[0] = xf * lax.rsqrt(jnp.mean(xf * xf, axis=-1, keepdims=True) + NORM_EPS) * w_ref[...]


def final_norm(x, w, n_ctx, tm=256):
    b, n, d = x.shape
    off = n_ctx // tm
    return pl.pallas_call(
        _final_norm_kernel,
        grid=(b, (n - n_ctx) // tm),
        in_specs=[pl.BlockSpec((1, tm, d), lambda bi, i: (bi, i + off, 0)),
                  pl.BlockSpec((1, d), lambda bi, i: (0, 0))],
        out_specs=pl.BlockSpec((1, tm, d), lambda bi, i: (bi, i, 0)),
        out_shape=jax.ShapeDtypeStruct((b, n - n_ctx, d), F32),
        compiler_params=_params("parallel", "parallel"),
        name="final_norm",
    )(x, w.reshape(1, d))


def _matmul_kernel(a_ref, w_ref, *rest, tm, n_ctx, residual):
    if residual:
        res_ref, g_ref, o_ref = rest
    else:
        (o_ref,) = rest
    acc = _dot(a_ref[0], w_ref[...])
    if residual:
        g = _row_select(pl.program_id(1), tm, n_ctx, g_ref[0])
        acc = res_ref[0] + g * acc
    o_ref[0] = acc.astype(o_ref.dtype)


def matmul(a, w, out_dtype, res=None, gate=None, n_ctx=0, tm=256, tn=None):
    b, n, k = a.shape
    m = w.shape[1]
    tn = m if tn is None else tn
    residual = res is not None
    in_specs = [pl.BlockSpec((1, tm, k), lambda bi, i, j: (bi, i, 0)),
                pl.BlockSpec((k, tn), lambda bi, i, j: (0, j))]
    args = [a, w]
    if residual:
        in_specs += [pl.BlockSpec((1, tm, tn), lambda bi, i, j: (bi, i, j)),
                     pl.BlockSpec((1, 2, tn), lambda bi, i, j: (bi, 0, j))]
        args += [res, gate]
    return pl.pallas_call(
        functools.partial(_matmul_kernel, tm=tm, n_ctx=n_ctx, residual=residual),
        grid=(b, n // tm, m // tn),
        in_specs=in_specs,
        out_specs=pl.BlockSpec((1, tm, tn), lambda bi, i, j: (bi, i, j)),
        out_shape=jax.ShapeDtypeStruct((b, n, m), out_dtype),
        compiler_params=_params("parallel", "parallel", "arbitrary"),
        name="matmul_res" if residual else "matmul",
    )(*args)


def _rotate_pairs(x, axis):
    seg = lax.broadcasted_iota(jnp.int32, x.shape, axis) // 16
    n = x.shape[axis]
    return jnp.where(seg % 2 == 0, pltpu.roll(x, n - 16, axis), pltpu.roll(x, 16, axis))


def _da_qkv_kernel(h_ref, wqt_ref, wk_ref, wvt_ref, cq_ref, sq_ref, ck_ref, sk_ref,
                   qt_ref, k_ref, vt_ref, *, heads):
    h = h_ref[0]
    qt = _dot_nt(wqt_ref[...], h)
    cq, sq = cq_ref[...], sq_ref[...]
    for hd in range(heads):
        rs = slice(hd * LANES, (hd + 1) * LANES)
        x = qt[rs]
        qt_ref[0, rs, :] = (x * cq + _rotate_pairs(x, 0) * sq).astype(qt_ref.dtype)
    k = _dot(h, wk_ref[...])
    ck, sk = ck_ref[...], sk_ref[...]
    for hd in range(heads):
        cs = slice(hd * LANES, (hd + 1) * LANES)
        x = k[:, cs]
        k_ref[0, :, cs] = (x * ck + _rotate_pairs(x, 1) * sk).astype(k_ref.dtype)
    vt_ref[0, 0] = _dot_nt(wvt_ref[...], h).astype(vt_ref.dtype)


def da_qkv(h, wqkv, tabs, tm):
    b, n, d = h.shape
    m = wqkv.shape[1] // 3
    cq_t, sq_t, ck, sk = tabs
    wqt = wqkv[:, :m].T.astype(BF16)
    wk = wqkv[:, m:2 * m].astype(BF16)
    wvt = wqkv[:, 2 * m:].T.astype(BF16)
    full = lambda r, c: pl.BlockSpec((r, c), lambda bi, i: (0, 0))
    return pl.pallas_call(
        functools.partial(_da_qkv_kernel, heads=m // LANES),
        grid=(b, n // tm),
        in_specs=[pl.BlockSpec((1, tm, d), lambda bi, i: (bi, i, 0)),
                  full(m, d), full(d, m), full(m, d),
                  pl.BlockSpec((LANES, tm), lambda bi, i: (0, i)),
                  pl.BlockSpec((LANES, tm), lambda bi, i: (0, i)),
                  pl.BlockSpec((tm, LANES), lambda bi, i: (i, 0)),
                  pl.BlockSpec((tm, LANES), lambda bi, i: (i, 0))],
        out_specs=[pl.BlockSpec((1, m, tm), lambda bi, i: (bi, 0, i)),
                   pl.BlockSpec((1, tm, m), lambda bi, i: (bi, i, 0)),
                   pl.BlockSpec((1, 1, m, tm), lambda bi, i: (bi, i, 0, 0))],
        out_shape=[jax.ShapeDtypeStruct((b, m, n), BF16),
                   jax.ShapeDtypeStruct((b, n, m), BF16),
                   jax.ShapeDtypeStruct((b, n // tm, m, tm), BF16)],
        compiler_params=_params("parallel", "parallel"),
        name="da_qkv",
    )(h, wqt, wk, wvt, cq_t, sq_t, ck, sk)


def rope_tables(n_ctx, seq, q_scale):
    d = DA_HEAD_DIM
    t = jnp.arange(seq)
    rows = (t // GRID_W).astype(F32)
    cols = (t % GRID_W).astype(F32)
    quarter = d // 4
    freqs = ROPE_BASE ** (-jnp.arange(quarter, dtype=F32) / quarter)
    ang_r = rows[:, None] * freqs[None, :]
    ang_c = cols[:, None] * freqs[None, :]
    ang = jnp.concatenate([ang_r, ang_r, ang_c, ang_c], axis=-1)
    cos = jnp.cos(ang)
    sign = jnp.concatenate([-jnp.ones((quarter,), F32), jnp.ones((quarter,), F32)] * 2)
    sin = jnp.sin(ang) * sign[None, :]
    cos = jnp.concatenate([jnp.ones((n_ctx, d), F32), cos], axis=0)
    sin = jnp.concatenate([jnp.zeros((n_ctx, d), F32), sin], axis=0)
    cos = jnp.concatenate([cos, cos], axis=-1)
    sin = jnp.concatenate([sin, sin], axis=-1)
    return (cos * q_scale).T, (sin * q_scale).T, cos, sin


def _diff_attn_kernel(qt_ref, k_ref, vt_ref, lq1_ref, lk1_ref, lq2_ref, lk2_ref, sub_ref, o_ref,
                      acc_ref, s_ref, *, tk, kw, nchunks, lam_init):
    qt = qt_ref[0]
    tq = qt.shape[1]
    sub = lax.broadcasted_iota(jnp.int32, qt.shape, 0)
    zero = jnp.zeros_like(qt)
    maps = (jnp.where(sub < DA_HEAD_DIM, qt, zero), jnp.where(sub >= DA_HEAD_DIM, qt, zero))
    acc_ref[...] = jnp.zeros(acc_ref.shape, F32)
    half = kw // 2

    def col_reduce(x, op):
        return op(op(x.reshape(8, x.shape[0] // 8, x.shape[1]), axis=0), axis=0, keepdims=True)

    def scores(c, slot):
        start = c * tk if isinstance(c, int) else pl.multiple_of(c * tk, tk)
        for g in range(2):
            for hf in range(2):
                k = k_ref[0, pl.ds(start + hf * half, half), :]
                s_ref[slot, g, hf * half:(hf + 1) * half, :] = _dot(k, maps[g])

    def softmax_pv(c, slot, carry):
        vt = vt_ref[0, c]
        new = []
        for g in range(2):
            m_old, l_old = carry[g]
            s = s_ref[slot, g]
            m_new = jnp.maximum(m_old, col_reduce(s, jnp.max))
            alpha = jnp.exp2(m_old - m_new)
            p = jnp.exp2(s - m_new)
            l_new = alpha * l_old + col_reduce(p, jnp.sum)
            pb = p.astype(BF16)
            acc_ref[g] = (alpha * acc_ref[g] + _dot(vt[:, :half], pb[:half])
                          + _dot(vt[:, half:kw], pb[half:]))
            new.append((m_new, l_new))
        return tuple(new)

    def body(i, carry):
        c = 2 * i
        scores(c + 1, 1)
        carry = softmax_pv(c, 0, carry)
        scores(c + 2, 0)
        return softmax_pv(c + 1, 1, carry)

    init = (jnp.full((1, tq), -jnp.inf, F32), jnp.zeros((1, tq), F32))
    scores(0, 0)
    npairs = (nchunks - 1) // 2
    carry = lax.fori_loop(0, npairs, body, (init, init))
    if (nchunks - 1) % 2:
        scores(nchunks - 1, 1)
        carry = softmax_pv(nchunks - 2, 0, carry)
        carry = softmax_pv(nchunks - 1, 1, carry)
    else:
        carry = softmax_pv(nchunks - 1, 0, carry)
    (_, l0), (_, l1) = carry
    lam = (jnp.exp(jnp.sum(lq1_ref[...] * lk1_ref[...], axis=-1, keepdims=True))
           - jnp.exp(jnp.sum(lq2_ref[...] * lk2_ref[...], axis=-1, keepdims=True)) + lam_init)
    od = acc_ref[0] / l0 - lam * (acc_ref[1] / l1)
    y = od * lax.rsqrt(jnp.mean(od * od, axis=0, keepdims=True) + NORM_EPS) * sub_ref[...]
    o_ref[0] = (y * (1.0 - lam_init)).T.astype(o_ref.dtype)


def diff_attention(qt, k, vt, lam_params, subln, lam_init, q_start, n_q, n_k, tq=256):
    b, m, n = qt.shape
    h = m // LANES
    tk = vt.shape[-1]
    kw = min(tk, n_k)
    nchunks = n_k // kw
    qoff = q_start // tq
    vec = lambda a: a.reshape(1, -1).astype(F32)
    small = pl.BlockSpec((1, DA_HEAD_DIM), lambda bi, hi, i: (0, 0))
    return pl.pallas_call(
        functools.partial(_diff_attn_kernel, tk=tk, kw=kw, nchunks=nchunks, lam_init=lam_init),
        grid=(b, h, n_q // tq),
        in_specs=[pl.BlockSpec((1, LANES, tq), lambda bi, hi, i: (bi, hi, i + qoff)),
                  pl.BlockSpec((1, n_k, LANES), lambda bi, hi, i: (bi, 0, hi)),
                  pl.BlockSpec((1, nchunks, LANES, tk), lambda bi, hi, i: (bi, 0, hi, 0)),
                  small, small, small, small,
                  pl.BlockSpec((LANES, 1), lambda bi, hi, i: (0, 0))],
        out_specs=pl.BlockSpec((1, tq, LANES), lambda bi, hi, i: (bi, i, hi)),
        out_shape=jax.ShapeDtypeStruct((b, n_q, m), BF16),
        scratch_shapes=[pltpu.VMEM((2, LANES, tq), F32), pltpu.VMEM((2, 2, kw, tq), F32)],
        compiler_params=_params("parallel", "parallel", "arbitrary"),
        name="diff_attention",
    )(qt, k, vt, *[vec(p) for p in lam_params], subln.reshape(-1, 1).astype(F32))


def _pair_attention(qp, kp, vp, bias, n_bias):
    tq = qp.shape[0]
    lane = lax.broadcasted_iota(jnp.int32, qp.shape, 1)
    zero = jnp.zeros_like(qp)
    qs = jnp.concatenate([jnp.where(lane < 64, qp, zero), jnp.where(lane >= 64, qp, zero)], axis=0)
    s = _dot_nt(qs, kp)
    if n_bias:
        s_w = s[:, :n_bias] + bias
        s_c = s[:, n_bias:]
        m = jnp.maximum(jnp.max(s_w, axis=-1, keepdims=True), jnp.max(s_c, axis=-1, keepdims=True))
        p_w = jnp.exp(s_w - m)
        p_c = jnp.exp(s_c - m)
        l = jnp.sum(p_w, axis=-1, keepdims=True) + jnp.sum(p_c, axis=-1, keepdims=True)
        pv = _dot(p_w.astype(BF16), vp[:n_bias]) + _dot(p_c.astype(BF16), vp[n_bias:])
    else:
        m = jnp.max(s, axis=-1, keepdims=True)
        p = jnp.exp(s - m)
        l = jnp.sum(p, axis=-1, keepdims=True)
        pv = _dot(p.astype(BF16), vp)
    o = pv / l
    lane_o = lax.broadcasted_iota(jnp.int32, (tq, LANES), 1)
    return jnp.where(lane_o < 64, o[:tq], o[tq:])


def _na_kernel(*refs, wr, n_ctx, scale):
    q_ref = refs[0]
    k_refs = refs[1:1 + wr]
    v_refs = refs[1 + wr:1 + 2 * wr]
    kc_ref, vc_ref, bias_ref, o_ref, kbuf, vbuf = refs[1 + 2 * wr:]
    nw = wr * GRID_W
    for w in range(wr):
        kbuf[w * GRID_W:(w + 1) * GRID_W, :] = k_refs[w][0]
        vbuf[w * GRID_W:(w + 1) * GRID_W, :] = v_refs[w][0]
    kbuf[nw:nw + n_ctx, :] = kc_ref[0]
    vbuf[nw:nw + n_ctx, :] = vc_ref[0]
    for hp in range(NA_HEADS // 2):
        cs = slice(hp * LANES, (hp + 1) * LANES)
        qp = q_ref[0, :, cs] * scale
        bias = bias_ref[0, 2 * hp:2 * hp + 2].reshape(2 * GRID_W, nw)
        o = _pair_attention(qp, kbuf[:, cs], vbuf[:, cs], bias, nw)
        o_ref[0, :, cs] = o.astype(o_ref.dtype)


def na_bias_table(rpb, rows):
    wr = min(NA_ROWS, rows)
    qcol = np.arange(GRID_W)
    col_start = np.clip(qcol - NA_COLS // 2, 0, GRID_W - NA_COLS)
    kc = np.arange(GRID_W)
    inside = (kc[None, :] >= col_start[:, None]) & (kc[None, :] < col_start[:, None] + NA_COLS)
    dc = np.clip(kc[None, :] - qcol[:, None] + NA_COLS - 1, 0, 2 * NA_COLS - 2)
    dr = np.arange(wr)[None, :] - np.arange(wr)[:, None] + NA_ROWS - 1
    sel_r = (dr[:, :, None] == np.arange(2 * NA_ROWS - 1)).astype(np.float32)
    sel_c = (dc[:, :, None] == np.arange(2 * NA_COLS - 1)).astype(np.float32)
    tab = jnp.einsum("cwa,hab,qkb->chqwk", sel_r, rpb.astype(F32), sel_c,
                     precision=lax.Precision.HIGHEST)
    tab = jnp.where(inside[None, None, :, None, :], tab, MASK_VALUE)
    return tab.reshape(wr, rpb.shape[0], GRID_W, wr * GRID_W)


def neighbourhood_attention(qkv, bias_tab, n_ctx):
    b, n, m3 = qkv.shape
    m = m3 // 3
    rows = (n - n_ctx) // GRID_W
    wr = min(NA_ROWS, rows)
    cb = n_ctx // GRID_W
    rs = lambda r: jnp.clip(r - wr // 2, 0, rows - wr)
    blk = lambda f: pl.BlockSpec((1, GRID_W, m), f)
    in_specs = [blk(lambda bi, r: (bi, cb + r, 0))]
    in_specs += [blk(functools.partial(lambda bi, r, w: (bi, cb + rs(r) + w, 1), w=w)) for w in range(wr)]
    in_specs += [blk(functools.partial(lambda bi, r, w: (bi, cb + rs(r) + w, 2), w=w)) for w in range(wr)]
    in_specs += [pl.BlockSpec((1, n_ctx, m), lambda bi, r: (bi, 0, 1)),
                 pl.BlockSpec((1, n_ctx, m), lambda bi, r: (bi, 0, 2)),
                 pl.BlockSpec((1, NA_HEADS, GRID_W, wr * GRID_W), lambda bi, r: (r - rs(r), 0, 0, 0))]
    nk = wr * GRID_W + n_ctx
    return pl.pallas_call(
        functools.partial(_na_kernel, wr=wr, n_ctx=n_ctx, scale=(m // NA_HEADS) ** -0.5),
        grid=(b, rows),
        in_specs=in_specs,
        out_specs=pl.BlockSpec((1, GRID_W, m), lambda bi, r: (bi, r, 0)),
        out_shape=jax.ShapeDtypeStruct((b, n - n_ctx, m), BF16),
        scratch_shapes=[pltpu.VMEM((nk, m), BF16), pltpu.VMEM((nk, m), BF16)],
        compiler_params=_params("parallel", "arbitrary"),
        name="neighbourhood_attention",
    )(*([qkv] * (1 + 2 * wr + 2)), bias_tab)


def _ctx_attn_kernel(q_ref, k_ref, v_ref, o_ref, *, scale):
    for hp in range(NA_HEADS // 2):
        cs = slice(hp * LANES, (hp + 1) * LANES)
        o = _pair_attention(q_ref[0, :, cs] * scale, k_ref[0, :, cs], v_ref[0, :, cs], None, 0)
        o_ref[0, :, cs] = o.astype(o_ref.dtype)


def context_attention(qkv, n_ctx):
    b, n, m3 = qkv.shape
    m = m3 // 3
    return pl.pallas_call(
        functools.partial(_ctx_attn_kernel, scale=(m // NA_HEADS) ** -0.5),
        grid=(b,),
        in_specs=[pl.BlockSpec((1, n_ctx, m), lambda bi: (bi, 0, 0)),
                  pl.BlockSpec((1, n_ctx, m), lambda bi: (bi, 0, 1)),
                  pl.BlockSpec((1, n_ctx, m), lambda bi: (bi, 0, 2))],
        out_specs=pl.BlockSpec((1, n_ctx, m), lambda bi: (bi, 0, 0)),
        out_shape=jax.ShapeDtypeStruct((b, n_ctx, m), BF16),
        compiler_params=_params("parallel"),
        name="context_attention",
    )(qkv, qkv, qkv)


def _conv_silu_kernel(x_ref, w_ref, b_ref, o_ref, *, n, n_ctx):
    x = x_ref[0]
    row = lax.broadcasted_iota(jnp.int32, x.shape, 0)
    xm = jnp.where((row == 0) | (row == n_ctx), 0.0, pltpu.roll(x, 1, 0))
    xp = jnp.where((row == n_ctx - 1) | (row == n - 1), 0.0, pltpu.roll(x, n - 1, 0))
    y = w_ref[0:1, :] * xm + w_ref[1:2, :] * x + w_ref[2:3, :] * xp + b_ref[...]
    o_ref[0] = y * jax.nn.sigmoid(y)


def conv_silu(x, w, bias, n_ctx):
    b, n, c = x.shape
    return pl.pallas_call(
        functools.partial(_conv_silu_kernel, n=n, n_ctx=n_ctx),
        grid=(b, c // LANES),
        in_specs=[pl.BlockSpec((1, n, LANES), lambda bi, j: (bi, 0, j)),
                  pl.BlockSpec((3, LANES), lambda bi, j: (0, j)),
                  pl.BlockSpec((1, LANES), lambda bi, j: (0, j))],
        out_specs=pl.BlockSpec((1, n, LANES), lambda bi, j: (bi, 0, j)),
        out_shape=jax.ShapeDtypeStruct((b, n, c), F32),
        compiler_params=_params("parallel", "parallel"),
        name="conv_silu",
    )(x, w, bias.reshape(1, c))


def _softplus(x):
    return jnp.maximum(x, 0.0) + jnp.log1p(jnp.exp(-jnp.abs(x)))


def _ssd_kernel(x_ref, b_ref, c_ref, dt_ref, dtb_ref, alog_ref, e_ref, y_ref, state_ref,
                *, direction, nheads):
    L = SSM_CHUNK
    P = SSM_HEAD_DIM
    G = SSM_GROUPS
    hpg = nheads // G
    reverse = direction == 1

    @pl.when(pl.program_id(1) == 0)
    def _():
        state_ref[...] = jnp.zeros(state_ref.shape, F32)

    li = lax.broadcasted_iota(jnp.int32, (L, L), 0)
    si = lax.broadcasted_iota(jnp.int32, (L, L), 1)
    allowed = (si >= li) if reverse else (si <= li)
    tri = jnp.where(allowed, 1.0, 0.0).astype(BF16)
    tri_t = jnp.where((li >= si) if reverse else (li <= si), 1.0, 0.0).astype(BF16)

    dt_raw = dt_ref[0][:, direction * nheads:(direction + 1) * nheads]
    dt = _softplus(dt_raw + dtb_ref[...])
    a = dt * (-jnp.exp(alog_ref[...]))
    a3 = _split3(a)
    acs = sum(_dot(tri, p) for p in a3)
    acs_t = sum(_dot_tn(p, tri_t) for p in a3)
    expand = e_ref[...]
    acs_x = sum(_dot(p, expand) for p in _split3(acs))
    dt_x = sum(_dot(p, expand) for p in _split3(dt))
    end = 0 if reverse else L - 1
    total_x = acs_x[end:end + 1, :]

    xd = x_ref[0] * dt_x
    xd_b = xd.astype(BF16)
    x_state = (xd * jnp.exp(total_x - acs_x)).astype(BF16)
    exp_acs_x = jnp.exp(acs_x)
    chunk_decay_x = jnp.exp(total_x)
    lane = lax.broadcasted_iota(jnp.int32, (L, LANES), 1)

    for g in range(G):
        gs = slice(g * SSM_STATE, (g + 1) * SSM_STATE)
        hs = slice(g * hpg * P, (g + 1) * hpg * P)
        cg = c_ref[0][:, gs].astype(BF16)
        bg = b_ref[0][:, gs].astype(BF16)
        cb = _dot_nt(cg, bg)
        h_prev = state_ref[:, hs]
        y_off = _dot(cg, h_prev.astype(BF16)) * exp_acs_x[:, hs]
        state_ref[:, hs] = h_prev * chunk_decay_x[:, hs] + _dot_tn(bg, x_state[:, hs])
        for rp in range(hpg // 2):
            mats = []
            for r in (2 * rp, 2 * rp + 1):
                hidx = g * hpg + r
                seg = acs[:, hidx:hidx + 1] - acs_t[hidx:hidx + 1, :]
                mats.append((cb * jnp.exp(jnp.where(allowed, seg, -jnp.inf))).astype(BF16))
            ps = slice((g * hpg + 2 * rp) * P, (g * hpg + 2 * rp + 2) * P)
            yd = _dot(jnp.concatenate(mats, axis=0), xd_b[:, ps])
            y_ref[0, :, ps] = jnp.where(lane < P, yd[:L], yd[L:]) + y_off[:, 2 * rp * P:(2 * rp + 2) * P]


def ssd_scan(xbc, dt_raw, dt_bias, a_log, expand, direction, nheads, n_ctx):
    b, n, _ = xbc.shape
    d_inner = nheads * SSM_HEAD_DIM
    gn = SSM_GROUPS * SSM_STATE
    L = SSM_CHUNK
    nc = n // L
    ncc = n_ctx // L
    if direction == 0:
        cidx = lambda c: c
    else:
        cidx = lambda c: jnp.where(c < ncc, ncc - 1 - c, nc - 1 - (c - ncc))
    return pl.pallas_call(
        functools.partial(_ssd_kernel, direction=direction, nheads=nheads),
        grid=(b, nc),
        in_specs=[pl.BlockSpec((1, L, d_inner), lambda bi, c: (bi, cidx(c), 0)),
                  pl.BlockSpec((1, L, gn), lambda bi, c: (bi, cidx(c), d_inner // gn)),
                  pl.BlockSpec((1, L, gn), lambda bi, c: (bi, cidx(c), d_inner // gn + 1)),
                  pl.BlockSpec((1, L, LANES), lambda bi, c: (bi, cidx(c), 0)),
                  pl.BlockSpec((1, nheads), lambda bi, c: (0, 0)),
                  pl.BlockSpec((1, nheads), lambda bi, c: (0, 0)),
                  pl.BlockSpec((nheads, d_inner), lambda bi, c: (0, 0))],
        out_specs=pl.BlockSpec((1, L, d_inner), lambda bi, c: (bi, cidx(c), 0)),
        out_shape=jax.ShapeDtypeStruct((b, n, d_inner), F32),
        scratch_shapes=[pltpu.VMEM((SSM_STATE, d_inner), F32)],
        compiler_params=_params("parallel", "arbitrary"),
        name="ssd_scan_%d" % direction,
    )(xbc, xbc, xbc, dt_raw, dt_bias[direction:direction + 1], a_log[direction:direction + 1], expand)


def _ssm_out_kernel(y0_ref, y1_ref, x_ref, z_ref, d_ref, nw_ref, w_ref, res_ref, g_ref, o_ref,
                    *, tm, n_ctx):
    z = z_ref[0]
    u = (y0_ref[0] + y1_ref[0] + d_ref[...] * x_ref[0]) * (z * jax.nn.sigmoid(z))
    un = u * lax.rsqrt(jnp.mean(u * u, axis=-1, keepdims=True) + NORM_EPS) * nw_ref[...]
    g = _row_select(pl.program_id(1), tm, n_ctx, g_ref[0])
    o_ref[0] = res_ref[0] + g * _dot(un.astype(BF16), w_ref[...])


def ssm_out(y0, y1, xbc, z, d_chan, norm_w, w_out, res, gate, n_ctx, tm=256):
    b, n, di = y0.shape
    d = w_out.shape[1]
    row = lambda c: pl.BlockSpec((1, tm, c), lambda bi, i: (bi, i, 0))
    vec = lambda c: pl.BlockSpec((1, c), lambda bi, i: (0, 0))
    return pl.pallas_call(
        functools.partial(_ssm_out_kernel, tm=tm, n_ctx=n_ctx),
        grid=(b, n // tm),
        in_specs=[row(di), row(di), row(di), row(di), vec(di), vec(di),
                  pl.BlockSpec((di, d), lambda bi, i: (0, 0)), row(d),
                  pl.BlockSpec((1, 2, d), lambda bi, i: (bi, 0, 0))],
        out_specs=row(d),
        out_shape=jax.ShapeDtypeStruct((b, n, d), F32),
        compiler_params=_params("parallel", "parallel"),
        name="ssm_out",
    )(y0, y1, xbc, z, d_chan, norm_w.reshape(1, di), w_out, res, gate)


def _cumsum_lanes(x):
    n = x.shape[-1]
    lane = lax.broadcasted_iota(jnp.int32, x.shape, x.ndim - 1)
    s = 1
    while s < n:
        x = x + jnp.where(lane >= s, pltpu.roll(x, s, x.ndim - 1), 0.0)
        s *= 2
    return x


def _route_set(aff, cap):
    e, t = aff.shape
    bits = lax.bitcast_convert_type(aff, jnp.int32)

    def count_ge(thr):
        return jnp.sum(jnp.where(bits >= thr, 1.0, 0.0), axis=1, keepdims=True)

    def step(_, lohi):
        lo, hi = lohi
        mid = lo + (hi - lo) // 2
        ok = count_ge(mid) >= cap
        return jnp.where(ok, mid, lo), jnp.where(ok, hi, mid)

    lo0 = jnp.zeros((e, 1), jnp.int32)
    hi0 = jnp.full((e, 1), ONE_BITS + 1, jnp.int32)
    thr, _ = lax.fori_loop(0, 31, step, (lo0, hi0))
    gt = jnp.where(bits > thr, 1.0, 0.0)
    eq = jnp.where(bits == thr, 1.0, 0.0)
    need = cap - jnp.sum(gt, axis=1, keepdims=True)
    sel = gt + eq * jnp.where(_cumsum_lanes(eq) <= need, 1.0, 0.0)
    return sel, _cumsum_lanes(sel)


def _route_kernel(aff_ref, idx_ref, cnt_ref, *, sets, jblk):
    n_exp = aff_ref.shape[1]
    for t0, t, cap, row0 in sets:
        _, cnt = _route_set(aff_ref[0, :, t0:t0 + t], cap)
        cnt_ref[:, :t] = cnt
        for jb in range(0, cap, jblk):
            nj = min(jblk, cap - jb)
            jcol = (lax.broadcasted_iota(jnp.int32, (nj, 1), 0) + jb).astype(F32)
            lane = lax.broadcasted_iota(jnp.int32, (nj, LANES), 1)

            def per_expert(ei, acc):
                row = cnt_ref[pl.ds(ei, 1), :t]
                pos = jnp.sum(jnp.where(row <= jcol, 1.0, 0.0), axis=1, keepdims=True)
                return jnp.where(lane == ei, pos, acc)

            acc = lax.fori_loop(0, n_exp, per_expert, jnp.zeros((nj, LANES), F32))
            idx_ref[0, row0 + jb:row0 + jb + nj, :] = acc.astype(jnp.int32) + t0


def route(aff, sets):
    b, e, n = aff.shape
    m = sum(s[2] for s in sets)
    tmax = max(s[1] for s in sets)
    rows, sets_r = 0, []
    for t0, t, cap in sets:
        sets_r.append((t0, t, cap, rows))
        rows += cap
    return pl.pallas_call(
        functools.partial(_route_kernel, sets=tuple(sets_r), jblk=256),
        grid=(b,),
        in_specs=[pl.BlockSpec((1, e, n), lambda bi: (bi, 0, 0))],
        out_specs=pl.BlockSpec((1, m, LANES), lambda bi: (bi, 0, 0)),
        out_shape=jax.ShapeDtypeStruct((b, m, LANES), jnp.int32),
        scratch_shapes=[pltpu.VMEM((e, tmax), F32)],
        compiler_params=_params("parallel"),
        name="route",
    )(aff)


def _expert_ffn_kernel(idx_ref, hp_ref, afft_ref, g_ref, w1_ref, w3_ref, w2_ref, o_ref,
                       xw_ref, xs_ref, gate_ref, *, m, n_lat):
    ei = pl.program_id(1)
    f = pl.program_id(2)

    @pl.when(f == 0)
    def _():
        def gather(g, c):
            base = pl.multiple_of(g * 8, 8)
            for r in range(8):
                i = idx_ref[0, 0, 0, base + r]
                xw_ref[pl.ds(base + r, 1), :] = hp_ref[0, pl.ds(i, 1), :]
                gate_ref[pl.ds(base + r, 1), :] = afft_ref[0, pl.ds(i, 1), :]
            return c

        lax.fori_loop(0, m // 8, gather, 0)
        w = xw_ref[...]
        lo = lax.bitcast_convert_type(w << 16, F32).astype(BF16)
        hi = lax.bitcast_convert_type(w & jnp.uint32(0xFFFF0000), F32).astype(BF16)
        xs_ref[...] = jnp.concatenate([lo, hi], axis=1)
        lane = lax.broadcasted_iota(jnp.int32, gate_ref.shape, 1)
        gate = jnp.sum(jnp.where(lane == ei, gate_ref[...], 0.0), axis=1, keepdims=True)
        gate_ref[...] = jnp.broadcast_to(gate, gate_ref.shape)

    x = xs_ref[...]
    h1 = _dot(x, w1_ref[0].astype(BF16))
    h3 = _dot(x, w3_ref[0].astype(BF16))
    hid = (h1 * jax.nn.sigmoid(h1) * h3).astype(BF16)
    part = _dot(hid, w2_ref[0].astype(BF16))

    @pl.when(f == 0)
    def _():
        o_ref[0, 0] = part

    @pl.when(f > 0)
    def _():
        o_ref[0, 0] = o_ref[0, 0] + part

    @pl.when(f == pl.num_programs(2) - 1)
    def _():
        row = lax.broadcasted_iota(jnp.int32, (m, 1), 0)
        g = jnp.where(row < n_lat, g_ref[0, 1:2, :], g_ref[0, 0:1, :])
        o_ref[0, 0] = o_ref[0, 0] * gate_ref[:, 0:1] * g


def expert_ffn(idx, hp, afft, gate2, w1, w3, w2, layer, n_lat, tf=256):
    b, e, m = idx.shape
    n, dh = hp.shape[1:]
    d = 2 * dh
    ff = w1.shape[3]
    single = dict(pipeline_mode=pl.Buffered(1))
    return pl.pallas_call(
        functools.partial(_expert_ffn_kernel, m=m, n_lat=n_lat),
        grid=(b, e, ff // tf),
        in_specs=[pl.BlockSpec((1, 1, 1, m), lambda bi, ei, f: (bi, ei, 0, 0), memory_space=pltpu.SMEM),
                  pl.BlockSpec((1, n, dh), lambda bi, ei, f: (bi, 0, 0), **single),
                  pl.BlockSpec((1, n, LANES), lambda bi, ei, f: (bi, 0, 0), **single),
                  pl.BlockSpec((1, 2, d), lambda bi, ei, f: (bi, 0, 0)),
                  pl.BlockSpec((None, 1, d, tf), lambda bi, ei, f: (layer, ei, 0, f)),
                  pl.BlockSpec((None, 1, d, tf), lambda bi, ei, f: (layer, ei, 0, f)),
                  pl.BlockSpec((None, 1, tf, d), lambda bi, ei, f: (layer, ei, f, 0))],
        out_specs=pl.BlockSpec((1, 1, m, d), lambda bi, ei, f: (bi, ei, 0, 0)),
        out_shape=jax.ShapeDtypeStruct((b, e, m, d), F32),
        scratch_shapes=[pltpu.VMEM((m, dh), jnp.uint32), pltpu.VMEM((m, d), BF16),
                        pltpu.VMEM((m, LANES), F32)],
        compiler_params=_params("parallel", "arbitrary", "arbitrary"),
        name="expert_ffn",
    )(idx.reshape(b, e, 1, m), hp, afft, gate2, w1, w3, w2)


def _scatter_kernel(idx_ref, x_ref, y_ref, o_ref, *, m):
    ei = pl.program_id(2)

    @pl.when(ei == 0)
    def _():
        o_ref[...] = x_ref[...]

    def add_rows(g, c):
        base = pl.multiple_of(g * 8, 8)
        rows = [idx_ref[0, 0, 0, base + r] for r in range(8)]
        sums = [o_ref[0, pl.ds(i, 1), :] + y_ref[0, 0, pl.ds(base + r, 1), :] for r, i in enumerate(rows)]
        for i, v in zip(rows, sums):
            o_ref[0, pl.ds(i, 1), :] = v
        return c

    lax.fori_loop(0, m // 8, add_rows, 0)


def moe_scatter(idx, x, y, tc=512):
    b, e, m = idx.shape
    n, d = x.shape[1:]
    return pl.pallas_call(
        functools.partial(_scatter_kernel, m=m),
        grid=(b, d // tc, e),
        in_specs=[pl.BlockSpec((1, 1, 1, m), lambda bi, ci, ei: (bi, ei, 0, 0), memory_space=pltpu.SMEM),
                  pl.BlockSpec((1, n, tc), lambda bi, ci, ei: (bi, 0, ci), pipeline_mode=pl.Buffered(1)),
                  pl.BlockSpec((1, 1, m, tc), lambda bi, ci, ei: (bi, ei, 0, ci))],
        out_specs=pl.BlockSpec((1, n, tc), lambda bi, ci, ei: (bi, 0, ci), pipeline_mode=pl.Buffered(1)),
        out_shape=jax.ShapeDtypeStruct((b, n, d), F32),
        compiler_params=_params("parallel", "parallel", "arbitrary"),
        name="moe_scatter",
    )(idx.reshape(b, e, 1, m), x, y)


def _lambda_init(layer):
    return 0.8 - 0.6 * math.exp(-0.3 * layer)


def kernel(x, c, ctx, c_ctx, ada_w, ada_b, norm_mix, norm_ffn, final_norm_w,
           da_wqkv, da_wo, da_lam_q1, da_lam_k1, da_lam_q2, da_lam_k2, da_subln,
           ssm_w_in, ssm_conv_w, ssm_conv_b, ssm_dt_bias, ssm_A_log, ssm_D, ssm_norm, ssm_w_out,
           na_wqkv, na_wo, na_rpb, router_w, exp_w1, exp_w3, exp_w2):
    b, seq, d = x.shape
    n_ctx = ctx.shape[1]
    n = n_ctx + seq
    depth = ada_w.shape[0]
    n_exp = router_w.shape[-1]

    cond_rows = jnp.zeros((8, d), F32).at[:b].set(c).at[b].set(c_ctx)
    mods = ada_modulation(cond_rows, ada_w, ada_b).reshape(depth, 8, 6, d)
    mods = jnp.stack([jnp.broadcast_to(mods[:, b:b + 1], (depth, b, 6, d)), mods[:, :b]], axis=2)

    xj = jnp.concatenate([ctx, x], axis=1)
    rope_tabs = rope_tables(n_ctx, seq, DA_HEAD_DIM ** -0.5 * math.log2(math.e))

    ia = ib = ic = 0
    for layer in range(depth):
        need_ctx = layer < depth - 1
        sh_m, sc_m, g_m, sh_f, sc_f, g_f = [mods[layer, :, :, k] for k in range(6)]
        h = norm_mod(xj, norm_mix[layer], sh_m, sc_m, n_ctx)
        kind = layer % N_MIXERS
        if kind == 0:
            qt, k, vt = da_qkv(h, da_wqkv[ia], rope_tabs, DA_KV_CHUNK)
            lam_params = (da_lam_q1[ia], da_lam_k1[ia], da_lam_q2[ia], da_lam_k2[ia])
            li = _lambda_init(layer)
            o_l = diff_attention(qt, k, vt, lam_params, da_subln[ia], li, n_ctx, seq, n)
            o_c = diff_attention(qt, k, vt, lam_params, da_subln[ia], li, 0, n_ctx, n_ctx)
            o = jnp.concatenate([o_c, o_l], axis=1)
            xj = matmul(o, da_wo[ia].astype(BF16), F32, res=xj, gate=g_m, n_ctx=n_ctx)
            ia += 1
        elif kind == 1:
            nh = ssm_A_log.shape[-1]
            di = nh * SSM_HEAD_DIM
            conv_dim = ssm_conv_w.shape[-1]
            w_in = ssm_w_in[ib].astype(BF16)
            z = matmul(h, w_in[:, :di], F32, tn=1024)
            xbc = matmul(h, w_in[:, di:di + conv_dim], F32, tn=1024)
            w_dt = jnp.zeros((d, LANES), BF16).at[:, :2 * nh].set(w_in[:, di + conv_dim:])
            dt_raw = matmul(h, w_dt, F32)
            xbc = conv_silu(xbc, ssm_conv_w[ib], ssm_conv_b[ib], n_ctx)
            expand = (jnp.arange(di)[None, :] // SSM_HEAD_DIM == jnp.arange(nh)[:, None]).astype(BF16)
            ys = [ssd_scan(xbc, dt_raw, ssm_dt_bias[ib], ssm_A_log[ib], expand, dirn, nh, n_ctx)
                  for dirn in range(2)]
            d_chan = jnp.repeat(ssm_D[ib, 0] + ssm_D[ib, 1], SSM_HEAD_DIM).reshape(1, di)
            xj = ssm_out(ys[0], ys[1], xbc, z, d_chan, ssm_norm[ib], ssm_w_out[ib].astype(BF16),
                         xj, g_m, n_ctx)
            ib += 1
        else:
            qkv = matmul(h, na_wqkv[ic].astype(BF16), BF16, tn=1024)
            bias_tab = na_bias_table(na_rpb[ic], seq // GRID_W)
            o_l = neighbourhood_attention(qkv, bias_tab, n_ctx)
            o_c = context_attention(qkv, n_ctx)
            o = jnp.concatenate([o_c, o_l], axis=1)
            xj = matmul(o, na_wo[ic].astype(BF16), F32, res=xj, gate=g_m, n_ctx=n_ctx)
            ic += 1

        hp, aff, afft = norm_mod(xj, norm_ffn[layer], sh_f, sc_f, n_ctx, router_w=router_w[layer])
        sets = [(n_ctx, seq, EC_CAPACITY_FACTOR * seq // n_exp)]
        if need_ctx:
            sets.append((0, n_ctx, EC_CAPACITY_FACTOR * n_ctx // n_exp))
        idx = jnp.swapaxes(route(aff, sets)[:, :, :n_exp], 1, 2)
        y = expert_ffn(idx, hp, afft, g_f, exp_w1, exp_w3, exp_w2, layer, sets[0][2])
        xj = moe_scatter(idx, xj, y)

    return final_norm(xj, final_norm_w, n_ctx)
```

```python
import functools
import math

import jax
import jax.numpy as jnp
import numpy as np
from jax import lax
from jax.experimental import pallas as pl
from jax.experimental.pallas import tpu as pltpu

F32 = jnp.float32
BF16 = jnp.bfloat16

GRID_W = 64
NORM_EPS = 1e-6
ROPE_BASE = 10000.0
DA_HEADS = 8
DA_HEAD_DIM = 64
DA_KV_CHUNK = 768
DA_GROUP_LANES = 256
DA_Q_TILE = 512
DA_SCORE_BUFFERS = 3
DA_SUM_ROWS = 16
SSM_HEAD_DIM = 64
SSM_GROUPS = 4
SSM_STATE = 128
SSM_CHUNK = 128
NA_HEADS = 16
NA_ROWS = 8
NA_COLS = 16
N_MIXERS = 3
EC_CAPACITY_FACTOR = 2
MASK_VALUE = -1e30
ONE_BITS = 0x3F800000
LANES = 128
VMEM_LIMIT = 56 * 1024 * 1024


def _params(*sem):
    return pltpu.CompilerParams(dimension_semantics=sem, vmem_limit_bytes=VMEM_LIMIT)


def _split3(a):
    hi = a.astype(BF16)
    r1 = a - hi.astype(F32)
    mid = r1.astype(BF16)
    lo = (r1 - mid.astype(F32)).astype(BF16)
    return hi, mid, lo


def _dot(a, b):
    return jnp.dot(a, b, preferred_element_type=F32)


def _dot_nt(a, b):
    return lax.dot_general(a, b, (((1,), (1,)), ((), ())), preferred_element_type=F32)


def _dot_tn(a, b):
    return lax.dot_general(a, b, (((0,), (0,)), ((), ())), preferred_element_type=F32)


def _ada_kernel(c_ref, w_ref, b_ref, o_ref):
    c = c_ref[...]
    cond = c * jax.nn.sigmoid(c)
    o_ref[0] = jnp.dot(cond, w_ref[0], preferred_element_type=F32,
                       precision=lax.Precision.HIGHEST) + b_ref[0]


def ada_modulation(cond_rows, ada_w, ada_b):
    depth, d, n6 = ada_w.shape
    tn = 1536
    return pl.pallas_call(
        _ada_kernel,
        grid=(depth, n6 // tn),
        in_specs=[pl.BlockSpec((8, d), lambda l, j: (0, 0)),
                  pl.BlockSpec((1, d, tn), lambda l, j: (l, 0, j)),
                  pl.BlockSpec((1, 1, tn), lambda l, j: (l, 0, j))],
        out_specs=pl.BlockSpec((1, 8, tn), lambda l, j: (l, 0, j)),
        out_shape=jax.ShapeDtypeStruct((depth, 8, n6), F32),
        compiler_params=_params("parallel", "parallel"),
        name="ada_modulation",
    )(cond_rows, ada_w, ada_b.reshape(depth, 1, n6))


def _row_select(i, tm, n_ctx, vec2):
    row = i * tm + lax.broadcasted_iota(jnp.int32, (tm, 1), 0)
    return jnp.where(row >= n_ctx, vec2[1:2, :], vec2[0:1, :])


def _norm_mod_kernel(x_ref, w_ref, sh_ref, sc_ref, *rest, tm, n_ctx, n_exp):
    i = pl.program_id(1)
    xf = x_ref[0]
    y = xf * lax.rsqrt(jnp.mean(xf * xf, axis=-1, keepdims=True) + NORM_EPS) * w_ref[...]
    sc = _row_select(i, tm, n_ctx, sc_ref[0])
    sh = _row_select(i, tm, n_ctx, sh_ref[0])
    h = y * (1.0 + sc) + sh
    if not n_exp:
        (h_ref,) = rest
        h_ref[0] = h.astype(h_ref.dtype)
        return
    rwt_ref, rwp_ref, hp_ref, aff_ref, afft_ref = rest
    bits = lax.bitcast_convert_type(h.astype(BF16).astype(F32), jnp.uint32)
    dh = bits.shape[1] // 2
    hp_ref[0] = (bits[:, :dh] >> 16) | bits[:, dh:]
    logits = lax.dot_general(rwt_ref[...], h, (((1,), (1,)), ((), ())),
                             preferred_element_type=F32, precision=lax.Precision.HIGHEST)
    e = jnp.exp(logits - jnp.max(logits, axis=0, keepdims=True))
    aff_ref[0] = e / jnp.sum(e, axis=0, keepdims=True)
    logits_t = jnp.dot(h, rwp_ref[...], preferred_element_type=F32, precision=lax.Precision.HIGHEST)
    lane = lax.broadcasted_iota(jnp.int32, logits_t.shape, 1)
    logits_t = jnp.where(lane < n_exp, logits_t, MASK_VALUE)
    et = jnp.exp(logits_t - jnp.max(logits_t, axis=1, keepdims=True))
    afft_ref[0] = et / jnp.sum(et, axis=1, keepdims=True)


def norm_mod(x, w, shift, scale, n_ctx, router_w=None, tm=256):
    b, n, d = x.shape
    n_exp = 0 if router_w is None else router_w.shape[1]
    in_specs = [pl.BlockSpec((1, tm, d), lambda bi, i: (bi, i, 0)),
                pl.BlockSpec((1, d), lambda bi, i: (0, 0)),
                pl.BlockSpec((1, 2, d), lambda bi, i: (bi, 0, 0)),
                pl.BlockSpec((1, 2, d), lambda bi, i: (bi, 0, 0))]
    args = [x, w.reshape(1, d), shift, scale]
    if not n_exp:
        out_specs = [pl.BlockSpec((1, tm, d), lambda bi, i: (bi, i, 0))]
        out_shape = [jax.ShapeDtypeStruct((b, n, d), BF16)]
    else:
        in_specs += [pl.BlockSpec((n_exp, d), lambda bi, i: (0, 0)),
                     pl.BlockSpec((d, LANES), lambda bi, i: (0, 0))]
        args += [router_w.T, jnp.zeros((d, LANES), F32).at[:, :n_exp].set(router_w)]
        out_specs = [pl.BlockSpec((1, tm, d // 2), lambda bi, i: (bi, i, 0)),
                     pl.BlockSpec((1, n_exp, tm), lambda bi, i: (bi, 0, i)),
                     pl.BlockSpec((1, tm, LANES), lambda bi, i: (bi, i, 0))]
        out_shape = [jax.ShapeDtypeStruct((b, n, d // 2), jnp.uint32),
                     jax.ShapeDtypeStruct((b, n_exp, n), F32),
                     jax.ShapeDtypeStruct((b, n, LANES), F32)]
    outs = pl.pallas_call(
        functools.partial(_norm_mod_kernel, tm=tm, n_ctx=n_ctx, n_exp=n_exp),
        grid=(b, n // tm),
        in_specs=in_specs, out_specs=out_specs, out_shape=out_shape,
        compiler_params=_params("parallel", "parallel"),
        name="norm_mod_router" if n_exp else "norm_mod",
    )(*args)
    return outs if n_exp else outs[0]


def _final_norm_kernel(x_ref, w_ref, o_ref):
    xf = x_ref[0]
    o_ref[0] = xf * lax.rsqrt(jnp.mean(xf * xf, axis=-1, keepdims=True) + NORM_EPS) * w_ref[...]


def final_norm(x, w, n_ctx, tm=256):
    b, n, d = x.shape
    off = n_ctx // tm
    return pl.pallas_call(
        _final_norm_kernel,
        grid=(b, (n - n_ctx) // tm),
        in_specs=[pl.BlockSpec((1, tm, d), lambda bi, i: (bi, i + off, 0)),
                  pl.BlockSpec((1, d), lambda bi, i: (0, 0))],
        out_specs=pl.BlockSpec((1, tm, d), lambda bi, i: (bi, i, 0)),
        out_shape=jax.ShapeDtypeStruct((b, n - n_ctx, d), F32),
        compiler_params=_params("parallel", "parallel"),
        name="final_norm",
    )(x, w.reshape(1, d))


def _matmul_kernel(a_ref, w_ref, *rest, tm, n_ctx, residual):
    if residual:
        res_ref, g_ref, o_ref = rest
    else:
        (o_ref,) = rest
    acc = _dot(a_ref[0], w_ref[...])
    if residual:
        g = _row_select(pl.program_id(1), tm, n_ctx, g_ref[0])
        acc = res_ref[0] + g * acc
    o_ref[0] = acc.astype(o_ref.dtype)


def matmul(a, w, out_dtype, res=None, gate=None, n_ctx=0, tm=768, tn=None):
    b, n, k = a.shape
    m = w.shape[1]
    tn = m if tn is None else tn
    residual = res is not None
    in_specs = [pl.BlockSpec((1, tm, k), lambda bi, i, j: (bi, i, 0)),
                pl.BlockSpec((k, tn), lambda bi, i, j: (0, j))]
    args = [a, w]
    if residual:
        in_specs += [pl.BlockSpec((1, tm, tn), lambda bi, i, j: (bi, i, j)),
                     pl.BlockSpec((1, 2, tn), lambda bi, i, j: (bi, 0, j))]
        args += [res, gate]
    return pl.pallas_call(
        functools.partial(_matmul_kernel, tm=tm, n_ctx=n_ctx, residual=residual),
        grid=(b, n // tm, m // tn),
        in_specs=in_specs,
        out_specs=pl.BlockSpec((1, tm, tn), lambda bi, i, j: (bi, i, j)),
        out_shape=jax.ShapeDtypeStruct((b, n, m), out_dtype),
        compiler_params=_params("parallel", "parallel", "arbitrary"),
        name="matmul_res" if residual else "matmul",
    )(*args)


def _rotate_pairs(x, axis):
    seg = lax.broadcasted_iota(jnp.int32, x.shape, axis) // 16
    n = x.shape[axis]
    return jnp.where(seg % 2 == 0, pltpu.roll(x, n - 16, axis), pltpu.roll(x, 16, axis))


def _da_qkv_kernel(h_ref, wqt_ref, wk_ref, wvt_ref, cq_ref, sq_ref, ck_ref, sk_ref,
                   qt_ref, k_ref, vt_ref, *, heads):
    h = h_ref[0]
    qt = _dot_nt(wqt_ref[...], h)
    cq, sq = cq_ref[...], sq_ref[...]
    for hd in range(heads):
        rs = slice(hd * LANES, (hd + 1) * LANES)
        x = qt[rs]
        qt_ref[0, rs, :] = (x * cq + _rotate_pairs(x, 0) * sq).astype(qt_ref.dtype)
    k = _dot(h, wk_ref[...])
    ck, sk = ck_ref[...], sk_ref[...]
    for hd in range(heads):
        cs = slice(hd * LANES, (hd + 1) * LANES)
        x = k[:, cs]
        k_ref[0, :, cs] = (x * ck + _rotate_pairs(x, 1) * sk).astype(k_ref.dtype)
    vt = _dot_nt(wvt_ref[...], h).astype(vt_ref.dtype)
    ones = jnp.ones((DA_SUM_ROWS, vt.shape[1]), vt_ref.dtype)
    for hd in range(heads):
        vt_ref[0, 0, hd, :LANES, :] = vt[hd * LANES:(hd + 1) * LANES]
        vt_ref[0, 0, hd, LANES:, :] = ones


def da_qkv(h, wqkv, tabs, tm):
    b, n, d = h.shape
    m = wqkv.shape[1] // 3
    cq_t, sq_t, ck, sk = tabs
    wqt = wqkv[:, :m].T.astype(BF16)
    wk = wqkv[:, m:2 * m].astype(BF16)
    wvt = wqkv[:, 2 * m:].T.astype(BF16)
    full = lambda r, c: pl.BlockSpec((r, c), lambda bi, i: (0, 0))
    return pl.pallas_call(
        functools.partial(_da_qkv_kernel, heads=m // LANES),
        grid=(b, n // tm),
        in_specs=[pl.BlockSpec((1, tm, d), lambda bi, i: (bi, i, 0)),
                  full(m, d), full(d, m), full(m, d),
                  pl.BlockSpec((LANES, tm), lambda bi, i: (0, i)),
                  pl.BlockSpec((LANES, tm), lambda bi, i: (0, i)),
                  pl.BlockSpec((tm, LANES), lambda bi, i: (i, 0)),
                  pl.BlockSpec((tm, LANES), lambda bi, i: (i, 0))],
        out_specs=[pl.BlockSpec((1, m, tm), lambda bi, i: (bi, 0, i)),
                   pl.BlockSpec((1, tm, m), lambda bi, i: (bi, i, 0)),
                   pl.BlockSpec((1, 1, m // LANES, LANES + DA_SUM_ROWS, tm), lambda bi, i: (bi, i, 0, 0, 0))],
        out_shape=[jax.ShapeDtypeStruct((b, m, n), BF16),
                   jax.ShapeDtypeStruct((b, n, m), BF16),
                   jax.ShapeDtypeStruct((b, n // tm, m // LANES, LANES + DA_SUM_ROWS, tm), BF16)],
        compiler_params=_params("parallel", "parallel"),
        name="da_qkv",
    )(h, wqt, wk, wvt, cq_t, sq_t, ck, sk)


def rope_tables(n_ctx, seq, q_scale):
    d = DA_HEAD_DIM
    t = jnp.arange(seq)
    rows = (t // GRID_W).astype(F32)
    cols = (t % GRID_W).astype(F32)
    quarter = d // 4
    freqs = ROPE_BASE ** (-jnp.arange(quarter, dtype=F32) / quarter)
    ang_r = rows[:, None] * freqs[None, :]
    ang_c = cols[:, None] * freqs[None, :]
    ang = jnp.concatenate([ang_r, ang_r, ang_c, ang_c], axis=-1)
    cos = jnp.cos(ang)
    sign = jnp.concatenate([-jnp.ones((quarter,), F32), jnp.ones((quarter,), F32)] * 2)
    sin = jnp.sin(ang) * sign[None, :]
    cos = jnp.concatenate([jnp.ones((n_ctx, d), F32), cos], axis=0)
    sin = jnp.concatenate([jnp.zeros((n_ctx, d), F32), sin], axis=0)
    cos = jnp.concatenate([cos, cos], axis=-1)
    sin = jnp.concatenate([sin, sin], axis=-1)
    return (cos * q_scale).T, (sin * q_scale).T, cos, sin


def _diff_attn_kernel(*refs, nq, tk, kw, nchunks, lam_init):
    qt_refs = refs[:nq]
    k_ref, vt_ref, lq1_ref, lk1_ref, lq2_ref, lk2_ref, sub_ref, o_ref, acc_ref, s_ref = refs[nq:]
    gw = DA_GROUP_LANES
    ngroups = 2 * nq
    sub = lax.broadcasted_iota(jnp.int32, (LANES, gw), 0)
    zero = jnp.zeros((LANES, gw), qt_refs[0].dtype)
    qg = []
    for g in range(ngroups):
        q = qt_refs[g // 2][0]
        qg.append(jnp.where((sub < DA_HEAD_DIM) if g % 2 == 0 else (sub >= DA_HEAD_DIM), q, zero))
    acc_ref[...] = jnp.zeros(acc_ref.shape, F32)
    half = kw // 2

    def col_reduce(x, op):
        return op(op(x.reshape(8, x.shape[0] // 8, x.shape[1]), axis=0), axis=0, keepdims=True)

    def scores(t):
        c, g = divmod(t, ngroups)
        mx = None
        for hf in range(2):
            k = k_ref[0, c * tk + hf * half:c * tk + (hf + 1) * half, :]
            s = _dot(k, qg[g])
            s_ref[t % DA_SCORE_BUFFERS, hf * half:(hf + 1) * half, :] = s
            hm = col_reduce(s, jnp.max)
            mx = hm if mx is None else jnp.maximum(mx, hm)
        return mx

    def softmax_pv(t, mx, ml):
        c, g = divmod(t, ngroups)
        m_old, l_old = ml
        vt = vt_ref[0, c, 0]
        m_new = jnp.maximum(m_old, mx)
        alpha = jnp.exp2(m_old - m_new)
        pb = jnp.exp2(s_ref[t % DA_SCORE_BUFFERS] - m_new).astype(BF16)
        pv = _dot(vt[:, :kw], pb)
        acc_ref[g] = alpha * acc_ref[g] + pv[:LANES]
        return m_new, alpha * l_old + pv[LANES:LANES + 1]

    nsteps = nchunks * ngroups
    ahead = DA_SCORE_BUFFERS - 1
    carry = [(jnp.full((1, gw), -jnp.inf, F32), jnp.zeros((1, gw), F32))] * ngroups
    maxima = {t: scores(t) for t in range(min(ahead, nsteps))}
    for t in range(nsteps):
        if t + ahead < nsteps:
            maxima[t + ahead] = scores(t + ahead)
        carry[t % ngroups] = softmax_pv(t, maxima.pop(t), carry[t % ngroups])
    lam = (jnp.exp(jnp.sum(lq1_ref[...] * lk1_ref[...], axis=-1, keepdims=True))
           - jnp.exp(jnp.sum(lq2_ref[...] * lk2_ref[...], axis=-1, keepdims=True)) + lam_init)
    for qh in range(ngroups // 2):
        od = acc_ref[2 * qh] / carry[2 * qh][1] - lam * (acc_ref[2 * qh + 1] / carry[2 * qh + 1][1])
        y = od * lax.rsqrt(jnp.mean(od * od, axis=0, keepdims=True) + NORM_EPS) * sub_ref[...]
        o_ref[0, qh * gw:(qh + 1) * gw, :] = (y * (1.0 - lam_init)).T.astype(o_ref.dtype)


def diff_attention(qt, k, vt, lam_params, subln, lam_init, q_start, n_q, n_k, tq):
    b, m, n = qt.shape
    h = m // LANES
    tk = vt.shape[-1]
    kw = min(tk, n_k)
    nchunks = n_k // kw
    gw = DA_GROUP_LANES
    nq = tq // gw
    assert q_start % gw == 0 and tq % gw == 0 and n_q % tq == 0
    qoff = q_start // gw
    vec = lambda a: a.reshape(1, -1).astype(F32)
    small = pl.BlockSpec((1, DA_HEAD_DIM), lambda bi, hi, i: (0, 0))
    q_specs = [pl.BlockSpec((1, LANES, gw), functools.partial(
        lambda bi, hi, i, j: (bi, hi, qoff + nq * i + j), j=j)) for j in range(nq)]
    return pl.pallas_call(
        functools.partial(_diff_attn_kernel, nq=nq, tk=tk, kw=kw, nchunks=nchunks, lam_init=lam_init),
        grid=(b, h, n_q // tq),
        in_specs=q_specs + [
                  pl.BlockSpec((1, n_k, LANES), lambda bi, hi, i: (bi, 0, hi)),
                  pl.BlockSpec((1, nchunks, 1, vt.shape[3], tk), lambda bi, hi, i: (bi, 0, hi, 0, 0)),
                  small, small, small, small,
                  pl.BlockSpec((LANES, 1), lambda bi, hi, i: (0, 0))],
        out_specs=pl.BlockSpec((1, tq, LANES), lambda bi, hi, i: (bi, i, hi)),
        out_shape=jax.ShapeDtypeStruct((b, n_q, m), BF16),
        scratch_shapes=[pltpu.VMEM((2 * tq // DA_GROUP_LANES, LANES, DA_GROUP_LANES), F32),
                        pltpu.VMEM((DA_SCORE_BUFFERS, kw, DA_GROUP_LANES), F32)],
        compiler_params=_params("parallel", "parallel", "arbitrary"),
        name="diff_attention",
    )(*([qt] * nq), k, vt, *[vec(p) for p in lam_params], subln.reshape(-1, 1).astype(F32))


def _pair_scores(qp, kp):
    lane = lax.broadcasted_iota(jnp.int32, qp.shape, 1)
    zero = jnp.zeros_like(qp)
    qs = jnp.concatenate([jnp.where(lane < 64, qp, zero), jnp.where(lane >= 64, qp, zero)], axis=0)
    return _dot_nt(qs, kp)


def _pair_softmax_pv(s, vp, bias, n_bias):
    tq = s.shape[0] // 2
    if n_bias:
        s_w = s[:, :n_bias] + bias
        s_c = s[:, n_bias:]
        m = jnp.maximum(jnp.max(s_w, axis=-1, keepdims=True), jnp.max(s_c, axis=-1, keepdims=True))
        p_w = jnp.exp(s_w - m)
        p_c = jnp.exp(s_c - m)
        l = jnp.sum(p_w, axis=-1, keepdims=True) + jnp.sum(p_c, axis=-1, keepdims=True)
        pv = _dot(p_w.astype(BF16), vp[:n_bias]) + _dot(p_c.astype(BF16), vp[n_bias:])
    else:
        m = jnp.max(s, axis=-1, keepdims=True)
        p = jnp.exp(s - m)
        l = jnp.sum(p, axis=-1, keepdims=True)
        pv = _dot(p.astype(BF16), vp)
    o = pv / l
    lane_o = lax.broadcasted_iota(jnp.int32, (tq, LANES), 1)
    return jnp.where(lane_o < 64, o[:tq], o[tq:])


def _pairs_attention(q_of, k_of, v_of, bias_of, n_bias, o_ref):
    npairs = NA_HEADS // 2
    s_next = _pair_scores(q_of(0), k_of(0))
    for hp in range(npairs):
        s_cur = s_next
        if hp + 1 < npairs:
            s_next = _pair_scores(q_of(hp + 1), k_of(hp + 1))
        o = _pair_softmax_pv(s_cur, v_of(hp), bias_of(hp), n_bias)
        o_ref[0, :, hp * LANES:(hp + 1) * LANES] = o.astype(o_ref.dtype)


def _na_kernel(*refs, wr, n_ctx, scale):
    q_ref = refs[0]
    k_refs = refs[1:1 + wr]
    v_refs = refs[1 + wr:1 + 2 * wr]
    kc_ref, vc_ref, bias_ref, o_ref, kbuf, vbuf = refs[1 + 2 * wr:]
    nw = wr * GRID_W
    for w in range(wr):
        kbuf[w * GRID_W:(w + 1) * GRID_W, :] = k_refs[w][0]
        vbuf[w * GRID_W:(w + 1) * GRID_W, :] = v_refs[w][0]
    kbuf[nw:nw + n_ctx, :] = kc_ref[0]
    vbuf[nw:nw + n_ctx, :] = vc_ref[0]
    cols = lambda hp: slice(hp * LANES, (hp + 1) * LANES)
    _pairs_attention(lambda hp: q_ref[0, :, cols(hp)] * scale,
                     lambda hp: kbuf[:, cols(hp)],
                     lambda hp: vbuf[:, cols(hp)],
                     lambda hp: bias_ref[0, 2 * hp:2 * hp + 2].reshape(2 * GRID_W, nw),
                     nw, o_ref)


def na_bias_table(rpb, rows):
    wr = min(NA_ROWS, rows)
    qcol = np.arange(GRID_W)
    col_start = np.clip(qcol - NA_COLS // 2, 0, GRID_W - NA_COLS)
    kc = np.arange(GRID_W)
    inside = (kc[None, :] >= col_start[:, None]) & (kc[None, :] < col_start[:, None] + NA_COLS)
    dc = np.clip(kc[None, :] - qcol[:, None] + NA_COLS - 1, 0, 2 * NA_COLS - 2)
    dr = np.arange(wr)[None, :] - np.arange(wr)[:, None] + NA_ROWS - 1
    sel_r = (dr[:, :, None] == np.arange(2 * NA_ROWS - 1)).astype(np.float32)
    sel_c = (dc[:, :, None] == np.arange(2 * NA_COLS - 1)).astype(np.float32)
    tab = jnp.einsum("cwa,hab,qkb->chqwk", sel_r, rpb.astype(F32), sel_c,
                     precision=lax.Precision.HIGHEST)
    tab = jnp.where(inside[None, None, :, None, :], tab, MASK_VALUE)
    return tab.reshape(wr, rpb.shape[0], GRID_W, wr * GRID_W)


def neighbourhood_attention(qkv, bias_tab, n_ctx):
    b, n, m3 = qkv.shape
    m = m3 // 3
    rows = (n - n_ctx) // GRID_W
    wr = min(NA_ROWS, rows)
    cb = n_ctx // GRID_W
    rs = lambda r: jnp.clip(r - wr // 2, 0, rows - wr)
    blk = lambda f: pl.BlockSpec((1, GRID_W, m), f)
    in_specs = [blk(lambda bi, r: (bi, cb + r, 0))]
    in_specs += [blk(functools.partial(lambda bi, r, w: (bi, cb + rs(r) + w, 1), w=w)) for w in range(wr)]
    in_specs += [blk(functools.partial(lambda bi, r, w: (bi, cb + rs(r) + w, 2), w=w)) for w in range(wr)]
    in_specs += [pl.BlockSpec((1, n_ctx, m), lambda bi, r: (bi, 0, 1)),
                 pl.BlockSpec((1, n_ctx, m), lambda bi, r: (bi, 0, 2)),
                 pl.BlockSpec((1, NA_HEADS, GRID_W, wr * GRID_W), lambda bi, r: (r - rs(r), 0, 0, 0))]
    nk = wr * GRID_W + n_ctx
    return pl.pallas_call(
        functools.partial(_na_kernel, wr=wr, n_ctx=n_ctx, scale=(m // NA_HEADS) ** -0.5),
        grid=(b, rows),
        in_specs=in_specs,
        out_specs=pl.BlockSpec((1, GRID_W, m), lambda bi, r: (bi, r, 0)),
        out_shape=jax.ShapeDtypeStruct((b, n - n_ctx, m), BF16),
        scratch_shapes=[pltpu.VMEM((nk, m), BF16), pltpu.VMEM((nk, m), BF16)],
        compiler_params=_params("parallel", "arbitrary"),
        name="neighbourhood_attention",
    )(*([qkv] * (1 + 2 * wr + 2)), bias_tab)


def _ctx_attn_kernel(q_ref, k_ref, v_ref, o_ref, *, scale):
    cols = lambda hp: slice(hp * LANES, (hp + 1) * LANES)
    _pairs_attention(lambda hp: q_ref[0, :, cols(hp)] * scale,
                     lambda hp: k_ref[0, :, cols(hp)],
                     lambda hp: v_ref[0, :, cols(hp)],
                     lambda hp: None, 0, o_ref)


def context_attention(qkv, n_ctx):
    b, n, m3 = qkv.shape
    m = m3 // 3
    return pl.pallas_call(
        functools.partial(_ctx_attn_kernel, scale=(m // NA_HEADS) ** -0.5),
        grid=(b,),
        in_specs=[pl.BlockSpec((1, n_ctx, m), lambda bi: (bi, 0, 0)),
                  pl.BlockSpec((1, n_ctx, m), lambda bi: (bi, 0, 1)),
                  pl.BlockSpec((1, n_ctx, m), lambda bi: (bi, 0, 2))],
        out_specs=pl.BlockSpec((1, n_ctx, m), lambda bi: (bi, 0, 0)),
        out_shape=jax.ShapeDtypeStruct((b, n_ctx, m), BF16),
        compiler_params=_params("parallel"),
        name="context_attention",
    )(qkv, qkv, qkv)


def _conv_silu_kernel(x_ref, w_ref, b_ref, o_ref, *, n, n_ctx):
    x = x_ref[0]
    row = lax.broadcasted_iota(jnp.int32, x.shape, 0)
    xm = jnp.where((row == 0) | (row == n_ctx), 0.0, pltpu.roll(x, 1, 0))
    xp = jnp.where((row == n_ctx - 1) | (row == n - 1), 0.0, pltpu.roll(x, n - 1, 0))
    y = w_ref[0:1, :] * xm + w_ref[1:2, :] * x + w_ref[2:3, :] * xp + b_ref[...]
    o_ref[0] = y * jax.nn.sigmoid(y)


def conv_silu(x, w, bias, n_ctx):
    b, n, c = x.shape
    return pl.pallas_call(
        functools.partial(_conv_silu_kernel, n=n, n_ctx=n_ctx),
        grid=(b, c // LANES),
        in_specs=[pl.BlockSpec((1, n, LANES), lambda bi, j: (bi, 0, j)),
                  pl.BlockSpec((3, LANES), lambda bi, j: (0, j)),
                  pl.BlockSpec((1, LANES), lambda bi, j: (0, j))],
        out_specs=pl.BlockSpec((1, n, LANES), lambda bi, j: (bi, 0, j)),
        out_shape=jax.ShapeDtypeStruct((b, n, c), F32),
        compiler_params=_params("parallel", "parallel"),
        name="conv_silu",
    )(x, w, bias.reshape(1, c))


def _softplus(x):
    return jnp.maximum(x, 0.0) + jnp.log1p(jnp.exp(-jnp.abs(x)))


def _ssd_kernel(x_ref, b_ref, c_ref, dt_ref, dtb_ref, alog_ref, e_ref, y_ref, state_ref,
                *, direction, nheads):
    L = SSM_CHUNK
    P = SSM_HEAD_DIM
    G = SSM_GROUPS
    hpg = nheads // G
    reverse = direction == 1

    @pl.when(pl.program_id(1) == 0)
    def _():
        state_ref[...] = jnp.zeros(state_ref.shape, F32)

    li = lax.broadcasted_iota(jnp.int32, (L, L), 0)
    si = lax.broadcasted_iota(jnp.int32, (L, L), 1)
    allowed = (si >= li) if reverse else (si <= li)
    tri = jnp.where(allowed, 1.0, 0.0).astype(BF16)
    tri_t = jnp.where((li >= si) if reverse else (li <= si), 1.0, 0.0).astype(BF16)

    dt_raw = dt_ref[0][:, direction * nheads:(direction + 1) * nheads]
    dt = _softplus(dt_raw + dtb_ref[...])
    a = dt * (-jnp.exp(alog_ref[...]))
    a3 = _split3(a)
    acs = sum(_dot(tri, p) for p in a3)
    acs_t = sum(_dot_tn(p, tri_t) for p in a3)
    expand = e_ref[...]
    acs_x = sum(_dot(p, expand) for p in _split3(acs))
    dt_x = sum(_dot(p, expand) for p in _split3(dt))
    end = 0 if reverse else L - 1
    total_x = acs_x[end:end + 1, :]

    xd = x_ref[0] * dt_x
    xd_b = xd.astype(BF16)
    x_state = (xd * jnp.exp(total_x - acs_x)).astype(BF16)
    exp_acs_x = jnp.exp(acs_x)
    chunk_decay_x = jnp.exp(total_x)
    lane = lax.broadcasted_iota(jnp.int32, (L, LANES), 1)

    for g in range(G):
        gs = slice(g * SSM_STATE, (g + 1) * SSM_STATE)
        hs = slice(g * hpg * P, (g + 1) * hpg * P)
        cg = c_ref[0][:, gs].astype(BF16)
        bg = b_ref[0][:, gs].astype(BF16)
        cb = _dot_nt(cg, bg)
        h_prev = state_ref[:, hs]
        y_off = _dot(cg, h_prev.astype(BF16)) * exp_acs_x[:, hs]
        state_ref[:, hs] = h_prev * chunk_decay_x[:, hs] + _dot_tn(bg, x_state[:, hs])
        for rp in range(hpg // 2):
            mats = []
            for r in (2 * rp, 2 * rp + 1):
                hidx = g * hpg + r
                seg = acs[:, hidx:hidx + 1] - acs_t[hidx:hidx + 1, :]
                mats.append((cb * jnp.exp(jnp.where(allowed, seg, -jnp.inf))).astype(BF16))
            ps = slice((g * hpg + 2 * rp) * P, (g * hpg + 2 * rp + 2) * P)
            yd = _dot(jnp.concatenate(mats, axis=0), xd_b[:, ps])
            y_ref[0, :, ps] = jnp.where(lane < P, yd[:L], yd[L:]) + y_off[:, 2 * rp * P:(2 * rp + 2) * P]


def ssd_scan(xbc, dt_raw, dt_bias, a_log, expand, direction, nheads, n_ctx):
    b, n, _ = xbc.shape
    d_inner = nheads * SSM_HEAD_DIM
    gn = SSM_GROUPS * SSM_STATE
    L = SSM_CHUNK
    nc = n // L
    ncc = n_ctx // L
    if direction == 0:
        cidx = lambda c: c
    else:
        cidx = lambda c: jnp.where(c < ncc, ncc - 1 - c, nc - 1 - (c - ncc))
    return pl.pallas_call(
        functools.partial(_ssd_kernel, direction=direction, nheads=nheads),
        grid=(b, nc),
        in_specs=[pl.BlockSpec((1, L, d_inner), lambda bi, c: (bi, cidx(c), 0)),
                  pl.BlockSpec((1, L, gn), lambda bi, c: (bi, cidx(c), d_inner // gn)),
                  pl.BlockSpec((1, L, gn), lambda bi, c: (bi, cidx(c), d_inner // gn + 1)),
                  pl.BlockSpec((1, L, LANES), lambda bi, c: (bi, cidx(c), 0)),
                  pl.BlockSpec((1, nheads), lambda bi, c: (0, 0)),
                  pl.BlockSpec((1, nheads), lambda bi, c: (0, 0)),
                  pl.BlockSpec((nheads, d_inner), lambda bi, c: (0, 0))],
        out_specs=pl.BlockSpec((1, L, d_inner), lambda bi, c: (bi, cidx(c), 0)),
        out_shape=jax.ShapeDtypeStruct((b, n, d_inner), F32),
        scratch_shapes=[pltpu.VMEM((SSM_STATE, d_inner), F32)],
        compiler_params=_params("parallel", "arbitrary"),
        name="ssd_scan_%d" % direction,
    )(xbc, xbc, xbc, dt_raw, dt_bias[direction:direction + 1], a_log[direction:direction + 1], expand)


def _ssm_out_kernel(y0_ref, y1_ref, x_ref, z_ref, d_ref, nw_ref, w_ref, res_ref, g_ref, o_ref,
                    *, tm, n_ctx):
    z = z_ref[0]
    u = (y0_ref[0] + y1_ref[0] + d_ref[...] * x_ref[0]) * (z * jax.nn.sigmoid(z))
    un = u * lax.rsqrt(jnp.mean(u * u, axis=-1, keepdims=True) + NORM_EPS) * nw_ref[...]
    g = _row_select(pl.program_id(1), tm, n_ctx, g_ref[0])
    o_ref[0] = res_ref[0] + g * _dot(un.astype(BF16), w_ref[...])


def ssm_out(y0, y1, xbc, z, d_chan, norm_w, w_out, res, gate, n_ctx, tm=256):
    b, n, di = y0.shape
    d = w_out.shape[1]
    row = lambda c: pl.BlockSpec((1, tm, c), lambda bi, i: (bi, i, 0))
    vec = lambda c: pl.BlockSpec((1, c), lambda bi, i: (0, 0))
    return pl.pallas_call(
        functools.partial(_ssm_out_kernel, tm=tm, n_ctx=n_ctx),
        grid=(b, n // tm),
        in_specs=[row(di), row(di), row(di), row(di), vec(di), vec(di),
                  pl.BlockSpec((di, d), lambda bi, i: (0, 0)), row(d),
                  pl.BlockSpec((1, 2, d), lambda bi, i: (bi, 0, 0))],
        out_specs=row(d),
        out_shape=jax.ShapeDtypeStruct((b, n, d), F32),
        compiler_params=_params("parallel", "parallel"),
        name="ssm_out",
    )(y0, y1, xbc, z, d_chan, norm_w.reshape(1, di), w_out, res, gate)


def _cumsum_lanes(x):
    n = x.shape[-1]
    lane = lax.broadcasted_iota(jnp.int32, x.shape, x.ndim - 1)
    s = 1
    while s < n:
        x = x + jnp.where(lane >= s, pltpu.roll(x, s, x.ndim - 1), 0.0)
        s *= 2
    return x


def _route_set(aff, cap):
    e, t = aff.shape
    bits = lax.bitcast_convert_type(aff, jnp.int32)

    def count_ge(thr):
        return jnp.sum(jnp.where(bits >= thr, 1.0, 0.0), axis=1, keepdims=True)

    def step(_, lohi):
        lo, hi = lohi
        mid = lo + (hi - lo) // 2
        ok = count_ge(mid) >= cap
        return jnp.where(ok, mid, lo), jnp.where(ok, hi, mid)

    lo0 = jnp.zeros((e, 1), jnp.int32)
    hi0 = jnp.full((e, 1), ONE_BITS + 1, jnp.int32)
    thr, _ = lax.fori_loop(0, 31, step, (lo0, hi0))
    gt = jnp.where(bits > thr, 1.0, 0.0)
    eq = jnp.where(bits == thr, 1.0, 0.0)
    need = cap - jnp.sum(gt, axis=1, keepdims=True)
    sel = gt + eq * jnp.where(_cumsum_lanes(eq) <= need, 1.0, 0.0)
    return sel, _cumsum_lanes(sel)


def _route_kernel(aff_ref, idx_ref, cnt_ref, *, sets, jblk):
    n_exp = aff_ref.shape[1]
    for t0, t, cap, row0 in sets:
        _, cnt = _route_set(aff_ref[0, :, t0:t0 + t], cap)
        cnt_ref[:, :t] = cnt
        for jb in range(0, cap, jblk):
            nj = min(jblk, cap - jb)
            jcol = (lax.broadcasted_iota(jnp.int32, (nj, 1), 0) + jb).astype(F32)
            lane = lax.broadcasted_iota(jnp.int32, (nj, LANES), 1)

            def per_expert(ei, acc):
                row = cnt_ref[pl.ds(ei, 1), :t]
                pos = jnp.sum(jnp.where(row <= jcol, 1.0, 0.0), axis=1, keepdims=True)
                return jnp.where(lane == ei, pos, acc)

            acc = lax.fori_loop(0, n_exp, per_expert, jnp.zeros((nj, LANES), F32))
            idx_ref[0, row0 + jb:row0 + jb + nj, :] = acc.astype(jnp.int32) + t0


def route(aff, sets):
    b, e, n = aff.shape
    m = sum(s[2] for s in sets)
    tmax = max(s[1] for s in sets)
    rows, sets_r = 0, []
    for t0, t, cap in sets:
        sets_r.append((t0, t, cap, rows))
        rows += cap
    return pl.pallas_call(
        functools.partial(_route_kernel, sets=tuple(sets_r), jblk=256),
        grid=(b,),
        in_specs=[pl.BlockSpec((1, e, n), lambda bi: (bi, 0, 0))],
        out_specs=pl.BlockSpec((1, m, LANES), lambda bi: (bi, 0, 0)),
        out_shape=jax.ShapeDtypeStruct((b, m, LANES), jnp.int32),
        scratch_shapes=[pltpu.VMEM((e, tmax), F32)],
        compiler_params=_params("parallel"),
        name="route",
    )(aff)


def _expert_ffn_kernel(idx_ref, hp_ref, afft_ref, g_ref, w1_ref, w3_ref, w2_ref, o_ref,
                       xw_ref, xs_ref, gate_ref, *, m, n_lat):
    ei = pl.program_id(1)
    f = pl.program_id(2)

    @pl.when(f == 0)
    def _():
        def gather(g, c):
            base = pl.multiple_of(g * 8, 8)
            for r in range(8):
                i = idx_ref[0, 0, 0, base + r]
                xw_ref[pl.ds(base + r, 1), :] = hp_ref[0, pl.ds(i, 1), :]
                gate_ref[pl.ds(base + r, 1), :] = afft_ref[0, pl.ds(i, 1), :]
            return c

        lax.fori_loop(0, m // 8, gather, 0)
        w = xw_ref[...]
        lo = lax.bitcast_convert_type(w << 16, F32).astype(BF16)
        hi = lax.bitcast_convert_type(w & jnp.uint32(0xFFFF0000), F32).astype(BF16)
        xs_ref[...] = jnp.concatenate([lo, hi], axis=1)
        lane = lax.broadcasted_iota(jnp.int32, gate_ref.shape, 1)
        gate = jnp.sum(jnp.where(lane == ei, gate_ref[...], 0.0), axis=1, keepdims=True)
        gate_ref[...] = jnp.broadcast_to(gate, gate_ref.shape)
        o_ref[...] = jnp.zeros(o_ref.shape, F32)

    x = xs_ref[...]
    h1 = _dot(x, w1_ref[0].astype(BF16))
    h3 = _dot(x, w3_ref[0].astype(BF16))
    hid = (h1 * jax.nn.sigmoid(h1) * h3).astype(BF16)
    part = _dot(hid, w2_ref[0].astype(BF16))

    o_ref[0, 0] = o_ref[0, 0] + part

    @pl.when(f == pl.num_programs(2) - 1)
    def _():
        row = lax.broadcasted_iota(jnp.int32, (m, 1), 0)
        g = jnp.where(row < n_lat, g_ref[0, 1:2, :], g_ref[0, 0:1, :])
        o_ref[0, 0] = o_ref[0, 0] * gate_ref[:, 0:1] * g


def expert_ffn(idx, hp, afft, gate2, w1, w3, w2, layer, n_lat, tf=256):
    b, e, m = idx.shape
    n, dh = hp.shape[1:]
    d = 2 * dh
    ff = w1.shape[3]
    single = dict(pipeline_mode=pl.Buffered(1))
    return pl.pallas_call(
        functools.partial(_expert_ffn_kernel, m=m, n_lat=n_lat),
        grid=(b, e, ff // tf),
        in_specs=[pl.BlockSpec((1, 1, 1, m), lambda bi, ei, f: (bi, ei, 0, 0), memory_space=pltpu.SMEM),
                  pl.BlockSpec((1, n, dh), lambda bi, ei, f: (bi, 0, 0), **single),
                  pl.BlockSpec((1, n, LANES), lambda bi, ei, f: (bi, 0, 0), **single),
                  pl.BlockSpec((1, 2, d), lambda bi, ei, f: (bi, 0, 0)),
                  pl.BlockSpec((None, 1, d, tf), lambda bi, ei, f: (layer, ei, 0, f)),
                  pl.BlockSpec((None, 1, d, tf), lambda bi, ei, f: (layer, ei, 0, f)),
                  pl.BlockSpec((None, 1, tf, d), lambda bi, ei, f: (layer, ei, f, 0))],
        out_specs=pl.BlockSpec((1, 1, m, d), lambda bi, ei, f: (bi, ei, 0, 0)),
        out_shape=jax.ShapeDtypeStruct((b, e, m, d), F32),
        scratch_shapes=[pltpu.VMEM((m, dh), jnp.uint32), pltpu.VMEM((m, d), BF16),
                        pltpu.VMEM((m, LANES), F32)],
        compiler_params=_params("parallel", "arbitrary", "arbitrary"),
        name="expert_ffn",
    )(idx.reshape(b, e, 1, m), hp, afft, gate2, w1, w3, w2)


def _scatter_kernel(idx_ref, x_ref, y_ref, o_ref, *, m):
    ei = pl.program_id(2)

    @pl.when(ei == 0)
    def _():
        o_ref[...] = x_ref[...]

    def add_rows(g, c):
        base = pl.multiple_of(g * 8, 8)
        rows = [idx_ref[0, 0, 0, base + r] for r in range(8)]
        sums = [o_ref[0, pl.ds(i, 1), :] + y_ref[0, 0, pl.ds(base + r, 1), :] for r, i in enumerate(rows)]
        for i, v in zip(rows, sums):
            o_ref[0, pl.ds(i, 1), :] = v
        return c

    lax.fori_loop(0, m // 8, add_rows, 0)


def moe_scatter(idx, x, y, tc=512):
    b, e, m = idx.shape
    n, d = x.shape[1:]
    return pl.pallas_call(
        functools.partial(_scatter_kernel, m=m),
        grid=(b, d // tc, e),
        in_specs=[pl.BlockSpec((1, 1, 1, m), lambda bi, ci, ei: (bi, ei, 0, 0), memory_space=pltpu.SMEM),
                  pl.BlockSpec((1, n, tc), lambda bi, ci, ei: (bi, 0, ci), pipeline_mode=pl.Buffered(1)),
                  pl.BlockSpec((1, 1, m, tc), lambda bi, ci, ei: (bi, ei, 0, ci))],
        out_specs=pl.BlockSpec((1, n, tc), lambda bi, ci, ei: (bi, 0, ci), pipeline_mode=pl.Buffered(1)),
        out_shape=jax.ShapeDtypeStruct((b, n, d), F32),
        compiler_params=_params("parallel", "parallel", "arbitrary"),
        name="moe_scatter",
    )(idx.reshape(b, e, 1, m), x, y)


def _lambda_init(layer):
    return 0.8 - 0.6 * math.exp(-0.3 * layer)


def kernel(x, c, ctx, c_ctx, ada_w, ada_b, norm_mix, norm_ffn, final_norm_w,
           da_wqkv, da_wo, da_lam_q1, da_lam_k1, da_lam_q2, da_lam_k2, da_subln,
           ssm_w_in, ssm_conv_w, ssm_conv_b, ssm_dt_bias, ssm_A_log, ssm_D, ssm_norm, ssm_w_out,
           na_wqkv, na_wo, na_rpb, router_w, exp_w1, exp_w3, exp_w2):
    b, seq, d = x.shape
    n_ctx = ctx.shape[1]
    n = n_ctx + seq
    depth = ada_w.shape[0]
    n_exp = router_w.shape[-1]

    cond_rows = jnp.zeros((8, d), F32).at[:b].set(c).at[b].set(c_ctx)
    mods = ada_modulation(cond_rows, ada_w, ada_b).reshape(depth, 8, 6, d)
    mods = jnp.stack([jnp.broadcast_to(mods[:, b:b + 1], (depth, b, 6, d)), mods[:, :b]], axis=2)

    xj = jnp.concatenate([ctx, x], axis=1)
    rope_tabs = rope_tables(n_ctx, seq, DA_HEAD_DIM ** -0.5 * math.log2(math.e))

    ia = ib = ic = 0
    for layer in range(depth):
        need_ctx = layer < depth - 1
        sh_m, sc_m, g_m, sh_f, sc_f, g_f = [mods[layer, :, :, k] for k in range(6)]
        h = norm_mod(xj, norm_mix[layer], sh_m, sc_m, n_ctx)
        kind = layer % N_MIXERS
        if kind == 0:
            qt, k, vt = da_qkv(h, da_wqkv[ia], rope_tabs, DA_KV_CHUNK)
            lam_params = (da_lam_q1[ia], da_lam_k1[ia], da_lam_q2[ia], da_lam_k2[ia])
            li = _lambda_init(layer)
            o_l = diff_attention(qt, k, vt, lam_params, da_subln[ia], li, n_ctx, seq, n, DA_Q_TILE)
            o_c = diff_attention(qt, k, vt, lam_params, da_subln[ia], li, 0, n_ctx, n_ctx, n_ctx)
            o = jnp.concatenate([o_c, o_l], axis=1)
            xj = matmul(o, da_wo[ia].astype(BF16), F32, res=xj, gate=g_m, n_ctx=n_ctx)
            ia += 1
        elif kind == 1:
            nh = ssm_A_log.shape[-1]
            di = nh * SSM_HEAD_DIM
            conv_dim = ssm_conv_w.shape[-1]
            w_in = ssm_w_in[ib].astype(BF16)
            z = matmul(h, w_in[:, :di], F32, tn=1024)
            xbc = matmul(h, w_in[:, di:di + conv_dim], F32, tn=1024)
            w_dt = jnp.zeros((d, LANES), BF16).at[:, :2 * nh].set(w_in[:, di + conv_dim:])
            dt_raw = matmul(h, w_dt, F32)
            xbc = conv_silu(xbc, ssm_conv_w[ib], ssm_conv_b[ib], n_ctx)
            expand = (jnp.arange(di)[None, :] // SSM_HEAD_DIM == jnp.arange(nh)[:, None]).astype(BF16)
            ys = [ssd_scan(xbc, dt_raw, ssm_dt_bias[ib], ssm_A_log[ib], expand, dirn, nh, n_ctx)
                  for dirn in range(2)]
            d_chan = jnp.repeat(ssm_D[ib, 0] + ssm_D[ib, 1], SSM_HEAD_DIM).reshape(1, di)
            xj = ssm_out(ys[0], ys[1], xbc, z, d_chan, ssm_norm[ib], ssm_w_out[ib].astype(BF16),
                         xj, g_m, n_ctx)
            ib += 1
        else:
            qkv = matmul(h, na_wqkv[ic].astype(BF16), BF16, tn=1024)
            bias_tab = na_bias_table(na_rpb[ic], seq // GRID_W)
            o_l = neighbourhood_attention(qkv, bias_tab, n_ctx)
            o_c = context_attention(qkv, n_ctx)
            o = jnp.concatenate([o_c, o_l], axis=1)
            xj = matmul(o, na_wo[ic].astype(BF16), F32, res=xj, gate=g_m, n_ctx=n_ctx)
            ic += 1

        hp, aff, afft = norm_mod(xj, norm_ffn[layer], sh_f, sc_f, n_ctx, router_w=router_w[layer])
        sets = [(n_ctx, seq, EC_CAPACITY_FACTOR * seq // n_exp)]
        if need_ctx:
            sets.append((0, n_ctx, EC_CAPACITY_FACTOR * n_ctx // n_exp))
        idx = jnp.swapaxes(route(aff, sets)[:, :, :n_exp], 1, 2)
        y = expert_ffn(idx, hp, afft, g_f, exp_w1, exp_w3, exp_w2, layer, sets[0][2])
        xj = moe_scatter(idx, xj, y)

    return final_norm(xj, final_norm_w, n_ctx)
```

```python
import functools
import math

import jax
import jax.numpy as jnp
import numpy as np
from jax import lax
from jax.experimental import pallas as pl
from jax.experimental.pallas import tpu as pltpu

F32 = jnp.float32
BF16 = jnp.bfloat16

GRID_W = 64
NORM_EPS = 1e-6
ROPE_BASE = 10000.0
DA_HEADS = 8
DA_HEAD_DIM = 64
DA_KV_CHUNK = 768
DA_GROUP_LANES = 256
DA_Q_TILE = 512
DA_SCORE_BUFFERS = 4
DA_SUM_ROWS = 16
SSM_HEAD_DIM = 64
SSM_GROUPS = 4
SSM_STATE = 128
SSM_CHUNK = 128
NA_HEADS = 16
NA_ROWS = 8
NA_COLS = 16
N_MIXERS = 3
EC_CAPACITY_FACTOR = 2
MASK_VALUE = -1e30
ONE_BITS = 0x3F800000
LANES = 128
VMEM_LIMIT = 56 * 1024 * 1024


def _params(*sem):
    return pltpu.CompilerParams(dimension_semantics=sem, vmem_limit_bytes=VMEM_LIMIT)


def _split3(a):
    hi = a.astype(BF16)
    r1 = a - hi.astype(F32)
    mid = r1.astype(BF16)
    lo = (r1 - mid.astype(F32)).astype(BF16)
    return hi, mid, lo


def _dot(a, b):
    return jnp.dot(a, b, preferred_element_type=F32)


def _dot_nt(a, b):
    return lax.dot_general(a, b, (((1,), (1,)), ((), ())), preferred_element_type=F32)


def _dot_tn(a, b):
    return lax.dot_general(a, b, (((0,), (0,)), ((), ())), preferred_element_type=F32)


def _ada_kernel(c_ref, w_ref, b_ref, o_ref):
    c = c_ref[...]
    cond = c * jax.nn.sigmoid(c)
    o_ref[0] = jnp.dot(cond, w_ref[0], preferred_element_type=F32,
                       precision=lax.Precision.HIGHEST) + b_ref[0]


def ada_modulation(cond_rows, ada_w, ada_b):
    depth, d, n6 = ada_w.shape
    tn = 1536
    return pl.pallas_call(
        _ada_kernel,
        grid=(depth, n6 // tn),
        in_specs=[pl.BlockSpec((8, d), lambda l, j: (0, 0)),
                  pl.BlockSpec((1, d, tn), lambda l, j: (l, 0, j)),
                  pl.BlockSpec((1, 1, tn), lambda l, j: (l, 0, j))],
        out_specs=pl.BlockSpec((1, 8, tn), lambda l, j: (l, 0, j)),
        out_shape=jax.ShapeDtypeStruct((depth, 8, n6), F32),
        compiler_params=_params("parallel", "parallel"),
        name="ada_modulation",
    )(cond_rows, ada_w, ada_b.reshape(depth, 1, n6))


def _row_select(i, tm, n_ctx, vec2):
    row = i * tm + lax.broadcasted_iota(jnp.int32, (tm, 1), 0)
    return jnp.where(row >= n_ctx, vec2[1:2, :], vec2[0:1, :])


def _norm_mod_kernel(x_ref, w_ref, sh_ref, sc_ref, *rest, tm, n_ctx, n_exp):
    i = pl.program_id(1)
    xf = x_ref[0]
    y = xf * lax.rsqrt(jnp.mean(xf * xf, axis=-1, keepdims=True) + NORM_EPS) * w_ref[...]
    sc = _row_select(i, tm, n_ctx, sc_ref[0])
    sh = _row_select(i, tm, n_ctx, sh_ref[0])
    h = y * (1.0 + sc) + sh
    if not n_exp:
        (h_ref,) = rest
        h_ref[0] = h.astype(h_ref.dtype)
        return
    rwt_ref, hp_ref, aff_ref, afft_ref = rest
    bits = lax.bitcast_convert_type(h.astype(BF16).astype(F32), jnp.uint32)
    dh = bits.shape[1] // 2
    hp_ref[0] = (bits[:, :dh] >> 16) | bits[:, dh:]
    logits = lax.dot_general(rwt_ref[...], h, (((1,), (1,)), ((), ())),
                             preferred_element_type=F32, precision=lax.Precision.HIGHEST)
    e = jnp.exp(logits - jnp.max(logits, axis=0, keepdims=True))
    aff = e / jnp.sum(e, axis=0, keepdims=True)
    aff_ref[0] = aff
    afft_ref[0] = jnp.concatenate([aff, jnp.zeros((LANES - n_exp, aff.shape[1]), F32)], axis=0).T


def norm_mod(x, w, shift, scale, n_ctx, router_w=None, tm=256):
    b, n, d = x.shape
    n_exp = 0 if router_w is None else router_w.shape[1]
    in_specs = [pl.BlockSpec((1, tm, d), lambda bi, i: (bi, i, 0)),
                pl.BlockSpec((1, d), lambda bi, i: (0, 0)),
                pl.BlockSpec((1, 2, d), lambda bi, i: (bi, 0, 0)),
                pl.BlockSpec((1, 2, d), lambda bi, i: (bi, 0, 0))]
    args = [x, w.reshape(1, d), shift, scale]
    if not n_exp:
        out_specs = [pl.BlockSpec((1, tm, d), lambda bi, i: (bi, i, 0))]
        out_shape = [jax.ShapeDtypeStruct((b, n, d), BF16)]
    else:
        in_specs += [pl.BlockSpec((n_exp, d), lambda bi, i: (0, 0))]
        args += [router_w.T]
        out_specs = [pl.BlockSpec((1, tm, d // 2), lambda bi, i: (bi, i, 0)),
                     pl.BlockSpec((1, n_exp, tm), lambda bi, i: (bi, 0, i)),
                     pl.BlockSpec((1, tm, LANES), lambda bi, i: (bi, i, 0))]
        out_shape = [jax.ShapeDtypeStruct((b, n, d // 2), jnp.uint32),
                     jax.ShapeDtypeStruct((b, n_exp, n), F32),
                     jax.ShapeDtypeStruct((b, n, LANES), F32)]
    outs = pl.pallas_call(
        functools.partial(_norm_mod_kernel, tm=tm, n_ctx=n_ctx, n_exp=n_exp),
        grid=(b, n // tm),
        in_specs=in_specs, out_specs=out_specs, out_shape=out_shape,
        compiler_params=_params("parallel", "parallel"),
        name="norm_mod_router" if n_exp else "norm_mod",
    )(*args)
    return outs if n_exp else outs[0]


def _final_norm_kernel(x_ref, w_ref, o_ref):
    xf = x_ref[0]
    o_ref[0] = xf * lax.rsqrt(jnp.mean(xf * xf, axis=-1, keepdims=True) + NORM_EPS) * w_ref[...]


def final_norm(x, w, n_ctx, tm=256):
    b, n, d = x.shape
    off = n_ctx // tm
    return pl.pallas_call(
        _final_norm_kernel,
        grid=(b, (n - n_ctx) // tm),
        in_specs=[pl.BlockSpec((1, tm, d), lambda bi, i: (bi, i + off, 0)),
                  pl.BlockSpec((1, d), lambda bi, i: (0, 0))],
        out_specs=pl.BlockSpec((1, tm, d), lambda bi, i: (bi, i, 0)),
        out_shape=jax.ShapeDtypeStruct((b, n - n_ctx, d), F32),
        compiler_params=_params("parallel", "parallel"),
        name="final_norm",
    )(x, w.reshape(1, d))


def _matmul_kernel(a_ref, w_ref, *rest, tm, n_ctx, residual):
    if residual:
        res_ref, g_ref, o_ref = rest
    else:
        (o_ref,) = rest
    acc = _dot(a_ref[0], w_ref[...])
    if residual:
        g = _row_select(pl.program_id(1), tm, n_ctx, g_ref[0])
        acc = res_ref[0] + g * acc
    o_ref[0] = acc.astype(o_ref.dtype)


def matmul(a, w, out_dtype, res=None, gate=None, n_ctx=0, tm=768, tn=None):
    b, n, k = a.shape
    m = w.shape[1]
    tn = m if tn is None else tn
    residual = res is not None
    in_specs = [pl.BlockSpec((1, tm, k), lambda bi, i, j: (bi, i, 0)),
                pl.BlockSpec((k, tn), lambda bi, i, j: (0, j))]
    args = [a, w]
    if residual:
        in_specs += [pl.BlockSpec((1, tm, tn), lambda bi, i, j: (bi, i, j)),
                     pl.BlockSpec((1, 2, tn), lambda bi, i, j: (bi, 0, j))]
        args += [res, gate]
    return pl.pallas_call(
        functools.partial(_matmul_kernel, tm=tm, n_ctx=n_ctx, residual=residual),
        grid=(b, n // tm, m // tn),
        in_specs=in_specs,
        out_specs=pl.BlockSpec((1, tm, tn), lambda bi, i, j: (bi, i, j)),
        out_shape=jax.ShapeDtypeStruct((b, n, m), out_dtype),
        compiler_params=_params("parallel", "parallel", "arbitrary"),
        name="matmul_res" if residual else "matmul",
    )(*args)


def _rotate_pairs(x, axis):
    seg = lax.broadcasted_iota(jnp.int32, x.shape, axis) // 16
    n = x.shape[axis]
    return jnp.where(seg % 2 == 0, pltpu.roll(x, n - 16, axis), pltpu.roll(x, 16, axis))


def _da_qkv_kernel(h_ref, wqt_ref, wk_ref, wvt_ref, cq_ref, sq_ref, ck_ref, sk_ref,
                   qt_ref, k_ref, vt_ref, *, heads):
    h = h_ref[0]
    qt = _dot_nt(wqt_ref[...], h)
    cq, sq = cq_ref[...], sq_ref[...]
    for hd in range(heads):
        rs = slice(hd * LANES, (hd + 1) * LANES)
        x = qt[rs]
        qt_ref[0, rs, :] = (x * cq + _rotate_pairs(x, 0) * sq).astype(qt_ref.dtype)
    k = _dot(h, wk_ref[...])
    ck, sk = ck_ref[...], sk_ref[...]
    for hd in range(heads):
        cs = slice(hd * LANES, (hd + 1) * LANES)
        x = k[:, cs]
        k_ref[0, :, cs] = (x * ck + _rotate_pairs(x, 1) * sk).astype(k_ref.dtype)
    vt = _dot_nt(wvt_ref[...], h).astype(vt_ref.dtype)
    ones = jnp.ones((DA_SUM_ROWS, vt.shape[1]), vt_ref.dtype)
    for hd in range(heads):
        vt_ref[0, 0, hd, :LANES, :] = vt[hd * LANES:(hd + 1) * LANES]
        vt_ref[0, 0, hd, LANES:, :] = ones


def da_qkv(h, wqkv, tabs, tm):
    b, n, d = h.shape
    m = wqkv.shape[1] // 3
    cq_t, sq_t, ck, sk = tabs
    wqt = wqkv[:, :m].T.astype(BF16)
    wk = wqkv[:, m:2 * m].astype(BF16)
    wvt = wqkv[:, 2 * m:].T.astype(BF16)
    full = lambda r, c: pl.BlockSpec((r, c), lambda bi, i: (0, 0))
    return pl.pallas_call(
        functools.partial(_da_qkv_kernel, heads=m // LANES),
        grid=(b, n // tm),
        in_specs=[pl.BlockSpec((1, tm, d), lambda bi, i: (bi, i, 0)),
                  full(m, d), full(d, m), full(m, d),
                  pl.BlockSpec((LANES, tm), lambda bi, i: (0, i)),
                  pl.BlockSpec((LANES, tm), lambda bi, i: (0, i)),
                  pl.BlockSpec((tm, LANES), lambda bi, i: (i, 0)),
                  pl.BlockSpec((tm, LANES), lambda bi, i: (i, 0))],
        out_specs=[pl.BlockSpec((1, m, tm), lambda bi, i: (bi, 0, i)),
                   pl.BlockSpec((1, tm, m), lambda bi, i: (bi, i, 0)),
                   pl.BlockSpec((1, 1, m // LANES, LANES + DA_SUM_ROWS, tm), lambda bi, i: (bi, i, 0, 0, 0))],
        out_shape=[jax.ShapeDtypeStruct((b, m, n), BF16),
                   jax.ShapeDtypeStruct((b, n, m), BF16),
                   jax.ShapeDtypeStruct((b, n // tm, m // LANES, LANES + DA_SUM_ROWS, tm), BF16)],
        compiler_params=_params("parallel", "parallel"),
        name="da_qkv",
    )(h, wqt, wk, wvt, cq_t, sq_t, ck, sk)


def rope_tables(n_ctx, seq, q_scale):
    d = DA_HEAD_DIM
    t = jnp.arange(seq)
    rows = (t // GRID_W).astype(F32)
    cols = (t % GRID_W).astype(F32)
    quarter = d // 4
    freqs = ROPE_BASE ** (-jnp.arange(quarter, dtype=F32) / quarter)
    ang_r = rows[:, None] * freqs[None, :]
    ang_c = cols[:, None] * freqs[None, :]
    ang = jnp.concatenate([ang_r, ang_r, ang_c, ang_c], axis=-1)
    cos = jnp.cos(ang)
    sign = jnp.concatenate([-jnp.ones((quarter,), F32), jnp.ones((quarter,), F32)] * 2)
    sin = jnp.sin(ang) * sign[None, :]
    cos = jnp.concatenate([jnp.ones((n_ctx, d), F32), cos], axis=0)
    sin = jnp.concatenate([jnp.zeros((n_ctx, d), F32), sin], axis=0)
    cos = jnp.concatenate([cos, cos], axis=-1)
    sin = jnp.concatenate([sin, sin], axis=-1)
    return (cos * q_scale).T, (sin * q_scale).T, cos, sin


def _diff_attn_kernel(*refs, nq, tk, kw, nchunks, lam_init):
    qt_refs = refs[:nq]
    k_ref, vt_ref, lq1_ref, lk1_ref, lq2_ref, lk2_ref, sub_ref, o_ref, acc_ref, s_ref = refs[nq:]
    gw = DA_GROUP_LANES
    ngroups = 2 * nq
    sub = lax.broadcasted_iota(jnp.int32, (LANES, gw), 0)
    zero = jnp.zeros((LANES, gw), qt_refs[0].dtype)
    qg = []
    for g in range(ngroups):
        q = qt_refs[g // 2][0]
        qg.append(jnp.where((sub < DA_HEAD_DIM) if g % 2 == 0 else (sub >= DA_HEAD_DIM), q, zero))
    acc_ref[...] = jnp.zeros(acc_ref.shape, F32)
    half = kw // 2

    def col_reduce(x, op):
        return op(op(x.reshape(8, x.shape[0] // 8, x.shape[1]), axis=0), axis=0, keepdims=True)

    def scores(t):
        c, g = divmod(t, ngroups)
        mx = None
        for hf in range(2):
            k = k_ref[0, c * tk + hf * half:c * tk + (hf + 1) * half, :]
            s = _dot(k, qg[g])
            s_ref[t % DA_SCORE_BUFFERS, hf * half:(hf + 1) * half, :] = s
            hm = col_reduce(s, jnp.max)
            mx = hm if mx is None else jnp.maximum(mx, hm)
        return mx

    def softmax_pv(t, mx, ml):
        c, g = divmod(t, ngroups)
        m_old, l_old = ml
        vt = vt_ref[0, c, 0]
        m_new = jnp.maximum(m_old, mx)
        alpha = jnp.exp2(m_old - m_new)
        pb = jnp.exp2(s_ref[t % DA_SCORE_BUFFERS] - m_new).astype(BF16)
        pv = _dot(vt[:, :kw], pb)
        acc_ref[g] = alpha * acc_ref[g] + pv[:LANES]
        return m_new, alpha * l_old + pv[LANES:LANES + 1]

    nsteps = nchunks * ngroups
    ahead = DA_SCORE_BUFFERS - 1
    carry = [(jnp.full((1, gw), -jnp.inf, F32), jnp.zeros((1, gw), F32))] * ngroups
    maxima = {t: scores(t) for t in range(min(ahead, nsteps))}
    for t in range(nsteps):
        if t + ahead < nsteps:
            maxima[t + ahead] = scores(t + ahead)
        carry[t % ngroups] = softmax_pv(t, maxima.pop(t), carry[t % ngroups])
    lam = (jnp.exp(jnp.sum(lq1_ref[...] * lk1_ref[...], axis=-1, keepdims=True))
           - jnp.exp(jnp.sum(lq2_ref[...] * lk2_ref[...], axis=-1, keepdims=True)) + lam_init)
    for qh in range(ngroups // 2):
        od = acc_ref[2 * qh] / carry[2 * qh][1] - lam * (acc_ref[2 * qh + 1] / carry[2 * qh + 1][1])
        y = od * lax.rsqrt(jnp.mean(od * od, axis=0, keepdims=True) + NORM_EPS) * sub_ref[...]
        o_ref[0, qh * gw:(qh + 1) * gw, :] = (y * (1.0 - lam_init)).T.astype(o_ref.dtype)


def diff_attention(qt, k, vt, lam_params, subln, lam_init, q_start, n_q, n_k, tq):
    b, m, n = qt.shape
    h = m // LANES
    tk = vt.shape[-1]
    kw = min(tk, n_k)
    nchunks = n_k // kw
    gw = DA_GROUP_LANES
    nq = tq // gw
    assert q_start % gw == 0 and tq % gw == 0 and n_q % tq == 0
    qoff = q_start // gw
    vec = lambda a: a.reshape(1, -1).astype(F32)
    small = pl.BlockSpec((1, DA_HEAD_DIM), lambda bi, hi, i: (0, 0))
    q_specs = [pl.BlockSpec((1, LANES, gw), functools.partial(
        lambda bi, hi, i, j: (bi, hi, qoff + nq * i + j), j=j)) for j in range(nq)]
    return pl.pallas_call(
        functools.partial(_diff_attn_kernel, nq=nq, tk=tk, kw=kw, nchunks=nchunks, lam_init=lam_init),
        grid=(b, h, n_q // tq),
        in_specs=q_specs + [
                  pl.BlockSpec((1, n_k, LANES), lambda bi, hi, i: (bi, 0, hi)),
                  pl.BlockSpec((1, nchunks, 1, vt.shape[3], tk), lambda bi, hi, i: (bi, 0, hi, 0, 0)),
                  small, small, small, small,
                  pl.BlockSpec((LANES, 1), lambda bi, hi, i: (0, 0))],
        out_specs=pl.BlockSpec((1, tq, LANES), lambda bi, hi, i: (bi, i, hi)),
        out_shape=jax.ShapeDtypeStruct((b, n_q, m), BF16),
        scratch_shapes=[pltpu.VMEM((2 * tq // DA_GROUP_LANES, LANES, DA_GROUP_LANES), F32),
                        pltpu.VMEM((DA_SCORE_BUFFERS, kw, DA_GROUP_LANES), F32)],
        compiler_params=_params("parallel", "parallel", "arbitrary"),
        name="diff_attention",
    )(*([qt] * nq), k, vt, *[vec(p) for p in lam_params], subln.reshape(-1, 1).astype(F32))


def _pair_scores(qp, kp):
    lane = lax.broadcasted_iota(jnp.int32, qp.shape, 1)
    zero = jnp.zeros_like(qp)
    qs = jnp.concatenate([jnp.where(lane < 64, qp, zero), jnp.where(lane >= 64, qp, zero)], axis=0)
    return _dot_nt(qs, kp)


def _pair_softmax_pv(s, vp, bias, n_bias):
    tq = s.shape[0] // 2
    if n_bias:
        s_w = s[:, :n_bias] + bias
        s_c = s[:, n_bias:]
        m = jnp.maximum(jnp.max(s_w, axis=-1, keepdims=True), jnp.max(s_c, axis=-1, keepdims=True))
        p_w = jnp.exp(s_w - m)
        p_c = jnp.exp(s_c - m)
        l = jnp.sum(p_w, axis=-1, keepdims=True) + jnp.sum(p_c, axis=-1, keepdims=True)
        pv = _dot(p_w.astype(BF16), vp[:n_bias]) + _dot(p_c.astype(BF16), vp[n_bias:])
    else:
        m = jnp.max(s, axis=-1, keepdims=True)
        p = jnp.exp(s - m)
        l = jnp.sum(p, axis=-1, keepdims=True)
        pv = _dot(p.astype(BF16), vp)
    o = pv / l
    lane_o = lax.broadcasted_iota(jnp.int32, (tq, LANES), 1)
    return jnp.where(lane_o < 64, o[:tq], o[tq:])


def _pairs_attention(q_of, k_of, v_of, bias_of, n_bias, o_ref):
    npairs = NA_HEADS // 2
    s_next = _pair_scores(q_of(0), k_of(0))
    for hp in range(npairs):
        s_cur = s_next
        if hp + 1 < npairs:
            s_next = _pair_scores(q_of(hp + 1), k_of(hp + 1))
        o = _pair_softmax_pv(s_cur, v_of(hp), bias_of(hp), n_bias)
        o_ref[0, :, hp * LANES:(hp + 1) * LANES] = o.astype(o_ref.dtype)


def _na_kernel(*refs, wr, n_ctx, scale):
    q_ref = refs[0]
    k_refs = refs[1:1 + wr]
    v_refs = refs[1 + wr:1 + 2 * wr]
    kc_ref, vc_ref, bias_ref, o_ref, kbuf, vbuf = refs[1 + 2 * wr:]
    nw = wr * GRID_W
    for w in range(wr):
        kbuf[w * GRID_W:(w + 1) * GRID_W, :] = k_refs[w][0]
        vbuf[w * GRID_W:(w + 1) * GRID_W, :] = v_refs[w][0]
    kbuf[nw:nw + n_ctx, :] = kc_ref[0]
    vbuf[nw:nw + n_ctx, :] = vc_ref[0]
    cols = lambda hp: slice(hp * LANES, (hp + 1) * LANES)
    _pairs_attention(lambda hp: q_ref[0, :, cols(hp)] * scale,
                     lambda hp: kbuf[:, cols(hp)],
                     lambda hp: vbuf[:, cols(hp)],
                     lambda hp: bias_ref[0, 2 * hp:2 * hp + 2].reshape(2 * GRID_W, nw),
                     nw, o_ref)


def na_bias_table(rpb, rows):
    wr = min(NA_ROWS, rows)
    qcol = np.arange(GRID_W)
    col_start = np.clip(qcol - NA_COLS // 2, 0, GRID_W - NA_COLS)
    kc = np.arange(GRID_W)
    inside = (kc[None, :] >= col_start[:, None]) & (kc[None, :] < col_start[:, None] + NA_COLS)
    dc = np.clip(kc[None, :] - qcol[:, None] + NA_COLS - 1, 0, 2 * NA_COLS - 2)
    dr = np.arange(wr)[None, :] - np.arange(wr)[:, None] + NA_ROWS - 1
    sel_r = (dr[:, :, None] == np.arange(2 * NA_ROWS - 1)).astype(np.float32)
    sel_c = (dc[:, :, None] == np.arange(2 * NA_COLS - 1)).astype(np.float32)
    tab = jnp.einsum("cwa,hab,qkb->chqwk", sel_r, rpb.astype(F32), sel_c,
                     precision=lax.Precision.HIGHEST)
    tab = jnp.where(inside[None, None, :, None, :], tab, MASK_VALUE)
    return tab.reshape(wr, rpb.shape[0], GRID_W, wr * GRID_W)


def neighbourhood_attention(qkv, bias_tab, n_ctx):
    b, n, m3 = qkv.shape
    m = m3 // 3
    rows = (n - n_ctx) // GRID_W
    wr = min(NA_ROWS, rows)
    cb = n_ctx // GRID_W
    rs = lambda r: jnp.clip(r - wr // 2, 0, rows - wr)
    blk = lambda f: pl.BlockSpec((1, GRID_W, m), f)
    in_specs = [blk(lambda bi, r: (bi, cb + r, 0))]
    in_specs += [blk(functools.partial(lambda bi, r, w: (bi, cb + rs(r) + w, 1), w=w)) for w in range(wr)]
    in_specs += [blk(functools.partial(lambda bi, r, w: (bi, cb + rs(r) + w, 2), w=w)) for w in range(wr)]
    in_specs += [pl.BlockSpec((1, n_ctx, m), lambda bi, r: (bi, 0, 1)),
                 pl.BlockSpec((1, n_ctx, m), lambda bi, r: (bi, 0, 2)),
                 pl.BlockSpec((1, NA_HEADS, GRID_W, wr * GRID_W), lambda bi, r: (r - rs(r), 0, 0, 0))]
    nk = wr * GRID_W + n_ctx
    return pl.pallas_call(
        functools.partial(_na_kernel, wr=wr, n_ctx=n_ctx, scale=(m // NA_HEADS) ** -0.5),
        grid=(b, rows),
        in_specs=in_specs,
        out_specs=pl.BlockSpec((1, GRID_W, m), lambda bi, r: (bi, r, 0)),
        out_shape=jax.ShapeDtypeStruct((b, n - n_ctx, m), BF16),
        scratch_shapes=[pltpu.VMEM((nk, m), BF16), pltpu.VMEM((nk, m), BF16)],
        compiler_params=_params("parallel", "arbitrary"),
        name="neighbourhood_attention",
    )(*([qkv] * (1 + 2 * wr + 2)), bias_tab)


def _ctx_attn_kernel(q_ref, k_ref, v_ref, o_ref, *, scale):
    cols = lambda hp: slice(hp * LANES, (hp + 1) * LANES)
    _pairs_attention(lambda hp: q_ref[0, :, cols(hp)] * scale,
                     lambda hp: k_ref[0, :, cols(hp)],
                     lambda hp: v_ref[0, :, cols(hp)],
                     lambda hp: None, 0, o_ref)


def context_attention(qkv, n_ctx):
    b, n, m3 = qkv.shape
    m = m3 // 3
    return pl.pallas_call(
        functools.partial(_ctx_attn_kernel, scale=(m // NA_HEADS) ** -0.5),
        grid=(b,),
        in_specs=[pl.BlockSpec((1, n_ctx, m), lambda bi: (bi, 0, 0)),
                  pl.BlockSpec((1, n_ctx, m), lambda bi: (bi, 0, 1)),
                  pl.BlockSpec((1, n_ctx, m), lambda bi: (bi, 0, 2))],
        out_specs=pl.BlockSpec((1, n_ctx, m), lambda bi: (bi, 0, 0)),
        out_shape=jax.ShapeDtypeStruct((b, n_ctx, m), BF16),
        compiler_params=_params("parallel"),
        name="context_attention",
    )(qkv, qkv, qkv)


def _conv_silu_kernel(x_ref, w_ref, b_ref, o_ref, *, n, n_ctx):
    x = x_ref[0]
    row = lax.broadcasted_iota(jnp.int32, x.shape, 0)
    xm = jnp.where((row == 0) | (row == n_ctx), 0.0, pltpu.roll(x, 1, 0))
    xp = jnp.where((row == n_ctx - 1) | (row == n - 1), 0.0, pltpu.roll(x, n - 1, 0))
    y = w_ref[0:1, :] * xm + w_ref[1:2, :] * x + w_ref[2:3, :] * xp + b_ref[...]
    o_ref[0] = y * jax.nn.sigmoid(y)


def conv_silu(x, w, bias, n_ctx):
    b, n, c = x.shape
    return pl.pallas_call(
        functools.partial(_conv_silu_kernel, n=n, n_ctx=n_ctx),
        grid=(b, c // LANES),
        in_specs=[pl.BlockSpec((1, n, LANES), lambda bi, j: (bi, 0, j)),
                  pl.BlockSpec((3, LANES), lambda bi, j: (0, j)),
                  pl.BlockSpec((1, LANES), lambda bi, j: (0, j))],
        out_specs=pl.BlockSpec((1, n, LANES), lambda bi, j: (bi, 0, j)),
        out_shape=jax.ShapeDtypeStruct((b, n, c), F32),
        compiler_params=_params("parallel", "parallel"),
        name="conv_silu",
    )(x, w, bias.reshape(1, c))


def _softplus(x):
    return jnp.maximum(x, 0.0) + jnp.log1p(jnp.exp(-jnp.abs(x)))


def _ssd_kernel(x_ref, b_ref, c_ref, dt_ref, dtb_ref, alog_ref, e_ref, y_ref, state_ref,
                *, direction, nheads):
    L = SSM_CHUNK
    P = SSM_HEAD_DIM
    G = SSM_GROUPS
    hpg = nheads // G
    reverse = direction == 1

    @pl.when(pl.program_id(1) == 0)
    def _():
        state_ref[...] = jnp.zeros(state_ref.shape, F32)

    li = lax.broadcasted_iota(jnp.int32, (L, L), 0)
    si = lax.broadcasted_iota(jnp.int32, (L, L), 1)
    allowed = (si >= li) if reverse else (si <= li)
    tri = jnp.where(allowed, 1.0, 0.0).astype(BF16)
    tri_t = jnp.where((li >= si) if reverse else (li <= si), 1.0, 0.0).astype(BF16)

    dt_raw = dt_ref[0][:, direction * nheads:(direction + 1) * nheads]
    dt = _softplus(dt_raw + dtb_ref[...])
    a = dt * (-jnp.exp(alog_ref[...]))
    a3 = _split3(a)
    acs = sum(_dot(tri, p) for p in a3)
    acs_t = sum(_dot_tn(p, tri_t) for p in a3)
    expand = e_ref[...]
    acs_x = sum(_dot(p, expand) for p in _split3(acs))
    dt_x = sum(_dot(p, expand) for p in _split3(dt))
    end = 0 if reverse else L - 1
    total_x = acs_x[end:end + 1, :]

    xd = x_ref[0] * dt_x
    xd_b = xd.astype(BF16)
    x_state = (xd * jnp.exp(total_x - acs_x)).astype(BF16)
    exp_acs_x = jnp.exp(acs_x)
    chunk_decay_x = jnp.exp(total_x)
    lane = lax.broadcasted_iota(jnp.int32, (L, LANES), 1)

    for g in range(G):
        gs = slice(g * SSM_STATE, (g + 1) * SSM_STATE)
        hs = slice(g * hpg * P, (g + 1) * hpg * P)
        cg = c_ref[0][:, gs].astype(BF16)
        bg = b_ref[0][:, gs].astype(BF16)
        cb = _dot_nt(cg, bg)
        h_prev = state_ref[:, hs]
        y_off = _dot(cg, h_prev.astype(BF16)) * exp_acs_x[:, hs]
        state_ref[:, hs] = h_prev * chunk_decay_x[:, hs] + _dot_tn(bg, x_state[:, hs])
        for rp in range(hpg // 2):
            mats = []
            for r in (2 * rp, 2 * rp + 1):
                hidx = g * hpg + r
                seg = acs[:, hidx:hidx + 1] - acs_t[hidx:hidx + 1, :]
                mats.append((cb * jnp.exp(jnp.where(allowed, seg, -jnp.inf))).astype(BF16))
            ps = slice((g * hpg + 2 * rp) * P, (g * hpg + 2 * rp + 2) * P)
            yd = _dot(jnp.concatenate(mats, axis=0), xd_b[:, ps])
            y_ref[0, :, ps] = jnp.where(lane < P, yd[:L], yd[L:]) + y_off[:, 2 * rp * P:(2 * rp + 2) * P]


def ssd_scan(xbc, dt_raw, dt_bias, a_log, expand, direction, nheads, n_ctx):
    b, n, _ = xbc.shape
    d_inner = nheads * SSM_HEAD_DIM
    gn = SSM_GROUPS * SSM_STATE
    L = SSM_CHUNK
    nc = n // L
    ncc = n_ctx // L
    if direction == 0:
        cidx = lambda c: c
    else:
        cidx = lambda c: jnp.where(c < ncc, ncc - 1 - c, nc - 1 - (c - ncc))
    return pl.pallas_call(
        functools.partial(_ssd_kernel, direction=direction, nheads=nheads),
        grid=(b, nc),
        in_specs=[pl.BlockSpec((1, L, d_inner), lambda bi, c: (bi, cidx(c), 0)),
                  pl.BlockSpec((1, L, gn), lambda bi, c: (bi, cidx(c), d_inner // gn)),
                  pl.BlockSpec((1, L, gn), lambda bi, c: (bi, cidx(c), d_inner // gn + 1)),
                  pl.BlockSpec((1, L, LANES), lambda bi, c: (bi, cidx(c), 0)),
                  pl.BlockSpec((1, nheads), lambda bi, c: (0, 0)),
                  pl.BlockSpec((1, nheads), lambda bi, c: (0, 0)),
                  pl.BlockSpec((nheads, d_inner), lambda bi, c: (0, 0))],
        out_specs=pl.BlockSpec((1, L, d_inner), lambda bi, c: (bi, cidx(c), 0)),
        out_shape=jax.ShapeDtypeStruct((b, n, d_inner), F32),
        scratch_shapes=[pltpu.VMEM((SSM_STATE, d_inner), F32)],
        compiler_params=_params("parallel", "arbitrary"),
        name="ssd_scan_%d" % direction,
    )(xbc, xbc, xbc, dt_raw, dt_bias[direction:direction + 1], a_log[direction:direction + 1], expand)


def _ssm_out_kernel(y0_ref, y1_ref, x_ref, z_ref, d_ref, nw_ref, w_ref, res_ref, g_ref, o_ref,
                    *, tm, n_ctx):
    z = z_ref[0]
    u = (y0_ref[0] + y1_ref[0] + d_ref[...] * x_ref[0]) * (z * jax.nn.sigmoid(z))
    un = u * lax.rsqrt(jnp.mean(u * u, axis=-1, keepdims=True) + NORM_EPS) * nw_ref[...]
    g = _row_select(pl.program_id(1), tm, n_ctx, g_ref[0])
    o_ref[0] = res_ref[0] + g * _dot(un.astype(BF16), w_ref[...])


def ssm_out(y0, y1, xbc, z, d_chan, norm_w, w_out, res, gate, n_ctx, tm=256):
    b, n, di = y0.shape
    d = w_out.shape[1]
    row = lambda c: pl.BlockSpec((1, tm, c), lambda bi, i: (bi, i, 0))
    vec = lambda c: pl.BlockSpec((1, c), lambda bi, i: (0, 0))
    return pl.pallas_call(
        functools.partial(_ssm_out_kernel, tm=tm, n_ctx=n_ctx),
        grid=(b, n // tm),
        in_specs=[row(di), row(di), row(di), row(di), vec(di), vec(di),
                  pl.BlockSpec((di, d), lambda bi, i: (0, 0)), row(d),
                  pl.BlockSpec((1, 2, d), lambda bi, i: (bi, 0, 0))],
        out_specs=row(d),
        out_shape=jax.ShapeDtypeStruct((b, n, d), F32),
        compiler_params=_params("parallel", "parallel"),
        name="ssm_out",
    )(y0, y1, xbc, z, d_chan, norm_w.reshape(1, di), w_out, res, gate)


def _cumsum_lanes(x):
    n = x.shape[-1]
    lane = lax.broadcasted_iota(jnp.int32, x.shape, x.ndim - 1)
    s = 1
    while s < n:
        x = x + jnp.where(lane >= s, pltpu.roll(x, s, x.ndim - 1), 0.0)
        s *= 2
    return x


def _route_set(aff, cap):
    e, t = aff.shape
    bits = lax.bitcast_convert_type(aff, jnp.int32)

    def count_ge(thr):
        return jnp.sum(jnp.where(bits >= thr, 1.0, 0.0), axis=1, keepdims=True)

    def step(_, lohi):
        lo, hi = lohi
        mid = lo + (hi - lo) // 2
        ok = count_ge(mid) >= cap
        return jnp.where(ok, mid, lo), jnp.where(ok, hi, mid)

    lo0 = jnp.zeros((e, 1), jnp.int32)
    hi0 = jnp.full((e, 1), ONE_BITS + 1, jnp.int32)
    thr, _ = lax.fori_loop(0, 31, step, (lo0, hi0))
    gt = jnp.where(bits > thr, 1.0, 0.0)
    eq = jnp.where(bits == thr, 1.0, 0.0)
    need = cap - jnp.sum(gt, axis=1, keepdims=True)
    sel = gt + eq * jnp.where(_cumsum_lanes(eq) <= need, 1.0, 0.0)
    return sel, _cumsum_lanes(sel)


def _route_kernel(aff_ref, idx_ref, cnt_ref, *, sets, jblk):
    n_exp = aff_ref.shape[1]
    for t0, t, cap, row0 in sets:
        _, cnt = _route_set(aff_ref[0, :, t0:t0 + t], cap)
        cnt_ref[:, :t] = cnt
        for jb in range(0, cap, jblk):
            nj = min(jblk, cap - jb)
            jcol = (lax.broadcasted_iota(jnp.int32, (nj, 1), 0) + jb).astype(F32)
            lane = lax.broadcasted_iota(jnp.int32, (nj, LANES), 1)

            def per_expert(ei, acc):
                row = cnt_ref[pl.ds(ei, 1), :t]
                pos = jnp.sum(jnp.where(row <= jcol, 1.0, 0.0), axis=1, keepdims=True)
                return jnp.where(lane == ei, pos, acc)

            acc = lax.fori_loop(0, n_exp, per_expert, jnp.zeros((nj, LANES), F32))
            idx_ref[0, row0 + jb:row0 + jb + nj, :] = acc.astype(jnp.int32) + t0


def route(aff, sets):
    b, e, n = aff.shape
    m = sum(s[2] for s in sets)
    tmax = max(s[1] for s in sets)
    rows, sets_r = 0, []
    for t0, t, cap in sets:
        sets_r.append((t0, t, cap, rows))
        rows += cap
    return pl.pallas_call(
        functools.partial(_route_kernel, sets=tuple(sets_r), jblk=256),
        grid=(b,),
        in_specs=[pl.BlockSpec((1, e, n), lambda bi: (bi, 0, 0))],
        out_specs=pl.BlockSpec((1, m, LANES), lambda bi: (bi, 0, 0)),
        out_shape=jax.ShapeDtypeStruct((b, m, LANES), jnp.int32),
        scratch_shapes=[pltpu.VMEM((e, tmax), F32)],
        compiler_params=_params("parallel"),
        name="route",
    )(aff)


def _expert_ffn_kernel(idx_ref, idxn_ref, hp_ref, afft_ref, g_ref, w1_ref, w3_ref, w2_ref, o_ref,
                       xw_ref, gw_ref, xs_ref, gate_ref, *, m, n_lat, rows_per_step):
    ei = pl.program_id(1)
    f = pl.program_id(2)
    slot = ei % 2

    def copy_row(src_idx_ref, dst_slot, j):
        i = src_idx_ref[0, 0, 0, j]
        xw_ref[dst_slot, pl.ds(j, 1), :] = hp_ref[0, pl.ds(i, 1), :]
        gw_ref[dst_slot, pl.ds(j, 1), :] = afft_ref[0, pl.ds(i, 1), :]

    @pl.when((f == 0) & (ei == 0))
    def _():
        def gather(g, c):
            base = pl.multiple_of(g * 8, 8)
            for r in range(8):
                copy_row(idx_ref, 0, base + r)
            return c

        lax.fori_loop(0, m // 8, gather, 0)

    @pl.when(f == 0)
    def _():
        w = xw_ref[slot]
        lo = lax.bitcast_convert_type(w << 16, F32).astype(BF16)
        hi = lax.bitcast_convert_type(w & jnp.uint32(0xFFFF0000), F32).astype(BF16)
        xs_ref[...] = jnp.concatenate([lo, hi], axis=1)
        lane = lax.broadcasted_iota(jnp.int32, gate_ref.shape, 1)
        gate = jnp.sum(jnp.where(lane == ei, gw_ref[slot], 0.0), axis=1, keepdims=True)
        gate_ref[...] = jnp.broadcast_to(gate, gate_ref.shape)
        o_ref[...] = jnp.zeros(o_ref.shape, F32)

    for r in range(rows_per_step):
        copy_row(idxn_ref, 1 - slot, f * rows_per_step + r)

    x = xs_ref[...]
    h1 = _dot(x, w1_ref[0].astype(BF16))
    h3 = _dot(x, w3_ref[0].astype(BF16))
    hid = (h1 * jax.nn.sigmoid(h1) * h3).astype(BF16)
    part = _dot(hid, w2_ref[0].astype(BF16))

    o_ref[0, 0] = o_ref[0, 0] + part

    @pl.when(f == pl.num_programs(2) - 1)
    def _():
        row = lax.broadcasted_iota(jnp.int32, (m, 1), 0)
        g = jnp.where(row < n_lat, g_ref[0, 1:2, :], g_ref[0, 0:1, :])
        o_ref[0, 0] = o_ref[0, 0] * gate_ref[:, 0:1] * g


def expert_ffn(idx, hp, afft, gate2, w1, w3, w2, layer, n_lat, tf=256):
    b, e, m = idx.shape
    n, dh = hp.shape[1:]
    d = 2 * dh
    ff = w1.shape[3]
    single = dict(pipeline_mode=pl.Buffered(1))
    nf = ff // tf
    assert m % nf == 0
    idx4 = idx.reshape(b, e, 1, m)
    return pl.pallas_call(
        functools.partial(_expert_ffn_kernel, m=m, n_lat=n_lat, rows_per_step=m // nf),
        grid=(b, e, nf),
        in_specs=[pl.BlockSpec((1, 1, 1, m), lambda bi, ei, f: (bi, ei, 0, 0), memory_space=pltpu.SMEM),
                  pl.BlockSpec((1, 1, 1, m), lambda bi, ei, f: (bi, jnp.minimum(ei + 1, e - 1), 0, 0),
                               memory_space=pltpu.SMEM),
                  pl.BlockSpec((1, n, dh), lambda bi, ei, f: (bi, 0, 0), **single),
                  pl.BlockSpec((1, n, LANES), lambda bi, ei, f: (bi, 0, 0), **single),
                  pl.BlockSpec((1, 2, d), lambda bi, ei, f: (bi, 0, 0)),
                  pl.BlockSpec((None, 1, d, tf), lambda bi, ei, f: (layer, ei, 0, f)),
                  pl.BlockSpec((None, 1, d, tf), lambda bi, ei, f: (layer, ei, 0, f)),
                  pl.BlockSpec((None, 1, tf, d), lambda bi, ei, f: (layer, ei, f, 0))],
        out_specs=pl.BlockSpec((1, 1, m, d), lambda bi, ei, f: (bi, ei, 0, 0)),
        out_shape=jax.ShapeDtypeStruct((b, e, m, d), F32),
        scratch_shapes=[pltpu.VMEM((2, m, dh), jnp.uint32), pltpu.VMEM((2, m, LANES), F32),
                        pltpu.VMEM((m, d), BF16), pltpu.VMEM((m, LANES), F32)],
        compiler_params=_params("arbitrary", "arbitrary", "arbitrary"),
        name="expert_ffn",
    )(idx4, idx4, hp, afft, gate2, w1, w3, w2)


def _scatter_kernel(idx_ref, x_ref, y_ref, o_ref, *, m):
    ei = pl.program_id(2)

    @pl.when(ei == 0)
    def _():
        o_ref[...] = x_ref[...]

    def add_rows(g, c):
        base = pl.multiple_of(g * 8, 8)
        rows = [idx_ref[0, 0, 0, base + r] for r in range(8)]
        sums = [o_ref[0, pl.ds(i, 1), :] + y_ref[0, 0, pl.ds(base + r, 1), :] for r, i in enumerate(rows)]
        for i, v in zip(rows, sums):
            o_ref[0, pl.ds(i, 1), :] = v
        return c

    lax.fori_loop(0, m // 8, add_rows, 0)


def moe_scatter(idx, x, y, tc=512):
    b, e, m = idx.shape
    n, d = x.shape[1:]
    return pl.pallas_call(
        functools.partial(_scatter_kernel, m=m),
        grid=(b, d // tc, e),
        in_specs=[pl.BlockSpec((1, 1, 1, m), lambda bi, ci, ei: (bi, ei, 0, 0), memory_space=pltpu.SMEM),
                  pl.BlockSpec((1, n, tc), lambda bi, ci, ei: (bi, 0, ci), pipeline_mode=pl.Buffered(1)),
                  pl.BlockSpec((1, 1, m, tc), lambda bi, ci, ei: (bi, ei, 0, ci))],
        out_specs=pl.BlockSpec((1, n, tc), lambda bi, ci, ei: (bi, 0, ci), pipeline_mode=pl.Buffered(1)),
        out_shape=jax.ShapeDtypeStruct((b, n, d), F32),
        compiler_params=_params("parallel", "parallel", "arbitrary"),
        name="moe_scatter",
    )(idx.reshape(b, e, 1, m), x, y)


def _lambda_init(layer):
    return 0.8 - 0.6 * math.exp(-0.3 * layer)


def kernel(x, c, ctx, c_ctx, ada_w, ada_b, norm_mix, norm_ffn, final_norm_w,
           da_wqkv, da_wo, da_lam_q1, da_lam_k1, da_lam_q2, da_lam_k2, da_subln,
           ssm_w_in, ssm_conv_w, ssm_conv_b, ssm_dt_bias, ssm_A_log, ssm_D, ssm_norm, ssm_w_out,
           na_wqkv, na_wo, na_rpb, router_w, exp_w1, exp_w3, exp_w2):
    b, seq, d = x.shape
    n_ctx = ctx.shape[1]
    n = n_ctx + seq
    depth = ada_w.shape[0]
    n_exp = router_w.shape[-1]

    cond_rows = jnp.zeros((8, d), F32).at[:b].set(c).at[b].set(c_ctx)
    mods = ada_modulation(cond_rows, ada_w, ada_b).reshape(depth, 8, 6, d)
    mods = jnp.stack([jnp.broadcast_to(mods[:, b:b + 1], (depth, b, 6, d)), mods[:, :b]], axis=2)

    xj = jnp.concatenate([ctx, x], axis=1)
    rope_tabs = rope_tables(n_ctx, seq, DA_HEAD_DIM ** -0.5 * math.log2(math.e))

    ia = ib = ic = 0
    for layer in range(depth):
        need_ctx = layer < depth - 1
        sh_m, sc_m, g_m, sh_f, sc_f, g_f = [mods[layer, :, :, k] for k in range(6)]
        h = norm_mod(xj, norm_mix[layer], sh_m, sc_m, n_ctx)
        kind = layer % N_MIXERS
        if kind == 0:
            qt, k, vt = da_qkv(h, da_wqkv[ia], rope_tabs, DA_KV_CHUNK)
            lam_params = (da_lam_q1[ia], da_lam_k1[ia], da_lam_q2[ia], da_lam_k2[ia])
            li = _lambda_init(layer)
            o_l = diff_attention(qt, k, vt, lam_params, da_subln[ia], li, n_ctx, seq, n, DA_Q_TILE)
            o_c = diff_attention(qt, k, vt, lam_params, da_subln[ia], li, 0, n_ctx, n_ctx, n_ctx)
            o = jnp.concatenate([o_c, o_l], axis=1)
            xj = matmul(o, da_wo[ia].astype(BF16), F32, res=xj, gate=g_m, n_ctx=n_ctx)
            ia += 1
        elif kind == 1:
            nh = ssm_A_log.shape[-1]
            di = nh * SSM_HEAD_DIM
            conv_dim = ssm_conv_w.shape[-1]
            w_in = ssm_w_in[ib].astype(BF16)
            z = matmul(h, w_in[:, :di], F32, tn=1024)
            xbc = matmul(h, w_in[:, di:di + conv_dim], F32, tn=1024)
            w_dt = jnp.zeros((d, LANES), BF16).at[:, :2 * nh].set(w_in[:, di + conv_dim:])
            dt_raw = matmul(h, w_dt, F32)
            xbc = conv_silu(xbc, ssm_conv_w[ib], ssm_conv_b[ib], n_ctx)
            expand = (jnp.arange(di)[None, :] // SSM_HEAD_DIM == jnp.arange(nh)[:, None]).astype(BF16)
            ys = [ssd_scan(xbc, dt_raw, ssm_dt_bias[ib], ssm_A_log[ib], expand, dirn, nh, n_ctx)
                  for dirn in range(2)]
            d_chan = jnp.repeat(ssm_D[ib, 0] + ssm_D[ib, 1], SSM_HEAD_DIM).reshape(1, di)
            xj = ssm_out(ys[0], ys[1], xbc, z, d_chan, ssm_norm[ib], ssm_w_out[ib].astype(BF16),
                         xj, g_m, n_ctx)
            ib += 1
        else:
            qkv = matmul(h, na_wqkv[ic].astype(BF16), BF16, tn=1024)
            bias_tab = na_bias_table(na_rpb[ic], seq // GRID_W)
            o_l = neighbourhood_attention(qkv, bias_tab, n_ctx)
            o_c = context_attention(qkv, n_ctx)
            o = jnp.concatenate([o_c, o_l], axis=1)
            xj = matmul(o, na_wo[ic].astype(BF16), F32, res=xj, gate=g_m, n_ctx=n_ctx)
            ic += 1

        hp, aff, afft = norm_mod(xj, norm_ffn[layer], sh_f, sc_f, n_ctx, router_w=router_w[layer])
        sets = [(n_ctx, seq, EC_CAPACITY_FACTOR * seq // n_exp)]
        if need_ctx:
            sets.append((0, n_ctx, EC_CAPACITY_FACTOR * n_ctx // n_exp))
        idx = jnp.swapaxes(route(aff, sets)[:, :, :n_exp], 1, 2)
        y = expert_ffn(idx, hp, afft, g_f, exp_w1, exp_w3, exp_w2, layer, sets[0][2])
        xj = moe_scatter(idx, xj, y)

    return final_norm(xj, final_norm_w, n_ctx)
```

```python
import functools
import math

import jax
import jax.numpy as jnp
import numpy as np
from jax import lax
from jax.experimental import pallas as pl
from jax.experimental.pallas import tpu as pltpu

F32 = jnp.float32
BF16 = jnp.bfloat16

GRID_W = 64
NORM_EPS = 1e-6
ROPE_BASE = 10000.0
DA_HEADS = 8
DA_HEAD_DIM = 64
DA_KV_CHUNK = 768
DA_GROUP_LANES = 256
DA_Q_TILE = 512
DA_SCORE_BUFFERS = 4
DA_SUM_ROWS = 16
SSM_HEAD_DIM = 64
SSM_GROUPS = 4
SSM_STATE = 128
SSM_CHUNK = 128
NA_HEADS = 16
NA_ROWS = 8
NA_COLS = 16
N_MIXERS = 3
EC_CAPACITY_FACTOR = 2
MASK_VALUE = -1e30
ONE_BITS = 0x3F800000
LANES = 128
VMEM_LIMIT = 56 * 1024 * 1024


def _params(*sem):
    return pltpu.CompilerParams(dimension_semantics=sem, vmem_limit_bytes=VMEM_LIMIT)


def _split3(a):
    hi = a.astype(BF16)
    r1 = a - hi.astype(F32)
    mid = r1.astype(BF16)
    lo = (r1 - mid.astype(F32)).astype(BF16)
    return hi, mid, lo


def _dot(a, b):
    return jnp.dot(a, b, preferred_element_type=F32)


def _dot_nt(a, b):
    return lax.dot_general(a, b, (((1,), (1,)), ((), ())), preferred_element_type=F32)


def _dot_tn(a, b):
    return lax.dot_general(a, b, (((0,), (0,)), ((), ())), preferred_element_type=F32)


def _ada_kernel(c_ref, w_ref, b_ref, o_ref):
    c = c_ref[...]
    cond = c * jax.nn.sigmoid(c)
    o_ref[0] = jnp.dot(cond, w_ref[0], preferred_element_type=F32,
                       precision=lax.Precision.HIGHEST) + b_ref[0]


def ada_modulation(cond_rows, ada_w, ada_b):
    depth, d, n6 = ada_w.shape
    tn = 1536
    return pl.pallas_call(
        _ada_kernel,
        grid=(depth, n6 // tn),
        in_specs=[pl.BlockSpec((8, d), lambda l, j: (0, 0)),
                  pl.BlockSpec((1, d, tn), lambda l, j: (l, 0, j)),
                  pl.BlockSpec((1, 1, tn), lambda l, j: (l, 0, j))],
        out_specs=pl.BlockSpec((1, 8, tn), lambda l, j: (l, 0, j)),
        out_shape=jax.ShapeDtypeStruct((depth, 8, n6), F32),
        compiler_params=_params("parallel", "parallel"),
        name="ada_modulation",
    )(cond_rows, ada_w, ada_b.reshape(depth, 1, n6))


def _row_select(i, tm, n_ctx, vec2):
    row = i * tm + lax.broadcasted_iota(jnp.int32, (tm, 1), 0)
    return jnp.where(row >= n_ctx, vec2[1:2, :], vec2[0:1, :])


def _norm_mod_kernel(x_ref, w_ref, sh_ref, sc_ref, *rest, tm, n_ctx, n_exp, with_delta):
    i = pl.program_id(1)
    xf = x_ref[0]
    if with_delta:
        xf = xf + rest[0][0]
        rest = rest[1:]
    y = xf * lax.rsqrt(jnp.mean(xf * xf, axis=-1, keepdims=True) + NORM_EPS) * w_ref[...]
    sc = _row_select(i, tm, n_ctx, sc_ref[0])
    sh = _row_select(i, tm, n_ctx, sh_ref[0])
    h = y * (1.0 + sc) + sh
    if not n_exp:
        h_ref = rest[0]
        h_ref[0] = h.astype(h_ref.dtype)
        if with_delta:
            rest[1][0] = xf
        return
    rwt_ref, hp_ref, aff_ref, afft_ref = rest
    bits = lax.bitcast_convert_type(h.astype(BF16).astype(F32), jnp.uint32)
    dh = bits.shape[1] // 2
    hp_ref[0] = (bits[:, :dh] >> 16) | bits[:, dh:]
    logits = lax.dot_general(rwt_ref[...], h, (((1,), (1,)), ((), ())),
                             preferred_element_type=F32, precision=lax.Precision.HIGHEST)
    e = jnp.exp(logits - jnp.max(logits, axis=0, keepdims=True))
    aff = e / jnp.sum(e, axis=0, keepdims=True)
    aff_ref[0] = aff
    afft_ref[0] = jnp.concatenate([aff, jnp.zeros((LANES - n_exp, aff.shape[1]), F32)], axis=0).T


def norm_mod(x, w, shift, scale, n_ctx, router_w=None, delta=None, tm=256):
    b, n, d = x.shape
    n_exp = 0 if router_w is None else router_w.shape[1]
    with_delta = delta is not None
    assert not (n_exp and with_delta)
    row = pl.BlockSpec((1, tm, d), lambda bi, i: (bi, i, 0))
    in_specs = [row,
                pl.BlockSpec((1, d), lambda bi, i: (0, 0)),
                pl.BlockSpec((1, 2, d), lambda bi, i: (bi, 0, 0)),
                pl.BlockSpec((1, 2, d), lambda bi, i: (bi, 0, 0))]
    args = [x, w.reshape(1, d), shift, scale]
    if not n_exp:
        out_specs = [row]
        out_shape = [jax.ShapeDtypeStruct((b, n, d), BF16)]
        if with_delta:
            in_specs.append(row)
            args.append(delta)
            out_specs.append(row)
            out_shape.append(jax.ShapeDtypeStruct((b, n, d), F32))
    else:
        in_specs += [pl.BlockSpec((n_exp, d), lambda bi, i: (0, 0))]
        args += [router_w.T]
        out_specs = [pl.BlockSpec((1, tm, d // 2), lambda bi, i: (bi, i, 0)),
                     pl.BlockSpec((1, n_exp, tm), lambda bi, i: (bi, 0, i)),
                     pl.BlockSpec((1, tm, LANES), lambda bi, i: (bi, i, 0))]
        out_shape = [jax.ShapeDtypeStruct((b, n, d // 2), jnp.uint32),
                     jax.ShapeDtypeStruct((b, n_exp, n), F32),
                     jax.ShapeDtypeStruct((b, n, LANES), F32)]
    outs = pl.pallas_call(
        functools.partial(_norm_mod_kernel, tm=tm, n_ctx=n_ctx, n_exp=n_exp, with_delta=with_delta),
        grid=(b, n // tm),
        in_specs=in_specs, out_specs=out_specs, out_shape=out_shape,
        compiler_params=_params("parallel", "parallel"),
        name="norm_mod_router" if n_exp else "norm_mod",
    )(*args)
    return outs if (n_exp or with_delta) else outs[0]


def _final_norm_kernel(x_ref, d_ref, w_ref, o_ref):
    xf = x_ref[0] + d_ref[0]
    o_ref[0] = xf * lax.rsqrt(jnp.mean(xf * xf, axis=-1, keepdims=True) + NORM_EPS) * w_ref[...]


def final_norm(x, delta, w, n_ctx, tm=256):
    b, n, d = x.shape
    off = n_ctx // tm
    row = pl.BlockSpec((1, tm, d), lambda bi, i: (bi, i + off, 0))
    return pl.pallas_call(
        _final_norm_kernel,
        grid=(b, (n - n_ctx) // tm),
        in_specs=[row, row, pl.BlockSpec((1, d), lambda bi, i: (0, 0))],
        out_specs=pl.BlockSpec((1, tm, d), lambda bi, i: (bi, i, 0)),
        out_shape=jax.ShapeDtypeStruct((b, n - n_ctx, d), F32),
        compiler_params=_params("parallel", "parallel"),
        name="final_norm",
    )(x, delta, w.reshape(1, d))


def _matmul_kernel(a_ref, w_ref, *rest, tm, n_ctx, residual):
    if residual:
        res_ref, g_ref, o_ref = rest
    else:
        (o_ref,) = rest
    acc = _dot(a_ref[0], w_ref[...])
    if residual:
        g = _row_select(pl.program_id(1), tm, n_ctx, g_ref[0])
        acc = res_ref[0] + g * acc
    o_ref[0] = acc.astype(o_ref.dtype)


def matmul(a, w, out_dtype, res=None, gate=None, n_ctx=0, tm=768, tn=None):
    b, n, k = a.shape
    m = w.shape[1]
    tn = m if tn is None else tn
    residual = res is not None
    in_specs = [pl.BlockSpec((1, tm, k), lambda bi, i, j: (bi, i, 0)),
                pl.BlockSpec((k, tn), lambda bi, i, j: (0, j))]
    args = [a, w]
    if residual:
        in_specs += [pl.BlockSpec((1, tm, tn), lambda bi, i, j: (bi, i, j)),
                     pl.BlockSpec((1, 2, tn), lambda bi, i, j: (bi, 0, j))]
        args += [res, gate]
    return pl.pallas_call(
        functools.partial(_matmul_kernel, tm=tm, n_ctx=n_ctx, residual=residual),
        grid=(b, n // tm, m // tn),
        in_specs=in_specs,
        out_specs=pl.BlockSpec((1, tm, tn), lambda bi, i, j: (bi, i, j)),
        out_shape=jax.ShapeDtypeStruct((b, n, m), out_dtype),
        compiler_params=_params("parallel", "parallel", "arbitrary"),
        name="matmul_res" if residual else "matmul",
    )(*args)


def _rotate_pairs(x, axis):
    seg = lax.broadcasted_iota(jnp.int32, x.shape, axis) // 16
    n = x.shape[axis]
    return jnp.where(seg % 2 == 0, pltpu.roll(x, n - 16, axis), pltpu.roll(x, 16, axis))


def _da_qkv_kernel(h_ref, wqt_ref, wk_ref, wvt_ref, cq_ref, sq_ref, ck_ref, sk_ref,
                   qt_ref, k_ref, vt_ref, *, heads):
    h = h_ref[0]
    qt = _dot_nt(wqt_ref[...], h)
    cq, sq = cq_ref[...], sq_ref[...]
    for hd in range(heads):
        rs = slice(hd * LANES, (hd + 1) * LANES)
        x = qt[rs]
        qt_ref[0, rs, :] = (x * cq + _rotate_pairs(x, 0) * sq).astype(qt_ref.dtype)
    k = _dot(h, wk_ref[...])
    ck, sk = ck_ref[...], sk_ref[...]
    for hd in range(heads):
        cs = slice(hd * LANES, (hd + 1) * LANES)
        x = k[:, cs]
        k_ref[0, :, cs] = (x * ck + _rotate_pairs(x, 1) * sk).astype(k_ref.dtype)
    vt = _dot_nt(wvt_ref[...], h).astype(vt_ref.dtype)
    ones = jnp.ones((DA_SUM_ROWS, vt.shape[1]), vt_ref.dtype)
    for hd in range(heads):
        vt_ref[0, 0, hd, :LANES, :] = vt[hd * LANES:(hd + 1) * LANES]
        vt_ref[0, 0, hd, LANES:, :] = ones


def da_qkv(h, wqkv, tabs, tm):
    b, n, d = h.shape
    m = wqkv.shape[1] // 3
    cq_t, sq_t, ck, sk = tabs
    wqt = wqkv[:, :m].T.astype(BF16)
    wk = wqkv[:, m:2 * m].astype(BF16)
    wvt = wqkv[:, 2 * m:].T.astype(BF16)
    full = lambda r, c: pl.BlockSpec((r, c), lambda bi, i: (0, 0))
    return pl.pallas_call(
        functools.partial(_da_qkv_kernel, heads=m // LANES),
        grid=(b, n // tm),
        in_specs=[pl.BlockSpec((1, tm, d), lambda bi, i: (bi, i, 0)),
                  full(m, d), full(d, m), full(m, d),
                  pl.BlockSpec((LANES, tm), lambda bi, i: (0, i)),
                  pl.BlockSpec((LANES, tm), lambda bi, i: (0, i)),
                  pl.BlockSpec((tm, LANES), lambda bi, i: (i, 0)),
                  pl.BlockSpec((tm, LANES), lambda bi, i: (i, 0))],
        out_specs=[pl.BlockSpec((1, m, tm), lambda bi, i: (bi, 0, i)),
                   pl.BlockSpec((1, tm, m), lambda bi, i: (bi, i, 0)),
                   pl.BlockSpec((1, 1, m // LANES, LANES + DA_SUM_ROWS, tm), lambda bi, i: (bi, i, 0, 0, 0))],
        out_shape=[jax.ShapeDtypeStruct((b, m, n), BF16),
                   jax.ShapeDtypeStruct((b, n, m), BF16),
                   jax.ShapeDtypeStruct((b, n // tm, m // LANES, LANES + DA_SUM_ROWS, tm), BF16)],
        compiler_params=_params("parallel", "parallel"),
        name="da_qkv",
    )(h, wqt, wk, wvt, cq_t, sq_t, ck, sk)


def rope_tables(n_ctx, seq, q_scale):
    d = DA_HEAD_DIM
    t = jnp.arange(seq)
    rows = (t // GRID_W).astype(F32)
    cols = (t % GRID_W).astype(F32)
    quarter = d // 4
    freqs = ROPE_BASE ** (-jnp.arange(quarter, dtype=F32) / quarter)
    ang_r = rows[:, None] * freqs[None, :]
    ang_c = cols[:, None] * freqs[None, :]
    ang = jnp.concatenate([ang_r, ang_r, ang_c, ang_c], axis=-1)
    cos = jnp.cos(ang)
    sign = jnp.concatenate([-jnp.ones((quarter,), F32), jnp.ones((quarter,), F32)] * 2)
    sin = jnp.sin(ang) * sign[None, :]
    cos = jnp.concatenate([jnp.ones((n_ctx, d), F32), cos], axis=0)
    sin = jnp.concatenate([jnp.zeros((n_ctx, d), F32), sin], axis=0)
    cos = jnp.concatenate([cos, cos], axis=-1)
    sin = jnp.concatenate([sin, sin], axis=-1)
    return (cos * q_scale).T, (sin * q_scale).T, cos, sin


def _diff_attn_kernel(*refs, nq, tk, kw, nchunks, lam_init):
    qt_refs = refs[:nq]
    k_ref, vt_ref, lq1_ref, lk1_ref, lq2_ref, lk2_ref, sub_ref, o_ref, acc_ref, s_ref = refs[nq:]
    gw = DA_GROUP_LANES
    ngroups = 2 * nq
    sub = lax.broadcasted_iota(jnp.int32, (LANES, gw), 0)
    zero = jnp.zeros((LANES, gw), qt_refs[0].dtype)
    qg = []
    for g in range(ngroups):
        q = qt_refs[g // 2][0]
        qg.append(jnp.where((sub < DA_HEAD_DIM) if g % 2 == 0 else (sub >= DA_HEAD_DIM), q, zero))
    acc_ref[...] = jnp.zeros(acc_ref.shape, F32)
    half = kw // 2

    def col_reduce(x, op):
        return op(op(x.reshape(8, x.shape[0] // 8, x.shape[1]), axis=0), axis=0, keepdims=True)

    def scores(t):
        c, g = divmod(t, ngroups)
        mx = None
        for hf in range(2):
            k = k_ref[0, c * tk + hf * half:c * tk + (hf + 1) * half, :]
            s = _dot(k, qg[g])
            s_ref[t % DA_SCORE_BUFFERS, hf * half:(hf + 1) * half, :] = s
            hm = col_reduce(s, jnp.max)
            mx = hm if mx is None else jnp.maximum(mx, hm)
        return mx

    def softmax_pv(t, mx, ml):
        c, g = divmod(t, ngroups)
        m_old, l_old = ml
        vt = vt_ref[0, c, 0]
        m_new = jnp.maximum(m_old, mx)
        alpha = jnp.exp2(m_old - m_new)
        pb = jnp.exp2(s_ref[t % DA_SCORE_BUFFERS] - m_new).astype(BF16)
        pv = _dot(vt[:, :kw], pb)
        acc_ref[g] = alpha * acc_ref[g] + pv[:LANES]
        return m_new, alpha * l_old + pv[LANES:LANES + 1]

    nsteps = nchunks * ngroups
    ahead = DA_SCORE_BUFFERS - 1
    carry = [(jnp.full((1, gw), -jnp.inf, F32), jnp.zeros((1, gw), F32))] * ngroups
    maxima = {t: scores(t) for t in range(min(ahead, nsteps))}
    for t in range(nsteps):
        if t + ahead < nsteps:
            maxima[t + ahead] = scores(t + ahead)
        carry[t % ngroups] = softmax_pv(t, maxima.pop(t), carry[t % ngroups])
    lam = (jnp.exp(jnp.sum(lq1_ref[...] * lk1_ref[...], axis=-1, keepdims=True))
           - jnp.exp(jnp.sum(lq2_ref[...] * lk2_ref[...], axis=-1, keepdims=True)) + lam_init)
    for qh in range(ngroups // 2):
        od = acc_ref[2 * qh] / carry[2 * qh][1] - lam * (acc_ref[2 * qh + 1] / carry[2 * qh + 1][1])
        y = od * lax.rsqrt(jnp.mean(od * od, axis=0, keepdims=True) + NORM_EPS) * sub_ref[...]
        o_ref[0, qh * gw:(qh + 1) * gw, :] = (y * (1.0 - lam_init)).T.astype(o_ref.dtype)


def diff_attention(qt, k, vt, lam_params, subln, lam_init, q_start, n_q, n_k, tq):
    b, m, n = qt.shape
    h = m // LANES
    tk = vt.shape[-1]
    kw = min(tk, n_k)
    nchunks = n_k // kw
    gw = DA_GROUP_LANES
    nq = tq // gw
    assert q_start % gw == 0 and tq % gw == 0 and n_q % tq == 0
    qoff = q_start // gw
    vec = lambda a: a.reshape(1, -1).astype(F32)
    small = pl.BlockSpec((1, DA_HEAD_DIM), lambda bi, hi, i: (0, 0))
    q_specs = [pl.BlockSpec((1, LANES, gw), functools.partial(
        lambda bi, hi, i, j: (bi, hi, qoff + nq * i + j), j=j)) for j in range(nq)]
    return pl.pallas_call(
        functools.partial(_diff_attn_kernel, nq=nq, tk=tk, kw=kw, nchunks=nchunks, lam_init=lam_init),
        grid=(b, h, n_q // tq),
        in_specs=q_specs + [
                  pl.BlockSpec((1, n_k, LANES), lambda bi, hi, i: (bi, 0, hi)),
                  pl.BlockSpec((1, nchunks, 1, vt.shape[3], tk), lambda bi, hi, i: (bi, 0, hi, 0, 0)),
                  small, small, small, small,
                  pl.BlockSpec((LANES, 1), lambda bi, hi, i: (0, 0))],
        out_specs=pl.BlockSpec((1, tq, LANES), lambda bi, hi, i: (bi, i, hi)),
        out_shape=jax.ShapeDtypeStruct((b, n_q, m), BF16),
        scratch_shapes=[pltpu.VMEM((2 * tq // DA_GROUP_LANES, LANES, DA_GROUP_LANES), F32),
                        pltpu.VMEM((DA_SCORE_BUFFERS, kw, DA_GROUP_LANES), F32)],
        compiler_params=_params("parallel", "parallel", "arbitrary"),
        name="diff_attention",
    )(*([qt] * nq), k, vt, *[vec(p) for p in lam_params], subln.reshape(-1, 1).astype(F32))


def _pair_scores(qp, kp):
    lane = lax.broadcasted_iota(jnp.int32, qp.shape, 1)
    zero = jnp.zeros_like(qp)
    qs = jnp.concatenate([jnp.where(lane < 64, qp, zero), jnp.where(lane >= 64, qp, zero)], axis=0)
    return _dot_nt(qs, kp)


def _pair_softmax_pv(s, vp, bias, n_bias):
    tq = s.shape[0] // 2
    if n_bias:
        s_w = s[:, :n_bias] + bias
        s_c = s[:, n_bias:]
        m = jnp.maximum(jnp.max(s_w, axis=-1, keepdims=True), jnp.max(s_c, axis=-1, keepdims=True))
        p_w = jnp.exp(s_w - m)
        p_c = jnp.exp(s_c - m)
        l = jnp.sum(p_w, axis=-1, keepdims=True) + jnp.sum(p_c, axis=-1, keepdims=True)
        pv = _dot(p_w.astype(BF16), vp[:n_bias]) + _dot(p_c.astype(BF16), vp[n_bias:])
    else:
        m = jnp.max(s, axis=-1, keepdims=True)
        p = jnp.exp(s - m)
        l = jnp.sum(p, axis=-1, keepdims=True)
        pv = _dot(p.astype(BF16), vp)
    o = pv / l
    lane_o = lax.broadcasted_iota(jnp.int32, (tq, LANES), 1)
    return jnp.where(lane_o < 64, o[:tq], o[tq:])


def _pairs_attention(q_of, k_of, v_of, bias_of, n_bias, o_ref):
    npairs = NA_HEADS // 2
    s_next = _pair_scores(q_of(0), k_of(0))
    for hp in range(npairs):
        s_cur = s_next
        if hp + 1 < npairs:
            s_next = _pair_scores(q_of(hp + 1), k_of(hp + 1))
        o = _pair_softmax_pv(s_cur, v_of(hp), bias_of(hp), n_bias)
        o_ref[0, :, hp * LANES:(hp + 1) * LANES] = o.astype(o_ref.dtype)


def _na_kernel(*refs, wr, n_ctx, scale):
    q_ref = refs[0]
    k_refs = refs[1:1 + wr]
    v_refs = refs[1 + wr:1 + 2 * wr]
    kc_ref, vc_ref, bias_ref, o_ref, kbuf, vbuf = refs[1 + 2 * wr:]
    nw = wr * GRID_W
    for w in range(wr):
        kbuf[w * GRID_W:(w + 1) * GRID_W, :] = k_refs[w][0]
        vbuf[w * GRID_W:(w + 1) * GRID_W, :] = v_refs[w][0]
    kbuf[nw:nw + n_ctx, :] = kc_ref[0]
    vbuf[nw:nw + n_ctx, :] = vc_ref[0]
    cols = lambda hp: slice(hp * LANES, (hp + 1) * LANES)
    _pairs_attention(lambda hp: q_ref[0, :, cols(hp)] * scale,
                     lambda hp: kbuf[:, cols(hp)],
                     lambda hp: vbuf[:, cols(hp)],
                     lambda hp: bias_ref[0, 2 * hp:2 * hp + 2].reshape(2 * GRID_W, nw),
                     nw, o_ref)


def na_bias_table(rpb, rows):
    wr = min(NA_ROWS, rows)
    qcol = np.arange(GRID_W)
    col_start = np.clip(qcol - NA_COLS // 2, 0, GRID_W - NA_COLS)
    kc = np.arange(GRID_W)
    inside = (kc[None, :] >= col_start[:, None]) & (kc[None, :] < col_start[:, None] + NA_COLS)
    dc = np.clip(kc[None, :] - qcol[:, None] + NA_COLS - 1, 0, 2 * NA_COLS - 2)
    dr = np.arange(wr)[None, :] - np.arange(wr)[:, None] + NA_ROWS - 1
    sel_r = (dr[:, :, None] == np.arange(2 * NA_ROWS - 1)).astype(np.float32)
    sel_c = (dc[:, :, None] == np.arange(2 * NA_COLS - 1)).astype(np.float32)
    tab = jnp.einsum("cwa,hab,qkb->chqwk", sel_r, rpb.astype(F32), sel_c,
                     precision=lax.Precision.HIGHEST)
    tab = jnp.where(inside[None, None, :, None, :], tab, MASK_VALUE)
    return tab.reshape(wr, rpb.shape[0], GRID_W, wr * GRID_W)


def neighbourhood_attention(qkv, bias_tab, n_ctx):
    b, n, m3 = qkv.shape
    m = m3 // 3
    rows = (n - n_ctx) // GRID_W
    wr = min(NA_ROWS, rows)
    cb = n_ctx // GRID_W
    rs = lambda r: jnp.clip(r - wr // 2, 0, rows - wr)
    blk = lambda f: pl.BlockSpec((1, GRID_W, m), f)
    in_specs = [blk(lambda bi, r: (bi, cb + r, 0))]
    in_specs += [blk(functools.partial(lambda bi, r, w: (bi, cb + rs(r) + w, 1), w=w)) for w in range(wr)]
    in_specs += [blk(functools.partial(lambda bi, r, w: (bi, cb + rs(r) + w, 2), w=w)) for w in range(wr)]
    in_specs += [pl.BlockSpec((1, n_ctx, m), lambda bi, r: (bi, 0, 1)),
                 pl.BlockSpec((1, n_ctx, m), lambda bi, r: (bi, 0, 2)),
                 pl.BlockSpec((1, NA_HEADS, GRID_W, wr * GRID_W), lambda bi, r: (r - rs(r), 0, 0, 0))]
    nk = wr * GRID_W + n_ctx
    return pl.pallas_call(
        functools.partial(_na_kernel, wr=wr, n_ctx=n_ctx, scale=(m // NA_HEADS) ** -0.5),
        grid=(b, rows),
        in_specs=in_specs,
        out_specs=pl.BlockSpec((1, GRID_W, m), lambda bi, r: (bi, r, 0)),
        out_shape=jax.ShapeDtypeStruct((b, n - n_ctx, m), BF16),
        scratch_shapes=[pltpu.VMEM((nk, m), BF16), pltpu.VMEM((nk, m), BF16)],
        compiler_params=_params("parallel", "arbitrary"),
        name="neighbourhood_attention",
    )(*([qkv] * (1 + 2 * wr + 2)), bias_tab)


def _ctx_attn_kernel(q_ref, k_ref, v_ref, o_ref, *, scale):
    cols = lambda hp: slice(hp * LANES, (hp + 1) * LANES)
    _pairs_attention(lambda hp: q_ref[0, :, cols(hp)] * scale,
                     lambda hp: k_ref[0, :, cols(hp)],
                     lambda hp: v_ref[0, :, cols(hp)],
                     lambda hp: None, 0, o_ref)


def context_attention(qkv, n_ctx):
    b, n, m3 = qkv.shape
    m = m3 // 3
    return pl.pallas_call(
        functools.partial(_ctx_attn_kernel, scale=(m // NA_HEADS) ** -0.5),
        grid=(b,),
        in_specs=[pl.BlockSpec((1, n_ctx, m), lambda bi: (bi, 0, 0)),
                  pl.BlockSpec((1, n_ctx, m), lambda bi: (bi, 0, 1)),
                  pl.BlockSpec((1, n_ctx, m), lambda bi: (bi, 0, 2))],
        out_specs=pl.BlockSpec((1, n_ctx, m), lambda bi: (bi, 0, 0)),
        out_shape=jax.ShapeDtypeStruct((b, n_ctx, m), BF16),
        compiler_params=_params("parallel"),
        name="context_attention",
    )(qkv, qkv, qkv)


def _conv_silu_kernel(x_ref, w_ref, b_ref, o_ref, *, n, n_ctx):
    x = x_ref[0]
    row = lax.broadcasted_iota(jnp.int32, x.shape, 0)
    xm = jnp.where((row == 0) | (row == n_ctx), 0.0, pltpu.roll(x, 1, 0))
    xp = jnp.where((row == n_ctx - 1) | (row == n - 1), 0.0, pltpu.roll(x, n - 1, 0))
    y = w_ref[0:1, :] * xm + w_ref[1:2, :] * x + w_ref[2:3, :] * xp + b_ref[...]
    o_ref[0] = y * jax.nn.sigmoid(y)


def conv_silu(x, w, bias, n_ctx):
    b, n, c = x.shape
    return pl.pallas_call(
        functools.partial(_conv_silu_kernel, n=n, n_ctx=n_ctx),
        grid=(b, c // LANES),
        in_specs=[pl.BlockSpec((1, n, LANES), lambda bi, j: (bi, 0, j)),
                  pl.BlockSpec((3, LANES), lambda bi, j: (0, j)),
                  pl.BlockSpec((1, LANES), lambda bi, j: (0, j))],
        out_specs=pl.BlockSpec((1, n, LANES), lambda bi, j: (bi, 0, j)),
        out_shape=jax.ShapeDtypeStruct((b, n, c), F32),
        compiler_params=_params("parallel", "parallel"),
        name="conv_silu",
    )(x, w, bias.reshape(1, c))


def _softplus(x):
    return jnp.maximum(x, 0.0) + jnp.log1p(jnp.exp(-jnp.abs(x)))


def _ssd_kernel(x_ref, b_ref, c_ref, dt_ref, dtb_ref, alog_ref, e_ref, y_ref, state_ref,
                *, direction, nheads):
    L = SSM_CHUNK
    P = SSM_HEAD_DIM
    G = SSM_GROUPS
    hpg = nheads // G
    reverse = direction == 1

    @pl.when(pl.program_id(1) == 0)
    def _():
        state_ref[...] = jnp.zeros(state_ref.shape, F32)

    li = lax.broadcasted_iota(jnp.int32, (L, L), 0)
    si = lax.broadcasted_iota(jnp.int32, (L, L), 1)
    allowed = (si >= li) if reverse else (si <= li)
    tri = jnp.where(allowed, 1.0, 0.0).astype(BF16)
    tri_t = jnp.where((li >= si) if reverse else (li <= si), 1.0, 0.0).astype(BF16)

    dt_raw = dt_ref[0][:, direction * nheads:(direction + 1) * nheads]
    dt = _softplus(dt_raw + dtb_ref[...])
    a = dt * (-jnp.exp(alog_ref[...]))
    a3 = _split3(a)
    acs = sum(_dot(tri, p) for p in a3)
    acs_t = sum(_dot_tn(p, tri_t) for p in a3)
    expand = e_ref[...]
    acs_x = sum(_dot(p, expand) for p in _split3(acs))
    dt_x = sum(_dot(p, expand) for p in _split3(dt))
    end = 0 if reverse else L - 1
    total_x = acs_x[end:end + 1, :]

    xd = x_ref[0] * dt_x
    xd_b = xd.astype(BF16)
    x_state = (xd * jnp.exp(total_x - acs_x)).astype(BF16)
    exp_acs_x = jnp.exp(acs_x)
    chunk_decay_x = jnp.exp(total_x)
    lane = lax.broadcasted_iota(jnp.int32, (L, LANES), 1)

    for g in range(G):
        gs = slice(g * SSM_STATE, (g + 1) * SSM_STATE)
        hs = slice(g * hpg * P, (g + 1) * hpg * P)
        cg = c_ref[0][:, gs].astype(BF16)
        bg = b_ref[0][:, gs].astype(BF16)
        cb = _dot_nt(cg, bg)
        h_prev = state_ref[:, hs]
        y_off = _dot(cg, h_prev.astype(BF16)) * exp_acs_x[:, hs]
        state_ref[:, hs] = h_prev * chunk_decay_x[:, hs] + _dot_tn(bg, x_state[:, hs])
        for rp in range(hpg // 2):
            mats = []
            for r in (2 * rp, 2 * rp + 1):
                hidx = g * hpg + r
                seg = acs[:, hidx:hidx + 1] - acs_t[hidx:hidx + 1, :]
                mats.append((cb * jnp.exp(jnp.where(allowed, seg, -jnp.inf))).astype(BF16))
            ps = slice((g * hpg + 2 * rp) * P, (g * hpg + 2 * rp + 2) * P)
            yd = _dot(jnp.concatenate(mats, axis=0), xd_b[:, ps])
            y_ref[0, :, ps] = jnp.where(lane < P, yd[:L], yd[L:]) + y_off[:, 2 * rp * P:(2 * rp + 2) * P]


def ssd_scan(xbc, dt_raw, dt_bias, a_log, expand, direction, nheads, n_ctx):
    b, n, _ = xbc.shape
    d_inner = nheads * SSM_HEAD_DIM
    gn = SSM_GROUPS * SSM_STATE
    L = SSM_CHUNK
    nc = n // L
    ncc = n_ctx // L
    if direction == 0:
        cidx = lambda c: c
    else:
        cidx = lambda c: jnp.where(c < ncc, ncc - 1 - c, nc - 1 - (c - ncc))
    return pl.pallas_call(
        functools.partial(_ssd_kernel, direction=direction, nheads=nheads),
        grid=(b, nc),
        in_specs=[pl.BlockSpec((1, L, d_inner), lambda bi, c: (bi, cidx(c), 0)),
                  pl.BlockSpec((1, L, gn), lambda bi, c: (bi, cidx(c), d_inner // gn)),
                  pl.BlockSpec((1, L, gn), lambda bi, c: (bi, cidx(c), d_inner // gn + 1)),
                  pl.BlockSpec((1, L, LANES), lambda bi, c: (bi, cidx(c), 0)),
                  pl.BlockSpec((1, nheads), lambda bi, c: (0, 0)),
                  pl.BlockSpec((1, nheads), lambda bi, c: (0, 0)),
                  pl.BlockSpec((nheads, d_inner), lambda bi, c: (0, 0))],
        out_specs=pl.BlockSpec((1, L, d_inner), lambda bi, c: (bi, cidx(c), 0)),
        out_shape=jax.ShapeDtypeStruct((b, n, d_inner), F32),
        scratch_shapes=[pltpu.VMEM((SSM_STATE, d_inner), F32)],
        compiler_params=_params("parallel", "arbitrary"),
        name="ssd_scan_%d" % direction,
    )(xbc, xbc, xbc, dt_raw, dt_bias[direction:direction + 1], a_log[direction:direction + 1], expand)


def _ssm_out_kernel(y0_ref, y1_ref, x_ref, z_ref, d_ref, nw_ref, w_ref, res_ref, g_ref, o_ref,
                    *, tm, n_ctx):
    z = z_ref[0]
    u = (y0_ref[0] + y1_ref[0] + d_ref[...] * x_ref[0]) * (z * jax.nn.sigmoid(z))
    un = u * lax.rsqrt(jnp.mean(u * u, axis=-1, keepdims=True) + NORM_EPS) * nw_ref[...]
    g = _row_select(pl.program_id(1), tm, n_ctx, g_ref[0])
    o_ref[0] = res_ref[0] + g * _dot(un.astype(BF16), w_ref[...])


def ssm_out(y0, y1, xbc, z, d_chan, norm_w, w_out, res, gate, n_ctx, tm=256):
    b, n, di = y0.shape
    d = w_out.shape[1]
    row = lambda c: pl.BlockSpec((1, tm, c), lambda bi, i: (bi, i, 0))
    vec = lambda c: pl.BlockSpec((1, c), lambda bi, i: (0, 0))
    return pl.pallas_call(
        functools.partial(_ssm_out_kernel, tm=tm, n_ctx=n_ctx),
        grid=(b, n // tm),
        in_specs=[row(di), row(di), row(di), row(di), vec(di), vec(di),
                  pl.BlockSpec((di, d), lambda bi, i: (0, 0)), row(d),
                  pl.BlockSpec((1, 2, d), lambda bi, i: (bi, 0, 0))],
        out_specs=row(d),
        out_shape=jax.ShapeDtypeStruct((b, n, d), F32),
        compiler_params=_params("parallel", "parallel"),
        name="ssm_out",
    )(y0, y1, xbc, z, d_chan, norm_w.reshape(1, di), w_out, res, gate)


def _cumsum_lanes(x):
    n = x.shape[-1]
    lane = lax.broadcasted_iota(jnp.int32, x.shape, x.ndim - 1)
    s = 1
    while s < n:
        x = x + jnp.where(lane >= s, pltpu.roll(x, s, x.ndim - 1), 0.0)
        s *= 2
    return x


def _route_set(aff, cap):
    e, t = aff.shape
    bits = lax.bitcast_convert_type(aff, jnp.int32)

    def count_ge(thr):
        return jnp.sum(jnp.where(bits >= thr, 1.0, 0.0), axis=1, keepdims=True)

    def step(_, lohi):
        lo, hi = lohi
        mid = lo + (hi - lo) // 2
        ok = count_ge(mid) >= cap
        return jnp.where(ok, mid, lo), jnp.where(ok, hi, mid)

    lo0 = jnp.zeros((e, 1), jnp.int32)
    hi0 = jnp.full((e, 1), ONE_BITS + 1, jnp.int32)
    thr, _ = lax.fori_loop(0, 31, step, (lo0, hi0))
    gt = jnp.where(bits > thr, 1.0, 0.0)
    eq = jnp.where(bits == thr, 1.0, 0.0)
    need = cap - jnp.sum(gt, axis=1, keepdims=True)
    sel = gt + eq * jnp.where(_cumsum_lanes(eq) <= need, 1.0, 0.0)
    return sel, _cumsum_lanes(sel)


def _route_kernel(aff_ref, idx_ref, cnt_ref, *, sets, jblk):
    n_exp = aff_ref.shape[1]
    for t0, t, cap, row0 in sets:
        _, cnt = _route_set(aff_ref[0, :, t0:t0 + t], cap)
        cnt_ref[:, :t] = cnt
        for jb in range(0, cap, jblk):
            nj = min(jblk, cap - jb)
            jcol = (lax.broadcasted_iota(jnp.int32, (nj, 1), 0) + jb).astype(F32)
            lane = lax.broadcasted_iota(jnp.int32, (nj, LANES), 1)

            def per_expert(ei, acc):
                row = cnt_ref[pl.ds(ei, 1), :t]
                pos = jnp.sum(jnp.where(row <= jcol, 1.0, 0.0), axis=1, keepdims=True)
                return jnp.where(lane == ei, pos, acc)

            acc = lax.fori_loop(0, n_exp, per_expert, jnp.zeros((nj, LANES), F32))
            idx_ref[0, row0 + jb:row0 + jb + nj, :] = acc.astype(jnp.int32) + t0


def route(aff, sets):
    b, e, n = aff.shape
    m = sum(s[2] for s in sets)
    tmax = max(s[1] for s in sets)
    rows, sets_r = 0, []
    for t0, t, cap in sets:
        sets_r.append((t0, t, cap, rows))
        rows += cap
    return pl.pallas_call(
        functools.partial(_route_kernel, sets=tuple(sets_r), jblk=256),
        grid=(b,),
        in_specs=[pl.BlockSpec((1, e, n), lambda bi: (bi, 0, 0))],
        out_specs=pl.BlockSpec((1, m, LANES), lambda bi: (bi, 0, 0)),
        out_shape=jax.ShapeDtypeStruct((b, m, LANES), jnp.int32),
        scratch_shapes=[pltpu.VMEM((e, tmax), F32)],
        compiler_params=_params("parallel"),
        name="route",
    )(aff)


def _expert_ffn_kernel(idx_ref, idxn_ref, hp_ref, afft_ref, g_ref, w1_ref, w3_ref, w2_ref, o_ref,
                       xw_ref, gw_ref, xs_ref, gate_ref, *, m, n_lat, rows_per_step):
    ei = pl.program_id(1)
    f = pl.program_id(2)
    slot = ei % 2

    def copy_row(src_idx_ref, dst_slot, j):
        i = src_idx_ref[0, 0, 0, j]
        xw_ref[dst_slot, pl.ds(j, 1), :] = hp_ref[0, pl.ds(i, 1), :]
        gw_ref[dst_slot, pl.ds(j, 1), :] = afft_ref[0, pl.ds(i, 1), :]

    @pl.when((f == 0) & (ei == 0))
    def _():
        def gather(g, c):
            base = pl.multiple_of(g * 8, 8)
            for r in range(8):
                copy_row(idx_ref, 0, base + r)
            return c

        lax.fori_loop(0, m // 8, gather, 0)

    @pl.when(f == 0)
    def _():
        w = xw_ref[slot]
        lo = lax.bitcast_convert_type(w << 16, F32).astype(BF16)
        hi = lax.bitcast_convert_type(w & jnp.uint32(0xFFFF0000), F32).astype(BF16)
        xs_ref[...] = jnp.concatenate([lo, hi], axis=1)
        lane = lax.broadcasted_iota(jnp.int32, gate_ref.shape, 1)
        gate = jnp.sum(jnp.where(lane == ei, gw_ref[slot], 0.0), axis=1, keepdims=True)
        gate_ref[...] = jnp.broadcast_to(gate, gate_ref.shape)
        o_ref[...] = jnp.zeros(o_ref.shape, F32)

    for r in range(rows_per_step):
        copy_row(idxn_ref, 1 - slot, f * rows_per_step + r)

    x = xs_ref[...]
    h1 = _dot(x, w1_ref[0].astype(BF16))
    h3 = _dot(x, w3_ref[0].astype(BF16))
    hid = (h1 * jax.nn.sigmoid(h1) * h3).astype(BF16)
    part = _dot(hid, w2_ref[0].astype(BF16))

    o_ref[0, 0] = o_ref[0, 0] + part

    @pl.when(f == pl.num_programs(2) - 1)
    def _():
        row = lax.broadcasted_iota(jnp.int32, (m, 1), 0)
        g = jnp.where(row < n_lat, g_ref[0, 1:2, :], g_ref[0, 0:1, :])
        o_ref[0, 0] = o_ref[0, 0] * gate_ref[:, 0:1] * g


def expert_ffn(idx, hp, afft, gate2, w1, w3, w2, layer, n_lat, tf=256):
    b, e, m = idx.shape
    n, dh = hp.shape[1:]
    d = 2 * dh
    ff = w1.shape[3]
    single = dict(pipeline_mode=pl.Buffered(1))
    nf = ff // tf
    assert m % nf == 0
    idx4 = idx.reshape(b, e, 1, m)
    return pl.pallas_call(
        functools.partial(_expert_ffn_kernel, m=m, n_lat=n_lat, rows_per_step=m // nf),
        grid=(b, e, nf),
        in_specs=[pl.BlockSpec((1, 1, 1, m), lambda bi, ei, f: (bi, ei, 0, 0), memory_space=pltpu.SMEM),
                  pl.BlockSpec((1, 1, 1, m), lambda bi, ei, f: (bi, jnp.minimum(ei + 1, e - 1), 0, 0),
                               memory_space=pltpu.SMEM),
                  pl.BlockSpec((1, n, dh), lambda bi, ei, f: (bi, 0, 0), **single),
                  pl.BlockSpec((1, n, LANES), lambda bi, ei, f: (bi, 0, 0), **single),
                  pl.BlockSpec((1, 2, d), lambda bi, ei, f: (bi, 0, 0)),
                  pl.BlockSpec((None, 1, d, tf), lambda bi, ei, f: (layer, ei, 0, f)),
                  pl.BlockSpec((None, 1, d, tf), lambda bi, ei, f: (layer, ei, 0, f)),
                  pl.BlockSpec((None, 1, tf, d), lambda bi, ei, f: (layer, ei, f, 0))],
        out_specs=pl.BlockSpec((1, 1, m, d), lambda bi, ei, f: (bi, ei, 0, 0)),
        out_shape=jax.ShapeDtypeStruct((b, e, m, d), F32),
        scratch_shapes=[pltpu.VMEM((2, m, dh), jnp.uint32), pltpu.VMEM((2, m, LANES), F32),
                        pltpu.VMEM((m, d), BF16), pltpu.VMEM((m, LANES), F32)],
        compiler_params=_params("arbitrary", "arbitrary", "arbitrary"),
        name="expert_ffn",
    )(idx4, idx4, hp, afft, gate2, w1, w3, w2)


def _scatter_kernel(idx_ref, y_ref, o_ref, *, m):
    ei = pl.program_id(1)

    @pl.when(ei == 0)
    def _():
        o_ref[...] = jnp.zeros(o_ref.shape, F32)

    def add_rows(g, c):
        base = pl.multiple_of(g * 8, 8)
        rows = [idx_ref[0, 0, 0, base + r] for r in range(8)]
        sums = [o_ref[0, pl.ds(i, 1), :] + y_ref[0, 0, pl.ds(base + r, 1), :] for r, i in enumerate(rows)]
        for i, v in zip(rows, sums):
            o_ref[0, pl.ds(i, 1), :] = v
        return c

    lax.fori_loop(0, m // 8, add_rows, 0)


def moe_scatter(idx, y, n):
    b, e, m = idx.shape
    d = y.shape[-1]
    return pl.pallas_call(
        functools.partial(_scatter_kernel, m=m),
        grid=(b, e),
        in_specs=[pl.BlockSpec((1, 1, 1, m), lambda bi, ei: (bi, ei, 0, 0), memory_space=pltpu.SMEM),
                  pl.BlockSpec((1, 1, m, d), lambda bi, ei: (bi, ei, 0, 0))],
        out_specs=pl.BlockSpec((1, n, d), lambda bi, ei: (bi, 0, 0), pipeline_mode=pl.Buffered(1)),
        out_shape=jax.ShapeDtypeStruct((b, n, d), F32),
        compiler_params=_params("parallel", "arbitrary"),
        name="moe_scatter",
    )(idx.reshape(b, e, 1, m), y)


def _lambda_init(layer):
    return 0.8 - 0.6 * math.exp(-0.3 * layer)


def kernel(x, c, ctx, c_ctx, ada_w, ada_b, norm_mix, norm_ffn, final_norm_w,
           da_wqkv, da_wo, da_lam_q1, da_lam_k1, da_lam_q2, da_lam_k2, da_subln,
           ssm_w_in, ssm_conv_w, ssm_conv_b, ssm_dt_bias, ssm_A_log, ssm_D, ssm_norm, ssm_w_out,
           na_wqkv, na_wo, na_rpb, router_w, exp_w1, exp_w3, exp_w2):
    b, seq, d = x.shape
    n_ctx = ctx.shape[1]
    n = n_ctx + seq
    depth = ada_w.shape[0]
    n_exp = router_w.shape[-1]

    cond_rows = jnp.zeros((8, d), F32).at[:b].set(c).at[b].set(c_ctx)
    mods = ada_modulation(cond_rows, ada_w, ada_b).reshape(depth, 8, 6, d)
    mods = jnp.stack([jnp.broadcast_to(mods[:, b:b + 1], (depth, b, 6, d)), mods[:, :b]], axis=2)

    xj = jnp.concatenate([ctx, x], axis=1)
    rope_tabs = rope_tables(n_ctx, seq, DA_HEAD_DIM ** -0.5 * math.log2(math.e))

    ia = ib = ic = 0
    moe_delta = None
    for layer in range(depth):
        need_ctx = layer < depth - 1
        sh_m, sc_m, g_m, sh_f, sc_f, g_f = [mods[layer, :, :, k] for k in range(6)]
        if moe_delta is None:
            h = norm_mod(xj, norm_mix[layer], sh_m, sc_m, n_ctx)
        else:
            h, xj = norm_mod(xj, norm_mix[layer], sh_m, sc_m, n_ctx, delta=moe_delta)
        kind = layer % N_MIXERS
        if kind == 0:
            qt, k, vt = da_qkv(h, da_wqkv[ia], rope_tabs, DA_KV_CHUNK)
            lam_params = (da_lam_q1[ia], da_lam_k1[ia], da_lam_q2[ia], da_lam_k2[ia])
            li = _lambda_init(layer)
            o_l = diff_attention(qt, k, vt, lam_params, da_subln[ia], li, n_ctx, seq, n, DA_Q_TILE)
            o_c = diff_attention(qt, k, vt, lam_params, da_subln[ia], li, 0, n_ctx, n_ctx, n_ctx)
            o = jnp.concatenate([o_c, o_l], axis=1)
            xj = matmul(o, da_wo[ia].astype(BF16), F32, res=xj, gate=g_m, n_ctx=n_ctx)
            ia += 1
        elif kind == 1:
            nh = ssm_A_log.shape[-1]
            di = nh * SSM_HEAD_DIM
            conv_dim = ssm_conv_w.shape[-1]
            w_in = ssm_w_in[ib].astype(BF16)
            z = matmul(h, w_in[:, :di], F32, tn=1024)
            xbc = matmul(h, w_in[:, di:di + conv_dim], F32, tn=1024)
            w_dt = jnp.zeros((d, LANES), BF16).at[:, :2 * nh].set(w_in[:, di + conv_dim:])
            dt_raw = matmul(h, w_dt, F32)
            xbc = conv_silu(xbc, ssm_conv_w[ib], ssm_conv_b[ib], n_ctx)
            expand = (jnp.arange(di)[None, :] // SSM_HEAD_DIM == jnp.arange(nh)[:, None]).astype(BF16)
            ys = [ssd_scan(xbc, dt_raw, ssm_dt_bias[ib], ssm_A_log[ib], expand, dirn, nh, n_ctx)
                  for dirn in range(2)]
            d_chan = jnp.repeat(ssm_D[ib, 0] + ssm_D[ib, 1], SSM_HEAD_DIM).reshape(1, di)
            xj = ssm_out(ys[0], ys[1], xbc, z, d_chan, ssm_norm[ib], ssm_w_out[ib].astype(BF16),
                         xj, g_m, n_ctx)
            ib += 1
        else:
            qkv = matmul(h, na_wqkv[ic].astype(BF16), BF16, tn=1024)
            bias_tab = na_bias_table(na_rpb[ic], seq // GRID_W)
            o_l = neighbourhood_attention(qkv, bias_tab, n_ctx)
            o_c = context_attention(qkv, n_ctx)
            o = jnp.concatenate([o_c, o_l], axis=1)
            xj = matmul(o, na_wo[ic].astype(BF16), F32, res=xj, gate=g_m, n_ctx=n_ctx)
            ic += 1

        hp, aff, afft = norm_mod(xj, norm_ffn[layer], sh_f, sc_f, n_ctx, router_w=router_w[layer])
        sets = [(n_ctx, seq, EC_CAPACITY_FACTOR * seq // n_exp)]
        if need_ctx:
            sets.append((0, n_ctx, EC_CAPACITY_FACTOR * n_ctx // n_exp))
        idx = jnp.swapaxes(route(aff, sets)[:, :, :n_exp], 1, 2)
        y = expert_ffn(idx, hp, afft, g_f, exp_w1, exp_w3, exp_w2, layer, sets[0][2])
        moe_delta = moe_scatter(idx, y, n)

    return final_norm(xj, moe_delta, final_norm_w, n_ctx)
```

```python
import functools
import math

import jax
import jax.numpy as jnp
import numpy as np
from jax import lax
from jax.experimental import pallas as pl
from jax.experimental.pallas import tpu as pltpu

F32 = jnp.float32
BF16 = jnp.bfloat16

GRID_W = 64
NORM_EPS = 1e-6
ROPE_BASE = 10000.0
DA_HEADS = 8
DA_HEAD_DIM = 64
DA_KV_CHUNK = 768
DA_GROUP_LANES = 256
DA_Q_TILE = 512
DA_SCORE_BUFFERS = 4
DA_SUM_ROWS = 16
SSM_HEAD_DIM = 64
SSM_GROUPS = 4
SSM_STATE = 128
SSM_CHUNK = 128
NA_HEADS = 16
NA_ROWS = 8
NA_COLS = 16
N_MIXERS = 3
EC_CAPACITY_FACTOR = 2
MASK_VALUE = -1e30
ONE_BITS = 0x3F800000
ROUTE_PAD_COUNT = 1e9
LANES = 128
VMEM_LIMIT = 56 * 1024 * 1024


def _params(*sem):
    return pltpu.CompilerParams(dimension_semantics=sem, vmem_limit_bytes=VMEM_LIMIT)


def _split3(a):
    hi = a.astype(BF16)
    r1 = a - hi.astype(F32)
    mid = r1.astype(BF16)
    lo = (r1 - mid.astype(F32)).astype(BF16)
    return hi, mid, lo


def _dot(a, b):
    return jnp.dot(a, b, preferred_element_type=F32)


def _dot_nt(a, b):
    return lax.dot_general(a, b, (((1,), (1,)), ((), ())), preferred_element_type=F32)


def _dot_tn(a, b):
    return lax.dot_general(a, b, (((0,), (0,)), ((), ())), preferred_element_type=F32)


def _ada_kernel(c_ref, w_ref, b_ref, o_ref):
    c = c_ref[...]
    cond = c * jax.nn.sigmoid(c)
    o_ref[0] = jnp.dot(cond, w_ref[0], preferred_element_type=F32,
                       precision=lax.Precision.HIGHEST) + b_ref[0]


def ada_modulation(cond_rows, ada_w, ada_b):
    depth, d, n6 = ada_w.shape
    tn = 1536
    return pl.pallas_call(
        _ada_kernel,
        grid=(depth, n6 // tn),
        in_specs=[pl.BlockSpec((8, d), lambda l, j: (0, 0)),
                  pl.BlockSpec((1, d, tn), lambda l, j: (l, 0, j)),
                  pl.BlockSpec((1, 1, tn), lambda l, j: (l, 0, j))],
        out_specs=pl.BlockSpec((1, 8, tn), lambda l, j: (l, 0, j)),
        out_shape=jax.ShapeDtypeStruct((depth, 8, n6), F32),
        compiler_params=_params("parallel", "parallel"),
        name="ada_modulation",
    )(cond_rows, ada_w, ada_b.reshape(depth, 1, n6))


def _row_select(i, tm, n_ctx, vec2):
    row = i * tm + lax.broadcasted_iota(jnp.int32, (tm, 1), 0)
    return jnp.where(row >= n_ctx, vec2[1:2, :], vec2[0:1, :])


def _norm_mod_kernel(x_ref, w_ref, sh_ref, sc_ref, *rest, tm, n_ctx, n_exp, with_delta):
    i = pl.program_id(1)
    xf = x_ref[0]
    if with_delta:
        xf = xf + rest[0][0]
        rest = rest[1:]
    y = xf * lax.rsqrt(jnp.mean(xf * xf, axis=-1, keepdims=True) + NORM_EPS) * w_ref[...]
    sc = _row_select(i, tm, n_ctx, sc_ref[0])
    sh = _row_select(i, tm, n_ctx, sh_ref[0])
    h = y * (1.0 + sc) + sh
    if not n_exp:
        h_ref = rest[0]
        h_ref[0] = h.astype(h_ref.dtype)
        if with_delta:
            rest[1][0] = xf
        return
    rwt_ref, hp_ref, aff_ref, afft_ref = rest
    bits = lax.bitcast_convert_type(h.astype(BF16).astype(F32), jnp.uint32)
    dh = bits.shape[1] // 2
    hp_ref[0] = (bits[:, :dh] >> 16) | bits[:, dh:]
    logits = lax.dot_general(rwt_ref[...], h, (((1,), (1,)), ((), ())),
                             preferred_element_type=F32, precision=lax.Precision.HIGHEST)
    e = jnp.exp(logits - jnp.max(logits, axis=0, keepdims=True))
    aff = e / jnp.sum(e, axis=0, keepdims=True)
    aff_ref[0] = aff
    afft_ref[0] = jnp.concatenate([aff, jnp.zeros((LANES - n_exp, aff.shape[1]), F32)], axis=0).T


def norm_mod(x, w, shift, scale, n_ctx, router_w=None, delta=None, tm=256):
    b, n, d = x.shape
    n_exp = 0 if router_w is None else router_w.shape[1]
    with_delta = delta is not None
    assert not (n_exp and with_delta)
    row = pl.BlockSpec((1, tm, d), lambda bi, i: (bi, i, 0))
    in_specs = [row,
                pl.BlockSpec((1, d), lambda bi, i: (0, 0)),
                pl.BlockSpec((1, 2, d), lambda bi, i: (bi, 0, 0)),
                pl.BlockSpec((1, 2, d), lambda bi, i: (bi, 0, 0))]
    args = [x, w.reshape(1, d), shift, scale]
    if not n_exp:
        out_specs = [row]
        out_shape = [jax.ShapeDtypeStruct((b, n, d), BF16)]
        if with_delta:
            in_specs.append(row)
            args.append(delta)
            out_specs.append(row)
            out_shape.append(jax.ShapeDtypeStruct((b, n, d), F32))
    else:
        in_specs += [pl.BlockSpec((n_exp, d), lambda bi, i: (0, 0))]
        args += [router_w.T]
        out_specs = [pl.BlockSpec((1, tm, d // 2), lambda bi, i: (bi, i, 0)),
                     pl.BlockSpec((1, n_exp, tm), lambda bi, i: (bi, 0, i)),
                     pl.BlockSpec((1, tm, LANES), lambda bi, i: (bi, i, 0))]
        out_shape = [jax.ShapeDtypeStruct((b, n, d // 2), jnp.uint32),
                     jax.ShapeDtypeStruct((b, n_exp, n), F32),
                     jax.ShapeDtypeStruct((b, n, LANES), F32)]
    outs = pl.pallas_call(
        functools.partial(_norm_mod_kernel, tm=tm, n_ctx=n_ctx, n_exp=n_exp, with_delta=with_delta),
        grid=(b, n // tm),
        in_specs=in_specs, out_specs=out_specs, out_shape=out_shape,
        compiler_params=_params("parallel", "parallel"),
        name="norm_mod_router" if n_exp else "norm_mod",
    )(*args)
    return outs if (n_exp or with_delta) else outs[0]


def _final_norm_kernel(x_ref, d_ref, w_ref, o_ref):
    xf = x_ref[0] + d_ref[0]
    o_ref[0] = xf * lax.rsqrt(jnp.mean(xf * xf, axis=-1, keepdims=True) + NORM_EPS) * w_ref[...]


def final_norm(x, delta, w, n_ctx, tm=256):
    b, n, d = x.shape
    off = n_ctx // tm
    row = pl.BlockSpec((1, tm, d), lambda bi, i: (bi, i + off, 0))
    return pl.pallas_call(
        _final_norm_kernel,
        grid=(b, (n - n_ctx) // tm),
        in_specs=[row, row, pl.BlockSpec((1, d), lambda bi, i: (0, 0))],
        out_specs=pl.BlockSpec((1, tm, d), lambda bi, i: (bi, i, 0)),
        out_shape=jax.ShapeDtypeStruct((b, n - n_ctx, d), F32),
        compiler_params=_params("parallel", "parallel"),
        name="final_norm",
    )(x, delta, w.reshape(1, d))


def _matmul_kernel(a_ref, w_ref, *rest, tm, n_ctx, residual):
    if residual:
        res_ref, g_ref, o_ref = rest
    else:
        (o_ref,) = rest
    acc = _dot(a_ref[0], w_ref[...])
    if residual:
        g = _row_select(pl.program_id(1), tm, n_ctx, g_ref[0])
        acc = res_ref[0] + g * acc
    o_ref[0] = acc.astype(o_ref.dtype)


def matmul(a, w, out_dtype, res=None, gate=None, n_ctx=0, tm=768, tn=None):
    b, n, k = a.shape
    m = w.shape[1]
    tn = m if tn is None else tn
    residual = res is not None
    in_specs = [pl.BlockSpec((1, tm, k), lambda bi, i, j: (bi, i, 0)),
                pl.BlockSpec((k, tn), lambda bi, i, j: (0, j))]
    args = [a, w]
    if residual:
        in_specs += [pl.BlockSpec((1, tm, tn), lambda bi, i, j: (bi, i, j)),
                     pl.BlockSpec((1, 2, tn), lambda bi, i, j: (bi, 0, j))]
        args += [res, gate]
    return pl.pallas_call(
        functools.partial(_matmul_kernel, tm=tm, n_ctx=n_ctx, residual=residual),
        grid=(b, n // tm, m // tn),
        in_specs=in_specs,
        out_specs=pl.BlockSpec((1, tm, tn), lambda bi, i, j: (bi, i, j)),
        out_shape=jax.ShapeDtypeStruct((b, n, m), out_dtype),
        compiler_params=_params("parallel", "parallel", "arbitrary"),
        name="matmul_res" if residual else "matmul",
    )(*args)


def _rotate_pairs(x, axis):
    seg = lax.broadcasted_iota(jnp.int32, x.shape, axis) // 16
    n = x.shape[axis]
    return jnp.where(seg % 2 == 0, pltpu.roll(x, n - 16, axis), pltpu.roll(x, 16, axis))


def _da_qkv_kernel(h_ref, wqt_ref, wk_ref, wvt_ref, cq_ref, sq_ref, ck_ref, sk_ref,
                   qt_ref, k_ref, vt_ref, *, heads):
    h = h_ref[0]
    qt = _dot_nt(wqt_ref[...], h)
    cq, sq = cq_ref[...], sq_ref[...]
    for hd in range(heads):
        rs = slice(hd * LANES, (hd + 1) * LANES)
        x = qt[rs]
        qt_ref[0, rs, :] = (x * cq + _rotate_pairs(x, 0) * sq).astype(qt_ref.dtype)
    k = _dot(h, wk_ref[...])
    ck, sk = ck_ref[...], sk_ref[...]
    for hd in range(heads):
        cs = slice(hd * LANES, (hd + 1) * LANES)
        x = k[:, cs]
        k_ref[0, :, cs] = (x * ck + _rotate_pairs(x, 1) * sk).astype(k_ref.dtype)
    vt = _dot_nt(wvt_ref[...], h).astype(vt_ref.dtype)
    ones = jnp.ones((DA_SUM_ROWS, vt.shape[1]), vt_ref.dtype)
    for hd in range(heads):
        vt_ref[0, 0, hd, :LANES, :] = vt[hd * LANES:(hd + 1) * LANES]
        vt_ref[0, 0, hd, LANES:, :] = ones


def da_qkv(h, wqkv, tabs, tm):
    b, n, d = h.shape
    m = wqkv.shape[1] // 3
    cq_t, sq_t, ck, sk = tabs
    wqt = wqkv[:, :m].T.astype(BF16)
    wk = wqkv[:, m:2 * m].astype(BF16)
    wvt = wqkv[:, 2 * m:].T.astype(BF16)
    full = lambda r, c: pl.BlockSpec((r, c), lambda bi, i: (0, 0))
    return pl.pallas_call(
        functools.partial(_da_qkv_kernel, heads=m // LANES),
        grid=(b, n // tm),
        in_specs=[pl.BlockSpec((1, tm, d), lambda bi, i: (bi, i, 0)),
                  full(m, d), full(d, m), full(m, d),
                  pl.BlockSpec((LANES, tm), lambda bi, i: (0, i)),
                  pl.BlockSpec((LANES, tm), lambda bi, i: (0, i)),
                  pl.BlockSpec((tm, LANES), lambda bi, i: (i, 0)),
                  pl.BlockSpec((tm, LANES), lambda bi, i: (i, 0))],
        out_specs=[pl.BlockSpec((1, m, tm), lambda bi, i: (bi, 0, i)),
                   pl.BlockSpec((1, tm, m), lambda bi, i: (bi, i, 0)),
                   pl.BlockSpec((1, 1, m // LANES, LANES + DA_SUM_ROWS, tm), lambda bi, i: (bi, i, 0, 0, 0))],
        out_shape=[jax.ShapeDtypeStruct((b, m, n), BF16),
                   jax.ShapeDtypeStruct((b, n, m), BF16),
                   jax.ShapeDtypeStruct((b, n // tm, m // LANES, LANES + DA_SUM_ROWS, tm), BF16)],
        compiler_params=_params("parallel", "parallel"),
        name="da_qkv",
    )(h, wqt, wk, wvt, cq_t, sq_t, ck, sk)


def rope_tables(n_ctx, seq, q_scale):
    d = DA_HEAD_DIM
    t = jnp.arange(seq)
    rows = (t // GRID_W).astype(F32)
    cols = (t % GRID_W).astype(F32)
    quarter = d // 4
    freqs = ROPE_BASE ** (-jnp.arange(quarter, dtype=F32) / quarter)
    ang_r = rows[:, None] * freqs[None, :]
    ang_c = cols[:, None] * freqs[None, :]
    ang = jnp.concatenate([ang_r, ang_r, ang_c, ang_c], axis=-1)
    cos = jnp.cos(ang)
    sign = jnp.concatenate([-jnp.ones((quarter,), F32), jnp.ones((quarter,), F32)] * 2)
    sin = jnp.sin(ang) * sign[None, :]
    cos = jnp.concatenate([jnp.ones((n_ctx, d), F32), cos], axis=0)
    sin = jnp.concatenate([jnp.zeros((n_ctx, d), F32), sin], axis=0)
    cos = jnp.concatenate([cos, cos], axis=-1)
    sin = jnp.concatenate([sin, sin], axis=-1)
    return (cos * q_scale).T, (sin * q_scale).T, cos, sin


def _diff_attn_kernel(*refs, nq, tk, kw, nchunks, lam_init):
    qt_refs = refs[:nq]
    k_ref, vt_ref, lq1_ref, lk1_ref, lq2_ref, lk2_ref, sub_ref, o_ref, acc_ref, s_ref = refs[nq:]
    gw = DA_GROUP_LANES
    ngroups = 2 * nq
    sub = lax.broadcasted_iota(jnp.int32, (LANES, gw), 0)
    zero = jnp.zeros((LANES, gw), qt_refs[0].dtype)
    qg = []
    for g in range(ngroups):
        q = qt_refs[g // 2][0]
        qg.append(jnp.where((sub < DA_HEAD_DIM) if g % 2 == 0 else (sub >= DA_HEAD_DIM), q, zero))
    acc_ref[...] = jnp.zeros(acc_ref.shape, F32)
    half = kw // 2

    def col_reduce(x, op):
        return op(op(x.reshape(8, x.shape[0] // 8, x.shape[1]), axis=0), axis=0, keepdims=True)

    def scores(t):
        c, g = divmod(t, ngroups)
        mx = None
        for hf in range(2):
            k = k_ref[0, c * tk + hf * half:c * tk + (hf + 1) * half, :]
            s = _dot(k, qg[g])
            s_ref[t % DA_SCORE_BUFFERS, hf * half:(hf + 1) * half, :] = s
            hm = col_reduce(s, jnp.max)
            mx = hm if mx is None else jnp.maximum(mx, hm)
        return mx

    def softmax_pv(t, mx, ml):
        c, g = divmod(t, ngroups)
        m_old, l_old = ml
        vt = vt_ref[0, c, 0]
        m_new = jnp.maximum(m_old, mx)
        alpha = jnp.exp2(m_old - m_new)
        pb = jnp.exp2(s_ref[t % DA_SCORE_BUFFERS] - m_new).astype(BF16)
        pv = _dot(vt[:, :kw], pb)
        acc_ref[g] = alpha * acc_ref[g] + pv[:LANES]
        return m_new, alpha * l_old + pv[LANES:LANES + 1]

    nsteps = nchunks * ngroups
    ahead = DA_SCORE_BUFFERS - 1
    carry = [(jnp.full((1, gw), -jnp.inf, F32), jnp.zeros((1, gw), F32))] * ngroups
    maxima = {t: scores(t) for t in range(min(ahead, nsteps))}
    for t in range(nsteps):
        if t + ahead < nsteps:
            maxima[t + ahead] = scores(t + ahead)
        carry[t % ngroups] = softmax_pv(t, maxima.pop(t), carry[t % ngroups])
    lam = (jnp.exp(jnp.sum(lq1_ref[...] * lk1_ref[...], axis=-1, keepdims=True))
           - jnp.exp(jnp.sum(lq2_ref[...] * lk2_ref[...], axis=-1, keepdims=True)) + lam_init)
    for qh in range(ngroups // 2):
        od = acc_ref[2 * qh] / carry[2 * qh][1] - lam * (acc_ref[2 * qh + 1] / carry[2 * qh + 1][1])
        y = od * lax.rsqrt(jnp.mean(od * od, axis=0, keepdims=True) + NORM_EPS) * sub_ref[...]
        o_ref[0, qh * gw:(qh + 1) * gw, :] = (y * (1.0 - lam_init)).T.astype(o_ref.dtype)


def diff_attention(qt, k, vt, lam_params, subln, lam_init, q_start, n_q, n_k, tq):
    b, m, n = qt.shape
    h = m // LANES
    tk = vt.shape[-1]
    kw = min(tk, n_k)
    nchunks = n_k // kw
    gw = DA_GROUP_LANES
    nq = tq // gw
    assert q_start % gw == 0 and tq % gw == 0 and n_q % tq == 0
    qoff = q_start // gw
    vec = lambda a: a.reshape(1, -1).astype(F32)
    small = pl.BlockSpec((1, DA_HEAD_DIM), lambda bi, hi, i: (0, 0))
    q_specs = [pl.BlockSpec((1, LANES, gw), functools.partial(
        lambda bi, hi, i, j: (bi, hi, qoff + nq * i + j), j=j)) for j in range(nq)]
    return pl.pallas_call(
        functools.partial(_diff_attn_kernel, nq=nq, tk=tk, kw=kw, nchunks=nchunks, lam_init=lam_init),
        grid=(b, h, n_q // tq),
        in_specs=q_specs + [
                  pl.BlockSpec((1, n_k, LANES), lambda bi, hi, i: (bi, 0, hi)),
                  pl.BlockSpec((1, nchunks, 1, vt.shape[3], tk), lambda bi, hi, i: (bi, 0, hi, 0, 0)),
                  small, small, small, small,
                  pl.BlockSpec((LANES, 1), lambda bi, hi, i: (0, 0))],
        out_specs=pl.BlockSpec((1, tq, LANES), lambda bi, hi, i: (bi, i, hi)),
        out_shape=jax.ShapeDtypeStruct((b, n_q, m), BF16),
        scratch_shapes=[pltpu.VMEM((2 * tq // DA_GROUP_LANES, LANES, DA_GROUP_LANES), F32),
                        pltpu.VMEM((DA_SCORE_BUFFERS, kw, DA_GROUP_LANES), F32)],
        compiler_params=_params("parallel", "parallel", "arbitrary"),
        name="diff_attention",
    )(*([qt] * nq), k, vt, *[vec(p) for p in lam_params], subln.reshape(-1, 1).astype(F32))


def _pair_scores(qp, kp):
    lane = lax.broadcasted_iota(jnp.int32, qp.shape, 1)
    zero = jnp.zeros_like(qp)
    qs = jnp.concatenate([jnp.where(lane < 64, qp, zero), jnp.where(lane >= 64, qp, zero)], axis=0)
    return _dot_nt(qs, kp)


def _pair_softmax_pv(s, vp, bias, n_bias):
    tq = s.shape[0] // 2
    if n_bias:
        s_w = s[:, :n_bias] + bias
        s_c = s[:, n_bias:]
        m = jnp.maximum(jnp.max(s_w, axis=-1, keepdims=True), jnp.max(s_c, axis=-1, keepdims=True))
        p_w = jnp.exp(s_w - m)
        p_c = jnp.exp(s_c - m)
        l = jnp.sum(p_w, axis=-1, keepdims=True) + jnp.sum(p_c, axis=-1, keepdims=True)
        pv = _dot(p_w.astype(BF16), vp[:n_bias]) + _dot(p_c.astype(BF16), vp[n_bias:])
    else:
        m = jnp.max(s, axis=-1, keepdims=True)
        p = jnp.exp(s - m)
        l = jnp.sum(p, axis=-1, keepdims=True)
        pv = _dot(p.astype(BF16), vp)
    o = pv / l
    lane_o = lax.broadcasted_iota(jnp.int32, (tq, LANES), 1)
    return jnp.where(lane_o < 64, o[:tq], o[tq:])


def _pairs_attention(q_of, k_of, v_of, bias_of, n_bias, o_ref):
    npairs = NA_HEADS // 2
    s_next = _pair_scores(q_of(0), k_of(0))
    for hp in range(npairs):
        s_cur = s_next
        if hp + 1 < npairs:
            s_next = _pair_scores(q_of(hp + 1), k_of(hp + 1))
        o = _pair_softmax_pv(s_cur, v_of(hp), bias_of(hp), n_bias)
        o_ref[0, :, hp * LANES:(hp + 1) * LANES] = o.astype(o_ref.dtype)


def _na_kernel(*refs, wr, n_ctx, scale):
    q_ref = refs[0]
    k_refs = refs[1:1 + wr]
    v_refs = refs[1 + wr:1 + 2 * wr]
    kc_ref, vc_ref, bias_ref, o_ref, kbuf, vbuf = refs[1 + 2 * wr:]
    nw = wr * GRID_W
    for w in range(wr):
        kbuf[w * GRID_W:(w + 1) * GRID_W, :] = k_refs[w][0]
        vbuf[w * GRID_W:(w + 1) * GRID_W, :] = v_refs[w][0]
    kbuf[nw:nw + n_ctx, :] = kc_ref[0]
    vbuf[nw:nw + n_ctx, :] = vc_ref[0]
    cols = lambda hp: slice(hp * LANES, (hp + 1) * LANES)
    _pairs_attention(lambda hp: q_ref[0, :, cols(hp)] * scale,
                     lambda hp: kbuf[:, cols(hp)],
                     lambda hp: vbuf[:, cols(hp)],
                     lambda hp: bias_ref[0, 2 * hp:2 * hp + 2].reshape(2 * GRID_W, nw),
                     nw, o_ref)


def na_bias_table(rpb, rows):
    wr = min(NA_ROWS, rows)
    qcol = np.arange(GRID_W)
    col_start = np.clip(qcol - NA_COLS // 2, 0, GRID_W - NA_COLS)
    kc = np.arange(GRID_W)
    inside = (kc[None, :] >= col_start[:, None]) & (kc[None, :] < col_start[:, None] + NA_COLS)
    dc = np.clip(kc[None, :] - qcol[:, None] + NA_COLS - 1, 0, 2 * NA_COLS - 2)
    dr = np.arange(wr)[None, :] - np.arange(wr)[:, None] + NA_ROWS - 1
    sel_r = (dr[:, :, None] == np.arange(2 * NA_ROWS - 1)).astype(np.float32)
    sel_c = (dc[:, :, None] == np.arange(2 * NA_COLS - 1)).astype(np.float32)
    tab = jnp.einsum("cwa,hab,qkb->chqwk", sel_r, rpb.astype(F32), sel_c,
                     precision=lax.Precision.HIGHEST)
    tab = jnp.where(inside[None, None, :, None, :], tab, MASK_VALUE)
    return tab.reshape(wr, rpb.shape[0], GRID_W, wr * GRID_W)


def neighbourhood_attention(qkv, bias_tab, n_ctx):
    b, n, m3 = qkv.shape
    m = m3 // 3
    rows = (n - n_ctx) // GRID_W
    wr = min(NA_ROWS, rows)
    cb = n_ctx // GRID_W
    rs = lambda r: jnp.clip(r - wr // 2, 0, rows - wr)
    blk = lambda f: pl.BlockSpec((1, GRID_W, m), f)
    in_specs = [blk(lambda bi, r: (bi, cb + r, 0))]
    in_specs += [blk(functools.partial(lambda bi, r, w: (bi, cb + rs(r) + w, 1), w=w)) for w in range(wr)]
    in_specs += [blk(functools.partial(lambda bi, r, w: (bi, cb + rs(r) + w, 2), w=w)) for w in range(wr)]
    in_specs += [pl.BlockSpec((1, n_ctx, m), lambda bi, r: (bi, 0, 1)),
                 pl.BlockSpec((1, n_ctx, m), lambda bi, r: (bi, 0, 2)),
                 pl.BlockSpec((1, NA_HEADS, GRID_W, wr * GRID_W), lambda bi, r: (r - rs(r), 0, 0, 0))]
    nk = wr * GRID_W + n_ctx
    return pl.pallas_call(
        functools.partial(_na_kernel, wr=wr, n_ctx=n_ctx, scale=(m // NA_HEADS) ** -0.5),
        grid=(b, rows),
        in_specs=in_specs,
        out_specs=pl.BlockSpec((1, GRID_W, m), lambda bi, r: (bi, r, 0)),
        out_shape=jax.ShapeDtypeStruct((b, n - n_ctx, m), BF16),
        scratch_shapes=[pltpu.VMEM((nk, m), BF16), pltpu.VMEM((nk, m), BF16)],
        compiler_params=_params("parallel", "arbitrary"),
        name="neighbourhood_attention",
    )(*([qkv] * (1 + 2 * wr + 2)), bias_tab)


def _ctx_attn_kernel(q_ref, k_ref, v_ref, o_ref, *, scale):
    cols = lambda hp: slice(hp * LANES, (hp + 1) * LANES)
    _pairs_attention(lambda hp: q_ref[0, :, cols(hp)] * scale,
                     lambda hp: k_ref[0, :, cols(hp)],
                     lambda hp: v_ref[0, :, cols(hp)],
                     lambda hp: None, 0, o_ref)


def context_attention(qkv, n_ctx):
    b, n, m3 = qkv.shape
    m = m3 // 3
    return pl.pallas_call(
        functools.partial(_ctx_attn_kernel, scale=(m // NA_HEADS) ** -0.5),
        grid=(b,),
        in_specs=[pl.BlockSpec((1, n_ctx, m), lambda bi: (bi, 0, 0)),
                  pl.BlockSpec((1, n_ctx, m), lambda bi: (bi, 0, 1)),
                  pl.BlockSpec((1, n_ctx, m), lambda bi: (bi, 0, 2))],
        out_specs=pl.BlockSpec((1, n_ctx, m), lambda bi: (bi, 0, 0)),
        out_shape=jax.ShapeDtypeStruct((b, n_ctx, m), BF16),
        compiler_params=_params("parallel"),
        name="context_attention",
    )(qkv, qkv, qkv)


def _conv_silu_kernel(x_ref, w_ref, b_ref, o_ref, *, n, n_ctx):
    x = x_ref[0]
    row = lax.broadcasted_iota(jnp.int32, x.shape, 0)
    xm = jnp.where((row == 0) | (row == n_ctx), 0.0, pltpu.roll(x, 1, 0))
    xp = jnp.where((row == n_ctx - 1) | (row == n - 1), 0.0, pltpu.roll(x, n - 1, 0))
    y = w_ref[0:1, :] * xm + w_ref[1:2, :] * x + w_ref[2:3, :] * xp + b_ref[...]
    o_ref[0] = y * jax.nn.sigmoid(y)


def conv_silu(x, w, bias, n_ctx):
    b, n, c = x.shape
    return pl.pallas_call(
        functools.partial(_conv_silu_kernel, n=n, n_ctx=n_ctx),
        grid=(b, c // LANES),
        in_specs=[pl.BlockSpec((1, n, LANES), lambda bi, j: (bi, 0, j)),
                  pl.BlockSpec((3, LANES), lambda bi, j: (0, j)),
                  pl.BlockSpec((1, LANES), lambda bi, j: (0, j))],
        out_specs=pl.BlockSpec((1, n, LANES), lambda bi, j: (bi, 0, j)),
        out_shape=jax.ShapeDtypeStruct((b, n, c), F32),
        compiler_params=_params("parallel", "parallel"),
        name="conv_silu",
    )(x, w, bias.reshape(1, c))


def _softplus(x):
    return jnp.maximum(x, 0.0) + jnp.log1p(jnp.exp(-jnp.abs(x)))


def _ssd_kernel(x_ref, b_ref, c_ref, dt_ref, dtb_ref, alog_ref, e_ref, y_ref, state_ref,
                *, direction, nheads):
    L = SSM_CHUNK
    P = SSM_HEAD_DIM
    G = SSM_GROUPS
    hpg = nheads // G
    reverse = direction == 1

    @pl.when(pl.program_id(1) == 0)
    def _():
        state_ref[...] = jnp.zeros(state_ref.shape, F32)

    li = lax.broadcasted_iota(jnp.int32, (L, L), 0)
    si = lax.broadcasted_iota(jnp.int32, (L, L), 1)
    allowed = (si >= li) if reverse else (si <= li)
    tri = jnp.where(allowed, 1.0, 0.0).astype(BF16)
    tri_t = jnp.where((li >= si) if reverse else (li <= si), 1.0, 0.0).astype(BF16)

    dt_raw = dt_ref[0][:, direction * nheads:(direction + 1) * nheads]
    dt = _softplus(dt_raw + dtb_ref[...])
    a = dt * (-jnp.exp(alog_ref[...]))
    a3 = _split3(a)
    acs = sum(_dot(tri, p) for p in a3)
    acs_t = sum(_dot_tn(p, tri_t) for p in a3)
    expand = e_ref[...]
    acs_x = sum(_dot(p, expand) for p in _split3(acs))
    dt_x = sum(_dot(p, expand) for p in _split3(dt))
    end = 0 if reverse else L - 1
    total_x = acs_x[end:end + 1, :]

    xd = x_ref[0] * dt_x
    xd_b = xd.astype(BF16)
    x_state = (xd * jnp.exp(total_x - acs_x)).astype(BF16)
    exp_acs_x = jnp.exp(acs_x)
    chunk_decay_x = jnp.exp(total_x)
    lane = lax.broadcasted_iota(jnp.int32, (L, LANES), 1)

    for g in range(G):
        gs = slice(g * SSM_STATE, (g + 1) * SSM_STATE)
        hs = slice(g * hpg * P, (g + 1) * hpg * P)
        cg = c_ref[0][:, gs].astype(BF16)
        bg = b_ref[0][:, gs].astype(BF16)
        cb = _dot_nt(cg, bg)
        h_prev = state_ref[:, hs]
        y_off = _dot(cg, h_prev.astype(BF16)) * exp_acs_x[:, hs]
        state_ref[:, hs] = h_prev * chunk_decay_x[:, hs] + _dot_tn(bg, x_state[:, hs])
        for rp in range(hpg // 2):
            mats = []
            for r in (2 * rp, 2 * rp + 1):
                hidx = g * hpg + r
                seg = acs[:, hidx:hidx + 1] - acs_t[hidx:hidx + 1, :]
                mats.append((cb * jnp.exp(jnp.where(allowed, seg, -jnp.inf))).astype(BF16))
            ps = slice((g * hpg + 2 * rp) * P, (g * hpg + 2 * rp + 2) * P)
            yd = _dot(jnp.concatenate(mats, axis=0), xd_b[:, ps])
            y_ref[0, :, ps] = jnp.where(lane < P, yd[:L], yd[L:]) + y_off[:, 2 * rp * P:(2 * rp + 2) * P]


def ssd_scan(xbc, dt_raw, dt_bias, a_log, expand, direction, nheads, n_ctx):
    b, n, _ = xbc.shape
    d_inner = nheads * SSM_HEAD_DIM
    gn = SSM_GROUPS * SSM_STATE
    L = SSM_CHUNK
    nc = n // L
    ncc = n_ctx // L
    if direction == 0:
        cidx = lambda c: c
    else:
        cidx = lambda c: jnp.where(c < ncc, ncc - 1 - c, nc - 1 - (c - ncc))
    return pl.pallas_call(
        functools.partial(_ssd_kernel, direction=direction, nheads=nheads),
        grid=(b, nc),
        in_specs=[pl.BlockSpec((1, L, d_inner), lambda bi, c: (bi, cidx(c), 0)),
                  pl.BlockSpec((1, L, gn), lambda bi, c: (bi, cidx(c), d_inner // gn)),
                  pl.BlockSpec((1, L, gn), lambda bi, c: (bi, cidx(c), d_inner // gn + 1)),
                  pl.BlockSpec((1, L, LANES), lambda bi, c: (bi, cidx(c), 0)),
                  pl.BlockSpec((1, nheads), lambda bi, c: (0, 0)),
                  pl.BlockSpec((1, nheads), lambda bi, c: (0, 0)),
                  pl.BlockSpec((nheads, d_inner), lambda bi, c: (0, 0))],
        out_specs=pl.BlockSpec((1, L, d_inner), lambda bi, c: (bi, cidx(c), 0)),
        out_shape=jax.ShapeDtypeStruct((b, n, d_inner), F32),
        scratch_shapes=[pltpu.VMEM((SSM_STATE, d_inner), F32)],
        compiler_params=_params("parallel", "arbitrary"),
        name="ssd_scan_%d" % direction,
    )(xbc, xbc, xbc, dt_raw, dt_bias[direction:direction + 1], a_log[direction:direction + 1], expand)


def _ssm_out_kernel(y0_ref, y1_ref, x_ref, z_ref, d_ref, nw_ref, w_ref, res_ref, g_ref, o_ref,
                    *, tm, n_ctx):
    z = z_ref[0]
    u = (y0_ref[0] + y1_ref[0] + d_ref[...] * x_ref[0]) * (z * jax.nn.sigmoid(z))
    un = u * lax.rsqrt(jnp.mean(u * u, axis=-1, keepdims=True) + NORM_EPS) * nw_ref[...]
    g = _row_select(pl.program_id(1), tm, n_ctx, g_ref[0])
    o_ref[0] = res_ref[0] + g * _dot(un.astype(BF16), w_ref[...])


def ssm_out(y0, y1, xbc, z, d_chan, norm_w, w_out, res, gate, n_ctx, tm=256):
    b, n, di = y0.shape
    d = w_out.shape[1]
    row = lambda c: pl.BlockSpec((1, tm, c), lambda bi, i: (bi, i, 0))
    vec = lambda c: pl.BlockSpec((1, c), lambda bi, i: (0, 0))
    return pl.pallas_call(
        functools.partial(_ssm_out_kernel, tm=tm, n_ctx=n_ctx),
        grid=(b, n // tm),
        in_specs=[row(di), row(di), row(di), row(di), vec(di), vec(di),
                  pl.BlockSpec((di, d), lambda bi, i: (0, 0)), row(d),
                  pl.BlockSpec((1, 2, d), lambda bi, i: (bi, 0, 0))],
        out_specs=row(d),
        out_shape=jax.ShapeDtypeStruct((b, n, d), F32),
        compiler_params=_params("parallel", "parallel"),
        name="ssm_out",
    )(y0, y1, xbc, z, d_chan, norm_w.reshape(1, di), w_out, res, gate)


def _cumsum_lanes(x):
    n = x.shape[-1]
    lane = lax.broadcasted_iota(jnp.int32, x.shape, x.ndim - 1)
    s = 1
    while s < n:
        x = x + jnp.where(lane >= s, pltpu.roll(x, s, x.ndim - 1), 0.0)
        s *= 2
    return x


def _route_set(aff, cap):
    e, t = aff.shape
    bits = lax.bitcast_convert_type(aff, jnp.int32)

    def count_ge(thr):
        return jnp.sum(jnp.where(bits >= thr, 1.0, 0.0), axis=1, keepdims=True)

    def step(_, lohi):
        lo, hi = lohi
        mid = lo + (hi - lo) // 2
        ok = count_ge(mid) >= cap
        return jnp.where(ok, mid, lo), jnp.where(ok, hi, mid)

    lo0 = jnp.zeros((e, 1), jnp.int32)
    hi0 = jnp.full((e, 1), ONE_BITS + 1, jnp.int32)
    thr, _ = lax.fori_loop(0, 31, step, (lo0, hi0))
    gt = jnp.where(bits > thr, 1.0, 0.0)
    eq = jnp.where(bits == thr, 1.0, 0.0)
    need = cap - jnp.sum(gt, axis=1, keepdims=True)
    sel = gt + eq * jnp.where(_cumsum_lanes(eq) <= need, 1.0, 0.0)
    return sel, _cumsum_lanes(sel)


def _route_kernel(aff_ref, idx_ref, cnt_ref, tile_ref, pos_ref, *, sets):
    n_exp = aff_ref.shape[1]
    for t0, t, cap, col0 in sets:
        _, cnt = _route_set(aff_ref[0, :, t0:t0 + t], cap)
        cnt_ref[:, :t] = cnt
        nt = t // LANES
        ntp = max(nt, 16)
        jp = -(-cap // LANES) * LANES
        tile_ref[...] = jnp.full(tile_ref.shape, ROUTE_PAD_COUNT, F32)
        jrow = lax.broadcasted_iota(jnp.int32, (1, jp), 1).astype(F32)
        ksub = lax.broadcasted_iota(jnp.int32, (ntp, jp), 0)

        def per_expert(ei, c):
            row = cnt_ref[pl.ds(ei, 1), :t]
            for k in range(nt):
                tile_ref[k:k + 1, :] = row[:, k * LANES:(k + 1) * LANES]
            tiles = tile_ref[:ntp]
            full = jnp.sum(jnp.where(tiles[:, LANES - 1:] <= jrow, 1.0, 0.0), axis=0, keepdims=True)
            onehot = jnp.where(ksub == full.astype(jnp.int32), 1.0, 0.0).astype(BF16)
            bound = sum(_dot_tn(p, onehot) for p in _split3(tiles))
            inside = jnp.sum(jnp.where(bound <= jrow, 1.0, 0.0), axis=0, keepdims=True)
            pos = (LANES * full + inside).astype(jnp.int32) + t0
            pad = pos_ref.shape[1] - jp
            if pad:
                pos = jnp.concatenate([pos, jnp.zeros((1, pad), jnp.int32)], axis=1)
            pos_ref[pl.ds(ei, 1), :] = pos
            return c

        lax.fori_loop(0, n_exp, per_expert, 0)
        idx_ref[0, :, col0:col0 + cap] = pos_ref[:, :cap]


def route(aff, sets):
    b, e, n = aff.shape
    m = sum(s[2] for s in sets)
    tmax = max(s[1] for s in sets)
    cols, sets_r = 0, []
    for t0, t, cap in sets:
        sets_r.append((t0, t, cap, cols))
        cols += cap
    return pl.pallas_call(
        functools.partial(_route_kernel, sets=tuple(sets_r)),
        grid=(b,),
        in_specs=[pl.BlockSpec((1, e, n), lambda bi: (bi, 0, 0))],
        out_specs=pl.BlockSpec((1, e, m), lambda bi: (bi, 0, 0)),
        out_shape=jax.ShapeDtypeStruct((b, e, m), jnp.int32),
        scratch_shapes=[pltpu.VMEM((e, tmax), F32), pltpu.VMEM((max(tmax // LANES, 16), LANES), F32),
                        pltpu.VMEM((e, -(-max(s[2] for s in sets) // LANES) * LANES), jnp.int32)],
        compiler_params=_params("parallel"),
        name="route",
    )(aff)


def _expert_ffn_kernel(idx_ref, idxn_ref, hp_ref, afft_ref, g_ref, w1_ref, w3_ref, w2_ref, o_ref,
                       xw_ref, gw_ref, xs_ref, gate_ref, *, m, n_lat, rows_per_step):
    ei = pl.program_id(1)
    f = pl.program_id(2)
    slot = ei % 2

    def copy_row(src_idx_ref, dst_slot, j):
        i = src_idx_ref[0, 0, 0, j]
        xw_ref[dst_slot, pl.ds(j, 1), :] = hp_ref[0, pl.ds(i, 1), :]
        gw_ref[dst_slot, pl.ds(j, 1), :] = afft_ref[0, pl.ds(i, 1), :]

    @pl.when((f == 0) & (ei == 0))
    def _():
        def gather(g, c):
            base = pl.multiple_of(g * 8, 8)
            for r in range(8):
                copy_row(idx_ref, 0, base + r)
            return c

        lax.fori_loop(0, m // 8, gather, 0)

    @pl.when(f == 0)
    def _():
        w = xw_ref[slot]
        lo = lax.bitcast_convert_type(w << 16, F32).astype(BF16)
        hi = lax.bitcast_convert_type(w & jnp.uint32(0xFFFF0000), F32).astype(BF16)
        xs_ref[...] = jnp.concatenate([lo, hi], axis=1)
        lane = lax.broadcasted_iota(jnp.int32, gate_ref.shape, 1)
        gate = jnp.sum(jnp.where(lane == ei, gw_ref[slot], 0.0), axis=1, keepdims=True)
        gate_ref[...] = jnp.broadcast_to(gate, gate_ref.shape)
        o_ref[...] = jnp.zeros(o_ref.shape, F32)

    for r in range(rows_per_step):
        copy_row(idxn_ref, 1 - slot, f * rows_per_step + r)

    x = xs_ref[...]
    h1 = _dot(x, w1_ref[0].astype(BF16))
    h3 = _dot(x, w3_ref[0].astype(BF16))
    hid = (h1 * jax.nn.sigmoid(h1) * h3).astype(BF16)
    part = _dot(hid, w2_ref[0].astype(BF16))

    o_ref[0, 0] = o_ref[0, 0] + part

    @pl.when(f == pl.num_programs(2) - 1)
    def _():
        row = lax.broadcasted_iota(jnp.int32, (m, 1), 0)
        g = jnp.where(row < n_lat, g_ref[0, 1:2, :], g_ref[0, 0:1, :])
        o_ref[0, 0] = o_ref[0, 0] * gate_ref[:, 0:1] * g


def expert_ffn(idx, hp, afft, gate2, w1, w3, w2, layer, n_lat, tf=256):
    b, e, m = idx.shape
    n, dh = hp.shape[1:]
    d = 2 * dh
    ff = w1.shape[3]
    single = dict(pipeline_mode=pl.Buffered(1))
    nf = ff // tf
    assert m % nf == 0
    idx4 = idx.reshape(b, e, 1, m)
    return pl.pallas_call(
        functools.partial(_expert_ffn_kernel, m=m, n_lat=n_lat, rows_per_step=m // nf),
        grid=(b, e, nf),
        in_specs=[pl.BlockSpec((1, 1, 1, m), lambda bi, ei, f: (bi, ei, 0, 0), memory_space=pltpu.SMEM),
                  pl.BlockSpec((1, 1, 1, m), lambda bi, ei, f: (bi, jnp.minimum(ei + 1, e - 1), 0, 0),
                               memory_space=pltpu.SMEM),
                  pl.BlockSpec((1, n, dh), lambda bi, ei, f: (bi, 0, 0), **single),
                  pl.BlockSpec((1, n, LANES), lambda bi, ei, f: (bi, 0, 0), **single),
                  pl.BlockSpec((1, 2, d), lambda bi, ei, f: (bi, 0, 0)),
                  pl.BlockSpec((None, 1, d, tf), lambda bi, ei, f: (layer, ei, 0, f)),
                  pl.BlockSpec((None, 1, d, tf), lambda bi, ei, f: (layer, ei, 0, f)),
                  pl.BlockSpec((None, 1, tf, d), lambda bi, ei, f: (layer, ei, f, 0))],
        out_specs=pl.BlockSpec((1, 1, m, d), lambda bi, ei, f: (bi, ei, 0, 0)),
        out_shape=jax.ShapeDtypeStruct((b, e, m, d), F32),
        scratch_shapes=[pltpu.VMEM((2, m, dh), jnp.uint32), pltpu.VMEM((2, m, LANES), F32),
                        pltpu.VMEM((m, d), BF16), pltpu.VMEM((m, LANES), F32)],
        compiler_params=_params("arbitrary", "arbitrary", "arbitrary"),
        name="expert_ffn",
    )(idx4, idx4, hp, afft, gate2, w1, w3, w2)


def _scatter_kernel(idx_ref, y_ref, o_ref, *, m):
    ei = pl.program_id(1)

    @pl.when(ei == 0)
    def _():
        o_ref[...] = jnp.zeros(o_ref.shape, F32)

    def add_rows(g, c):
        base = pl.multiple_of(g * 8, 8)
        rows = [idx_ref[0, 0, 0, base + r] for r in range(8)]
        sums = [o_ref[0, pl.ds(i, 1), :] + y_ref[0, 0, pl.ds(base + r, 1), :] for r, i in enumerate(rows)]
        for i, v in zip(rows, sums):
            o_ref[0, pl.ds(i, 1), :] = v
        return c

    lax.fori_loop(0, m // 8, add_rows, 0)


def moe_scatter(idx, y, n):
    b, e, m = idx.shape
    d = y.shape[-1]
    return pl.pallas_call(
        functools.partial(_scatter_kernel, m=m),
        grid=(b, e),
        in_specs=[pl.BlockSpec((1, 1, 1, m), lambda bi, ei: (bi, ei, 0, 0), memory_space=pltpu.SMEM),
                  pl.BlockSpec((1, 1, m, d), lambda bi, ei: (bi, ei, 0, 0))],
        out_specs=pl.BlockSpec((1, n, d), lambda bi, ei: (bi, 0, 0), pipeline_mode=pl.Buffered(1)),
        out_shape=jax.ShapeDtypeStruct((b, n, d), F32),
        compiler_params=_params("parallel", "arbitrary"),
        name="moe_scatter",
    )(idx.reshape(b, e, 1, m), y)


def _lambda_init(layer):
    return 0.8 - 0.6 * math.exp(-0.3 * layer)


def kernel(x, c, ctx, c_ctx, ada_w, ada_b, norm_mix, norm_ffn, final_norm_w,
           da_wqkv, da_wo, da_lam_q1, da_lam_k1, da_lam_q2, da_lam_k2, da_subln,
           ssm_w_in, ssm_conv_w, ssm_conv_b, ssm_dt_bias, ssm_A_log, ssm_D, ssm_norm, ssm_w_out,
           na_wqkv, na_wo, na_rpb, router_w, exp_w1, exp_w3, exp_w2):
    b, seq, d = x.shape
    n_ctx = ctx.shape[1]
    n = n_ctx + seq
    depth = ada_w.shape[0]
    n_exp = router_w.shape[-1]

    cond_rows = jnp.zeros((8, d), F32).at[:b].set(c).at[b].set(c_ctx)
    mods = ada_modulation(cond_rows, ada_w, ada_b).reshape(depth, 8, 6, d)
    mods = jnp.stack([jnp.broadcast_to(mods[:, b:b + 1], (depth, b, 6, d)), mods[:, :b]], axis=2)

    xj = jnp.concatenate([ctx, x], axis=1)
    rope_tabs = rope_tables(n_ctx, seq, DA_HEAD_DIM ** -0.5 * math.log2(math.e))

    ia = ib = ic = 0
    moe_delta = None
    for layer in range(depth):
        need_ctx = layer < depth - 1
        sh_m, sc_m, g_m, sh_f, sc_f, g_f = [mods[layer, :, :, k] for k in range(6)]
        if moe_delta is None:
            h = norm_mod(xj, norm_mix[layer], sh_m, sc_m, n_ctx)
        else:
            h, xj = norm_mod(xj, norm_mix[layer], sh_m, sc_m, n_ctx, delta=moe_delta)
        kind = layer % N_MIXERS
        if kind == 0:
            qt, k, vt = da_qkv(h, da_wqkv[ia], rope_tabs, DA_KV_CHUNK)
            lam_params = (da_lam_q1[ia], da_lam_k1[ia], da_lam_q2[ia], da_lam_k2[ia])
            li = _lambda_init(layer)
            o_l = diff_attention(qt, k, vt, lam_params, da_subln[ia], li, n_ctx, seq, n, DA_Q_TILE)
            o_c = diff_attention(qt, k, vt, lam_params, da_subln[ia], li, 0, n_ctx, n_ctx, n_ctx)
            o = jnp.concatenate([o_c, o_l], axis=1)
            xj = matmul(o, da_wo[ia].astype(BF16), F32, res=xj, gate=g_m, n_ctx=n_ctx)
            ia += 1
        elif kind == 1:
            nh = ssm_A_log.shape[-1]
            di = nh * SSM_HEAD_DIM
            conv_dim = ssm_conv_w.shape[-1]
            w_in = ssm_w_in[ib].astype(BF16)
            z = matmul(h, w_in[:, :di], F32, tn=1024)
            xbc = matmul(h, w_in[:, di:di + conv_dim], F32, tn=1024)
            w_dt = jnp.zeros((d, LANES), BF16).at[:, :2 * nh].set(w_in[:, di + conv_dim:])
            dt_raw = matmul(h, w_dt, F32)
            xbc = conv_silu(xbc, ssm_conv_w[ib], ssm_conv_b[ib], n_ctx)
            expand = (jnp.arange(di)[None, :] // SSM_HEAD_DIM == jnp.arange(nh)[:, None]).astype(BF16)
            ys = [ssd_scan(xbc, dt_raw, ssm_dt_bias[ib], ssm_A_log[ib], expand, dirn, nh, n_ctx)
                  for dirn in range(2)]
            d_chan = jnp.repeat(ssm_D[ib, 0] + ssm_D[ib, 1], SSM_HEAD_DIM).reshape(1, di)
            xj = ssm_out(ys[0], ys[1], xbc, z, d_chan, ssm_norm[ib], ssm_w_out[ib].astype(BF16),
                         xj, g_m, n_ctx)
            ib += 1
        else:
            qkv = matmul(h, na_wqkv[ic].astype(BF16), BF16, tn=1024)
            bias_tab = na_bias_table(na_rpb[ic], seq // GRID_W)
            o_l = neighbourhood_attention(qkv, bias_tab, n_ctx)
            o_c = context_attention(qkv, n_ctx)
            o = jnp.concatenate([o_c, o_l], axis=1)
            xj = matmul(o, na_wo[ic].astype(BF16), F32, res=xj, gate=g_m, n_ctx=n_ctx)
            ic += 1

        hp, aff, afft = norm_mod(xj, norm_ffn[layer], sh_f, sc_f, n_ctx, router_w=router_w[layer])
        sets = [(n_ctx, seq, EC_CAPACITY_FACTOR * seq // n_exp)]
        if need_ctx:
            sets.append((0, n_ctx, EC_CAPACITY_FACTOR * n_ctx // n_exp))
        idx = route(aff, sets)
        y = expert_ffn(idx, hp, afft, g_f, exp_w1, exp_w3, exp_w2, layer, sets[0][2])
        moe_delta = moe_scatter(idx, y, n)

    return final_norm(xj, moe_delta, final_norm_w, n_ctx)
```

```python
import functools
import math

import jax
import jax.numpy as jnp
import numpy as np
from jax import lax
from jax.experimental import pallas as pl
from jax.experimental.pallas import tpu as pltpu

F32 = jnp.float32
BF16 = jnp.bfloat16

GRID_W = 64
NORM_EPS = 1e-6
ROPE_BASE = 10000.0
DA_HEADS = 8
DA_HEAD_DIM = 64
DA_KV_CHUNK = 768
DA_GROUP_LANES = 256
DA_Q_TILE = 512
DA_SCORE_BUFFERS = 4
SSM_HEAD_DIM = 64
SSM_GROUPS = 4
SSM_STATE = 128
SSM_CHUNK = 128
NA_HEADS = 16
NA_ROWS = 8
NA_COLS = 16
N_MIXERS = 3
EC_CAPACITY_FACTOR = 2
MASK_VALUE = -1e30
ONE_BITS = 0x3F800000
ROUTE_PAD_COUNT = 1e9
LANES = 128
VMEM_LIMIT = 56 * 1024 * 1024


def _params(*sem):
    return pltpu.CompilerParams(dimension_semantics=sem, vmem_limit_bytes=VMEM_LIMIT)


def _split3(a):
    hi = a.astype(BF16)
    r1 = a - hi.astype(F32)
    mid = r1.astype(BF16)
    lo = (r1 - mid.astype(F32)).astype(BF16)
    return hi, mid, lo


def _dot(a, b):
    return jnp.dot(a, b, preferred_element_type=F32)


def _dot_nt(a, b):
    return lax.dot_general(a, b, (((1,), (1,)), ((), ())), preferred_element_type=F32)


def _dot_tn(a, b):
    return lax.dot_general(a, b, (((0,), (0,)), ((), ())), preferred_element_type=F32)


def _ada_kernel(c_ref, w_ref, b_ref, o_ref):
    c = c_ref[...]
    cond = c * jax.nn.sigmoid(c)
    o_ref[0] = jnp.dot(cond, w_ref[0], preferred_element_type=F32,
                       precision=lax.Precision.HIGHEST) + b_ref[0]


def ada_modulation(cond_rows, ada_w, ada_b):
    depth, d, n6 = ada_w.shape
    tn = 1536
    return pl.pallas_call(
        _ada_kernel,
        grid=(depth, n6 // tn),
        in_specs=[pl.BlockSpec((8, d), lambda l, j: (0, 0)),
                  pl.BlockSpec((1, d, tn), lambda l, j: (l, 0, j)),
                  pl.BlockSpec((1, 1, tn), lambda l, j: (l, 0, j))],
        out_specs=pl.BlockSpec((1, 8, tn), lambda l, j: (l, 0, j)),
        out_shape=jax.ShapeDtypeStruct((depth, 8, n6), F32),
        compiler_params=_params("parallel", "parallel"),
        name="ada_modulation",
    )(cond_rows, ada_w, ada_b.reshape(depth, 1, n6))


def _row_select(i, tm, n_ctx, vec2):
    row = i * tm + lax.broadcasted_iota(jnp.int32, (tm, 1), 0)
    return jnp.where(row >= n_ctx, vec2[1:2, :], vec2[0:1, :])


def _norm_mod_kernel(x_ref, w_ref, sh_ref, sc_ref, *rest, tm, n_ctx, n_exp, with_delta):
    i = pl.program_id(1)
    xf = x_ref[0]
    if with_delta:
        xf = xf + rest[0][0]
        rest = rest[1:]
    y = xf * lax.rsqrt(jnp.mean(xf * xf, axis=-1, keepdims=True) + NORM_EPS) * w_ref[...]
    sc = _row_select(i, tm, n_ctx, sc_ref[0])
    sh = _row_select(i, tm, n_ctx, sh_ref[0])
    h = y * (1.0 + sc) + sh
    if not n_exp:
        h_ref = rest[0]
        h_ref[0] = h.astype(h_ref.dtype)
        if with_delta:
            rest[1][0] = xf
        return
    rwt_ref, hp_ref, aff_ref, afft_ref = rest
    bits = lax.bitcast_convert_type(h.astype(BF16).astype(F32), jnp.uint32)
    dh = bits.shape[1] // 2
    hp_ref[0] = (bits[:, :dh] >> 16) | bits[:, dh:]
    logits = lax.dot_general(rwt_ref[...], h, (((1,), (1,)), ((), ())),
                             preferred_element_type=F32, precision=lax.Precision.HIGHEST)
    e = jnp.exp(logits - jnp.max(logits, axis=0, keepdims=True))
    aff = e / jnp.sum(e, axis=0, keepdims=True)
    aff_ref[0] = aff
    afft_ref[0] = jnp.concatenate([aff, jnp.zeros((LANES - n_exp, aff.shape[1]), F32)], axis=0).T


def norm_mod(x, w, shift, scale, n_ctx, router_w=None, delta=None, tm=256):
    b, n, d = x.shape
    n_exp = 0 if router_w is None else router_w.shape[1]
    with_delta = delta is not None
    assert not (n_exp and with_delta)
    row = pl.BlockSpec((1, tm, d), lambda bi, i: (bi, i, 0))
    in_specs = [row,
                pl.BlockSpec((1, d), lambda bi, i: (0, 0)),
                pl.BlockSpec((1, 2, d), lambda bi, i: (bi, 0, 0)),
                pl.BlockSpec((1, 2, d), lambda bi, i: (bi, 0, 0))]
    args = [x, w.reshape(1, d), shift, scale]
    if not n_exp:
        out_specs = [row]
        out_shape = [jax.ShapeDtypeStruct((b, n, d), BF16)]
        if with_delta:
            in_specs.append(row)
            args.append(delta)
            out_specs.append(row)
            out_shape.append(jax.ShapeDtypeStruct((b, n, d), F32))
    else:
        in_specs += [pl.BlockSpec((n_exp, d), lambda bi, i: (0, 0))]
        args += [router_w.T]
        out_specs = [pl.BlockSpec((1, tm, d // 2), lambda bi, i: (bi, i, 0)),
                     pl.BlockSpec((1, n_exp, tm), lambda bi, i: (bi, 0, i)),
                     pl.BlockSpec((1, tm, LANES), lambda bi, i: (bi, i, 0))]
        out_shape = [jax.ShapeDtypeStruct((b, n, d // 2), jnp.uint32),
                     jax.ShapeDtypeStruct((b, n_exp, n), F32),
                     jax.ShapeDtypeStruct((b, n, LANES), F32)]
    outs = pl.pallas_call(
        functools.partial(_norm_mod_kernel, tm=tm, n_ctx=n_ctx, n_exp=n_exp, with_delta=with_delta),
        grid=(b, n // tm),
        in_specs=in_specs, out_specs=out_specs, out_shape=out_shape,
        compiler_params=_params("parallel", "parallel"),
        name="norm_mod_router" if n_exp else "norm_mod",
    )(*args)
    return outs if (n_exp or with_delta) else outs[0]


def _final_norm_kernel(x_ref, d_ref, w_ref, o_ref):
    xf = x_ref[0] + d_ref[0]
    o_ref[0] = xf * lax.rsqrt(jnp.mean(xf * xf, axis=-1, keepdims=True) + NORM_EPS) * w_ref[...]


def final_norm(x, delta, w, n_ctx, tm=256):
    b, n, d = x.shape
    off = n_ctx // tm
    row = pl.BlockSpec((1, tm, d), lambda bi, i: (bi, i + off, 0))
    return pl.pallas_call(
        _final_norm_kernel,
        grid=(b, (n - n_ctx) // tm),
        in_specs=[row, row, pl.BlockSpec((1, d), lambda bi, i: (0, 0))],
        out_specs=pl.BlockSpec((1, tm, d), lambda bi, i: (bi, i, 0)),
        out_shape=jax.ShapeDtypeStruct((b, n - n_ctx, d), F32),
        compiler_params=_params("parallel", "parallel"),
        name="final_norm",
    )(x, delta, w.reshape(1, d))


def _matmul_kernel(a_ref, w_ref, *rest, tm, n_ctx, residual):
    if residual:
        res_ref, g_ref, o_ref = rest
    else:
        (o_ref,) = rest
    acc = _dot(a_ref[0], w_ref[...])
    if residual:
        g = _row_select(pl.program_id(1), tm, n_ctx, g_ref[0])
        acc = res_ref[0] + g * acc
    o_ref[0] = acc.astype(o_ref.dtype)


def matmul(a, w, out_dtype, res=None, gate=None, n_ctx=0, tm=768, tn=None):
    b, n, k = a.shape
    m = w.shape[1]
    tn = m if tn is None else tn
    residual = res is not None
    in_specs = [pl.BlockSpec((1, tm, k), lambda bi, i, j: (bi, i, 0)),
                pl.BlockSpec((k, tn), lambda bi, i, j: (0, j))]
    args = [a, w]
    if residual:
        in_specs += [pl.BlockSpec((1, tm, tn), lambda bi, i, j: (bi, i, j)),
                     pl.BlockSpec((1, 2, tn), lambda bi, i, j: (bi, 0, j))]
        args += [res, gate]
    return pl.pallas_call(
        functools.partial(_matmul_kernel, tm=tm, n_ctx=n_ctx, residual=residual),
        grid=(b, n // tm, m // tn),
        in_specs=in_specs,
        out_specs=pl.BlockSpec((1, tm, tn), lambda bi, i, j: (bi, i, j)),
        out_shape=jax.ShapeDtypeStruct((b, n, m), out_dtype),
        compiler_params=_params("parallel", "parallel", "arbitrary"),
        name="matmul_res" if residual else "matmul",
    )(*args)


def _rotate_pairs(x, axis):
    seg = lax.broadcasted_iota(jnp.int32, x.shape, axis) // 16
    n = x.shape[axis]
    return jnp.where(seg % 2 == 0, pltpu.roll(x, n - 16, axis), pltpu.roll(x, 16, axis))


def _da_qkv_kernel(h_ref, wqt_ref, wk_ref, wvt_ref, cq_ref, sq_ref, ck_ref, sk_ref,
                   qt_ref, k_ref, vt_ref, *, heads):
    h = h_ref[0]
    qt = _dot_nt(wqt_ref[...], h)
    cq, sq = cq_ref[...], sq_ref[...]
    for hd in range(heads):
        rs = slice(hd * LANES, (hd + 1) * LANES)
        x = qt[rs]
        qt_ref[0, rs, :] = (x * cq + _rotate_pairs(x, 0) * sq).astype(qt_ref.dtype)
    k = _dot(h, wk_ref[...])
    ck, sk = ck_ref[...], sk_ref[...]
    for hd in range(heads):
        cs = slice(hd * LANES, (hd + 1) * LANES)
        x = k[:, cs]
        k_ref[0, :, cs] = (x * ck + _rotate_pairs(x, 1) * sk).astype(k_ref.dtype)
    vt = _dot_nt(wvt_ref[...], h).astype(vt_ref.dtype)
    for hd in range(heads):
        vt_ref[0, 0, hd] = vt[hd * LANES:(hd + 1) * LANES]


def da_qkv(h, wqkv, tabs, tm):
    b, n, d = h.shape
    m = wqkv.shape[1] // 3
    cq_t, sq_t, ck, sk = tabs
    wqt = wqkv[:, :m].T.astype(BF16)
    wk = wqkv[:, m:2 * m].astype(BF16)
    wvt = wqkv[:, 2 * m:].T.astype(BF16)
    full = lambda r, c: pl.BlockSpec((r, c), lambda bi, i: (0, 0))
    return pl.pallas_call(
        functools.partial(_da_qkv_kernel, heads=m // LANES),
        grid=(b, n // tm),
        in_specs=[pl.BlockSpec((1, tm, d), lambda bi, i: (bi, i, 0)),
                  full(m, d), full(d, m), full(m, d),
                  pl.BlockSpec((LANES, tm), lambda bi, i: (0, i)),
                  pl.BlockSpec((LANES, tm), lambda bi, i: (0, i)),
                  pl.BlockSpec((tm, LANES), lambda bi, i: (i, 0)),
                  pl.BlockSpec((tm, LANES), lambda bi, i: (i, 0))],
        out_specs=[pl.BlockSpec((1, m, tm), lambda bi, i: (bi, 0, i)),
                   pl.BlockSpec((1, tm, m), lambda bi, i: (bi, i, 0)),
                   pl.BlockSpec((1, 1, m // LANES, LANES, tm), lambda bi, i: (bi, i, 0, 0, 0))],
        out_shape=[jax.ShapeDtypeStruct((b, m, n), BF16),
                   jax.ShapeDtypeStruct((b, n, m), BF16),
                   jax.ShapeDtypeStruct((b, n // tm, m // LANES, LANES, tm), BF16)],
        compiler_params=_params("parallel", "parallel"),
        name="da_qkv",
    )(h, wqt, wk, wvt, cq_t, sq_t, ck, sk)


def rope_tables(n_ctx, seq, q_scale):
    d = DA_HEAD_DIM
    t = jnp.arange(seq)
    rows = (t // GRID_W).astype(F32)
    cols = (t % GRID_W).astype(F32)
    quarter = d // 4
    freqs = ROPE_BASE ** (-jnp.arange(quarter, dtype=F32) / quarter)
    ang_r = rows[:, None] * freqs[None, :]
    ang_c = cols[:, None] * freqs[None, :]
    ang = jnp.concatenate([ang_r, ang_r, ang_c, ang_c], axis=-1)
    cos = jnp.cos(ang)
    sign = jnp.concatenate([-jnp.ones((quarter,), F32), jnp.ones((quarter,), F32)] * 2)
    sin = jnp.sin(ang) * sign[None, :]
    cos = jnp.concatenate([jnp.ones((n_ctx, d), F32), cos], axis=0)
    sin = jnp.concatenate([jnp.zeros((n_ctx, d), F32), sin], axis=0)
    cos = jnp.concatenate([cos, cos], axis=-1)
    sin = jnp.concatenate([sin, sin], axis=-1)
    return (cos * q_scale).T, (sin * q_scale).T, cos, sin


def _diff_attn_kernel(*refs, nq, tk, kw, nchunks, lam_init):
    qt_refs = refs[:nq]
    k_ref, vt_ref, lq1_ref, lk1_ref, lq2_ref, lk2_ref, sub_ref, o_ref, acc_ref, s_ref = refs[nq:]
    gw = DA_GROUP_LANES
    ngroups = 2 * nq
    sub = lax.broadcasted_iota(jnp.int32, (LANES, gw), 0)
    zero = jnp.zeros((LANES, gw), qt_refs[0].dtype)
    qg = []
    for g in range(ngroups):
        q = qt_refs[g // 2][0]
        qg.append(jnp.where((sub < DA_HEAD_DIM) if g % 2 == 0 else (sub >= DA_HEAD_DIM), q, zero))
    acc_ref[...] = jnp.zeros(acc_ref.shape, F32)
    half = kw // 2

    def col_reduce(x, op):
        return op(op(x.reshape(8, x.shape[0] // 8, x.shape[1]), axis=0), axis=0, keepdims=True)

    def scores(t):
        c, g = divmod(t, ngroups)
        mx = None
        for hf in range(2):
            k = k_ref[0, c * tk + hf * half:c * tk + (hf + 1) * half, :]
            s = _dot(k, qg[g])
            s_ref[t % DA_SCORE_BUFFERS, hf * half:(hf + 1) * half, :] = s
            hm = col_reduce(s, jnp.max)
            mx = hm if mx is None else jnp.maximum(mx, hm)
        return mx

    def softmax_pv(t, mx, ml):
        c, g = divmod(t, ngroups)
        m_old, l_old = ml
        vt = vt_ref[0, c, 0]
        m_new = jnp.maximum(m_old, mx)
        alpha = jnp.exp2(m_old - m_new)
        p = jnp.exp2(s_ref[t % DA_SCORE_BUFFERS] - m_new)
        acc_ref[g] = alpha * acc_ref[g] + _dot(vt[:, :kw], p.astype(BF16))
        return m_new, alpha * l_old + col_reduce(p, jnp.sum)

    nsteps = nchunks * ngroups
    ahead = DA_SCORE_BUFFERS - 1
    carry = [(jnp.full((1, gw), -jnp.inf, F32), jnp.zeros((1, gw), F32))] * ngroups
    maxima = {t: scores(t) for t in range(min(ahead, nsteps))}
    for t in range(nsteps):
        if t + ahead < nsteps:
            maxima[t + ahead] = scores(t + ahead)
        carry[t % ngroups] = softmax_pv(t, maxima.pop(t), carry[t % ngroups])
    lam = (jnp.exp(jnp.sum(lq1_ref[...] * lk1_ref[...], axis=-1, keepdims=True))
           - jnp.exp(jnp.sum(lq2_ref[...] * lk2_ref[...], axis=-1, keepdims=True)) + lam_init)
    for qh in range(ngroups // 2):
        od = acc_ref[2 * qh] / carry[2 * qh][1] - lam * (acc_ref[2 * qh + 1] / carry[2 * qh + 1][1])
        y = od * lax.rsqrt(jnp.mean(od * od, axis=0, keepdims=True) + NORM_EPS) * sub_ref[...]
        o_ref[0, qh * gw:(qh + 1) * gw, :] = (y * (1.0 - lam_init)).T.astype(o_ref.dtype)


def diff_attention(qt, k, vt, lam_params, subln, lam_init, q_start, n_q, n_k, tq):
    b, m, n = qt.shape
    h = m // LANES
    tk = vt.shape[-1]
    kw = min(tk, n_k)
    nchunks = n_k // kw
    gw = DA_GROUP_LANES
    nq = tq // gw
    assert q_start % gw == 0 and tq % gw == 0 and n_q % tq == 0
    qoff = q_start // gw
    vec = lambda a: a.reshape(1, -1).astype(F32)
    small = pl.BlockSpec((1, DA_HEAD_DIM), lambda bi, hi, i: (0, 0))
    q_specs = [pl.BlockSpec((1, LANES, gw), functools.partial(
        lambda bi, hi, i, j: (bi, hi, qoff + nq * i + j), j=j)) for j in range(nq)]
    return pl.pallas_call(
        functools.partial(_diff_attn_kernel, nq=nq, tk=tk, kw=kw, nchunks=nchunks, lam_init=lam_init),
        grid=(b, h, n_q // tq),
        in_specs=q_specs + [
                  pl.BlockSpec((1, n_k, LANES), lambda bi, hi, i: (bi, 0, hi)),
                  pl.BlockSpec((1, nchunks, 1, vt.shape[3], tk), lambda bi, hi, i: (bi, 0, hi, 0, 0)),
                  small, small, small, small,
                  pl.BlockSpec((LANES, 1), lambda bi, hi, i: (0, 0))],
        out_specs=pl.BlockSpec((1, tq, LANES), lambda bi, hi, i: (bi, i, hi)),
        out_shape=jax.ShapeDtypeStruct((b, n_q, m), BF16),
        scratch_shapes=[pltpu.VMEM((2 * tq // DA_GROUP_LANES, LANES, DA_GROUP_LANES), F32),
                        pltpu.VMEM((DA_SCORE_BUFFERS, kw, DA_GROUP_LANES), F32)],
        compiler_params=_params("parallel", "parallel", "arbitrary"),
        name="diff_attention",
    )(*([qt] * nq), k, vt, *[vec(p) for p in lam_params], subln.reshape(-1, 1).astype(F32))


def _pair_scores(qp, kp):
    lane = lax.broadcasted_iota(jnp.int32, qp.shape, 1)
    zero = jnp.zeros_like(qp)
    qs = jnp.concatenate([jnp.where(lane < 64, qp, zero), jnp.where(lane >= 64, qp, zero)], axis=0)
    return _dot_nt(qs, kp)


def _pair_softmax_pv(s, vp, bias, n_bias):
    tq = s.shape[0] // 2
    if n_bias:
        s_w = s[:, :n_bias] + bias
        s_c = s[:, n_bias:]
        m = jnp.maximum(jnp.max(s_w, axis=-1, keepdims=True), jnp.max(s_c, axis=-1, keepdims=True))
        p_w = jnp.exp(s_w - m)
        p_c = jnp.exp(s_c - m)
        l = jnp.sum(p_w, axis=-1, keepdims=True) + jnp.sum(p_c, axis=-1, keepdims=True)
        pv = _dot(p_w.astype(BF16), vp[:n_bias]) + _dot(p_c.astype(BF16), vp[n_bias:])
    else:
        m = jnp.max(s, axis=-1, keepdims=True)
        p = jnp.exp(s - m)
        l = jnp.sum(p, axis=-1, keepdims=True)
        pv = _dot(p.astype(BF16), vp)
    o = pv / l
    lane_o = lax.broadcasted_iota(jnp.int32, (tq, LANES), 1)
    return jnp.where(lane_o < 64, o[:tq], o[tq:])


def _pairs_attention(q_of, k_of, v_of, bias_of, n_bias, o_ref):
    npairs = NA_HEADS // 2
    s_next = _pair_scores(q_of(0), k_of(0))
    for hp in range(npairs):
        s_cur = s_next
        if hp + 1 < npairs:
            s_next = _pair_scores(q_of(hp + 1), k_of(hp + 1))
        o = _pair_softmax_pv(s_cur, v_of(hp), bias_of(hp), n_bias)
        o_ref[0, :, hp * LANES:(hp + 1) * LANES] = o.astype(o_ref.dtype)


def _na_kernel(*refs, wr, n_ctx, scale):
    q_ref = refs[0]
    k_refs = refs[1:1 + wr]
    v_refs = refs[1 + wr:1 + 2 * wr]
    kc_ref, vc_ref, bias_ref, o_ref, kbuf, vbuf = refs[1 + 2 * wr:]
    nw = wr * GRID_W
    for w in range(wr):
        kbuf[w * GRID_W:(w + 1) * GRID_W, :] = k_refs[w][0]
        vbuf[w * GRID_W:(w + 1) * GRID_W, :] = v_refs[w][0]
    kbuf[nw:nw + n_ctx, :] = kc_ref[0]
    vbuf[nw:nw + n_ctx, :] = vc_ref[0]
    cols = lambda hp: slice(hp * LANES, (hp + 1) * LANES)
    _pairs_attention(lambda hp: q_ref[0, :, cols(hp)] * scale,
                     lambda hp: kbuf[:, cols(hp)],
                     lambda hp: vbuf[:, cols(hp)],
                     lambda hp: bias_ref[0, 2 * hp:2 * hp + 2].reshape(2 * GRID_W, nw),
                     nw, o_ref)


def na_bias_table(rpb, rows):
    wr = min(NA_ROWS, rows)
    qcol = np.arange(GRID_W)
    col_start = np.clip(qcol - NA_COLS // 2, 0, GRID_W - NA_COLS)
    kc = np.arange(GRID_W)
    inside = (kc[None, :] >= col_start[:, None]) & (kc[None, :] < col_start[:, None] + NA_COLS)
    dc = np.clip(kc[None, :] - qcol[:, None] + NA_COLS - 1, 0, 2 * NA_COLS - 2)
    dr = np.arange(wr)[None, :] - np.arange(wr)[:, None] + NA_ROWS - 1
    sel_r = (dr[:, :, None] == np.arange(2 * NA_ROWS - 1)).astype(np.float32)
    sel_c = (dc[:, :, None] == np.arange(2 * NA_COLS - 1)).astype(np.float32)
    tab = jnp.einsum("cwa,hab,qkb->chqwk", sel_r, rpb.astype(F32), sel_c,
                     precision=lax.Precision.HIGHEST)
    tab = jnp.where(inside[None, None, :, None, :], tab, MASK_VALUE)
    return tab.reshape(wr, rpb.shape[0], GRID_W, wr * GRID_W)


def neighbourhood_attention(qkv, bias_tab, n_ctx):
    b, n, m3 = qkv.shape
    m = m3 // 3
    rows = (n - n_ctx) // GRID_W
    wr = min(NA_ROWS, rows)
    cb = n_ctx // GRID_W
    rs = lambda r: jnp.clip(r - wr // 2, 0, rows - wr)
    blk = lambda f: pl.BlockSpec((1, GRID_W, m), f)
    in_specs = [blk(lambda bi, r: (bi, cb + r, 0))]
    in_specs += [blk(functools.partial(lambda bi, r, w: (bi, cb + rs(r) + w, 1), w=w)) for w in range(wr)]
    in_specs += [blk(functools.partial(lambda bi, r, w: (bi, cb + rs(r) + w, 2), w=w)) for w in range(wr)]
    in_specs += [pl.BlockSpec((1, n_ctx, m), lambda bi, r: (bi, 0, 1)),
                 pl.BlockSpec((1, n_ctx, m), lambda bi, r: (bi, 0, 2)),
                 pl.BlockSpec((1, NA_HEADS, GRID_W, wr * GRID_W), lambda bi, r: (r - rs(r), 0, 0, 0))]
    nk = wr * GRID_W + n_ctx
    return pl.pallas_call(
        functools.partial(_na_kernel, wr=wr, n_ctx=n_ctx, scale=(m // NA_HEADS) ** -0.5),
        grid=(b, rows),
        in_specs=in_specs,
        out_specs=pl.BlockSpec((1, GRID_W, m), lambda bi, r: (bi, r, 0)),
        out_shape=jax.ShapeDtypeStruct((b, n - n_ctx, m), BF16),
        scratch_shapes=[pltpu.VMEM((nk, m), BF16), pltpu.VMEM((nk, m), BF16)],
        compiler_params=_params("parallel", "arbitrary"),
        name="neighbourhood_attention",
    )(*([qkv] * (1 + 2 * wr + 2)), bias_tab)


def _ctx_attn_kernel(q_ref, k_ref, v_ref, o_ref, *, scale):
    cols = lambda hp: slice(hp * LANES, (hp + 1) * LANES)
    _pairs_attention(lambda hp: q_ref[0, :, cols(hp)] * scale,
                     lambda hp: k_ref[0, :, cols(hp)],
                     lambda hp: v_ref[0, :, cols(hp)],
                     lambda hp: None, 0, o_ref)


def context_attention(qkv, n_ctx):
    b, n, m3 = qkv.shape
    m = m3 // 3
    return pl.pallas_call(
        functools.partial(_ctx_attn_kernel, scale=(m // NA_HEADS) ** -0.5),
        grid=(b,),
        in_specs=[pl.BlockSpec((1, n_ctx, m), lambda bi: (bi, 0, 0)),
                  pl.BlockSpec((1, n_ctx, m), lambda bi: (bi, 0, 1)),
                  pl.BlockSpec((1, n_ctx, m), lambda bi: (bi, 0, 2))],
        out_specs=pl.BlockSpec((1, n_ctx, m), lambda bi: (bi, 0, 0)),
        out_shape=jax.ShapeDtypeStruct((b, n_ctx, m), BF16),
        compiler_params=_params("parallel"),
        name="context_attention",
    )(qkv, qkv, qkv)


def _conv_silu_kernel(x_ref, w_ref, b_ref, o_ref, *, n, n_ctx):
    x = x_ref[0]
    row = lax.broadcasted_iota(jnp.int32, x.shape, 0)
    xm = jnp.where((row == 0) | (row == n_ctx), 0.0, pltpu.roll(x, 1, 0))
    xp = jnp.where((row == n_ctx - 1) | (row == n - 1), 0.0, pltpu.roll(x, n - 1, 0))
    y = w_ref[0:1, :] * xm + w_ref[1:2, :] * x + w_ref[2:3, :] * xp + b_ref[...]
    o_ref[0] = y * jax.nn.sigmoid(y)


def conv_silu(x, w, bias, n_ctx):
    b, n, c = x.shape
    return pl.pallas_call(
        functools.partial(_conv_silu_kernel, n=n, n_ctx=n_ctx),
        grid=(b, c // LANES),
        in_specs=[pl.BlockSpec((1, n, LANES), lambda bi, j: (bi, 0, j)),
                  pl.BlockSpec((3, LANES), lambda bi, j: (0, j)),
                  pl.BlockSpec((1, LANES), lambda bi, j: (0, j))],
        out_specs=pl.BlockSpec((1, n, LANES), lambda bi, j: (bi, 0, j)),
        out_shape=jax.ShapeDtypeStruct((b, n, c), F32),
        compiler_params=_params("parallel", "parallel"),
        name="conv_silu",
    )(x, w, bias.reshape(1, c))


def _softplus(x):
    return jnp.maximum(x, 0.0) + jnp.log1p(jnp.exp(-jnp.abs(x)))


def _ssd_kernel(x_ref, b_ref, c_ref, dt_ref, dtb_ref, alog_ref, e_ref, y_ref, state_ref,
                *, direction, nheads):
    L = SSM_CHUNK
    P = SSM_HEAD_DIM
    G = SSM_GROUPS
    hpg = nheads // G
    reverse = direction == 1

    @pl.when(pl.program_id(1) == 0)
    def _():
        state_ref[...] = jnp.zeros(state_ref.shape, F32)

    li = lax.broadcasted_iota(jnp.int32, (L, L), 0)
    si = lax.broadcasted_iota(jnp.int32, (L, L), 1)
    allowed = (si >= li) if reverse else (si <= li)
    tri = jnp.where(allowed, 1.0, 0.0).astype(BF16)
    tri_t = jnp.where((li >= si) if reverse else (li <= si), 1.0, 0.0).astype(BF16)

    dt_raw = dt_ref[0][:, direction * nheads:(direction + 1) * nheads]
    dt = _softplus(dt_raw + dtb_ref[...])
    a = dt * (-jnp.exp(alog_ref[...]))
    a3 = _split3(a)
    acs = sum(_dot(tri, p) for p in a3)
    acs_t = sum(_dot_tn(p, tri_t) for p in a3)
    expand = e_ref[...]
    acs_x = sum(_dot(p, expand) for p in _split3(acs))
    dt_x = sum(_dot(p, expand) for p in _split3(dt))
    end = 0 if reverse else L - 1
    total_x = acs_x[end:end + 1, :]

    xd = x_ref[0] * dt_x
    xd_b = xd.astype(BF16)
    x_state = (xd * jnp.exp(total_x - acs_x)).astype(BF16)
    exp_acs_x = jnp.exp(acs_x)
    chunk_decay_x = jnp.exp(total_x)
    lane = lax.broadcasted_iota(jnp.int32, (L, LANES), 1)

    for g in range(G):
        gs = slice(g * SSM_STATE, (g + 1) * SSM_STATE)
        hs = slice(g * hpg * P, (g + 1) * hpg * P)
        cg = c_ref[0][:, gs].astype(BF16)
        bg = b_ref[0][:, gs].astype(BF16)
        cb = _dot_nt(cg, bg)
        h_prev = state_ref[:, hs]
        y_off = _dot(cg, h_prev.astype(BF16)) * exp_acs_x[:, hs]
        state_ref[:, hs] = h_prev * chunk_decay_x[:, hs] + _dot_tn(bg, x_state[:, hs])
        for rp in range(hpg // 2):
            mats = []
            for r in (2 * rp, 2 * rp + 1):
                hidx = g * hpg + r
                seg = acs[:, hidx:hidx + 1] - acs_t[hidx:hidx + 1, :]
                mats.append((cb * jnp.exp(jnp.where(allowed, seg, -jnp.inf))).astype(BF16))
            ps = slice((g * hpg + 2 * rp) * P, (g * hpg + 2 * rp + 2) * P)
            yd = _dot(jnp.concatenate(mats, axis=0), xd_b[:, ps])
            y_ref[0, :, ps] = jnp.where(lane < P, yd[:L], yd[L:]) + y_off[:, 2 * rp * P:(2 * rp + 2) * P]


def ssd_scan(xbc, dt_raw, dt_bias, a_log, expand, direction, nheads, n_ctx):
    b, n, _ = xbc.shape
    d_inner = nheads * SSM_HEAD_DIM
    gn = SSM_GROUPS * SSM_STATE
    L = SSM_CHUNK
    nc = n // L
    ncc = n_ctx // L
    if direction == 0:
        cidx = lambda c: c
    else:
        cidx = lambda c: jnp.where(c < ncc, ncc - 1 - c, nc - 1 - (c - ncc))
    return pl.pallas_call(
        functools.partial(_ssd_kernel, direction=direction, nheads=nheads),
        grid=(b, nc),
        in_specs=[pl.BlockSpec((1, L, d_inner), lambda bi, c: (bi, cidx(c), 0)),
                  pl.BlockSpec((1, L, gn), lambda bi, c: (bi, cidx(c), d_inner // gn)),
                  pl.BlockSpec((1, L, gn), lambda bi, c: (bi, cidx(c), d_inner // gn + 1)),
                  pl.BlockSpec((1, L, LANES), lambda bi, c: (bi, cidx(c), 0)),
                  pl.BlockSpec((1, nheads), lambda bi, c: (0, 0)),
                  pl.BlockSpec((1, nheads), lambda bi, c: (0, 0)),
                  pl.BlockSpec((nheads, d_inner), lambda bi, c: (0, 0))],
        out_specs=pl.BlockSpec((1, L, d_inner), lambda bi, c: (bi, cidx(c), 0)),
        out_shape=jax.ShapeDtypeStruct((b, n, d_inner), F32),
        scratch_shapes=[pltpu.VMEM((SSM_STATE, d_inner), F32)],
        compiler_params=_params("parallel", "arbitrary"),
        name="ssd_scan_%d" % direction,
    )(xbc, xbc, xbc, dt_raw, dt_bias[direction:direction + 1], a_log[direction:direction + 1], expand)


def _ssm_out_kernel(y0_ref, y1_ref, x_ref, z_ref, d_ref, nw_ref, w_ref, res_ref, g_ref, o_ref,
                    *, tm, n_ctx):
    z = z_ref[0]
    u = (y0_ref[0] + y1_ref[0] + d_ref[...] * x_ref[0]) * (z * jax.nn.sigmoid(z))
    un = u * lax.rsqrt(jnp.mean(u * u, axis=-1, keepdims=True) + NORM_EPS) * nw_ref[...]
    g = _row_select(pl.program_id(1), tm, n_ctx, g_ref[0])
    o_ref[0] = res_ref[0] + g * _dot(un.astype(BF16), w_ref[...])


def ssm_out(y0, y1, xbc, z, d_chan, norm_w, w_out, res, gate, n_ctx, tm=256):
    b, n, di = y0.shape
    d = w_out.shape[1]
    row = lambda c: pl.BlockSpec((1, tm, c), lambda bi, i: (bi, i, 0))
    vec = lambda c: pl.BlockSpec((1, c), lambda bi, i: (0, 0))
    return pl.pallas_call(
        functools.partial(_ssm_out_kernel, tm=tm, n_ctx=n_ctx),
        grid=(b, n // tm),
        in_specs=[row(di), row(di), row(di), row(di), vec(di), vec(di),
                  pl.BlockSpec((di, d), lambda bi, i: (0, 0)), row(d),
                  pl.BlockSpec((1, 2, d), lambda bi, i: (bi, 0, 0))],
        out_specs=row(d),
        out_shape=jax.ShapeDtypeStruct((b, n, d), F32),
        compiler_params=_params("parallel", "parallel"),
        name="ssm_out",
    )(y0, y1, xbc, z, d_chan, norm_w.reshape(1, di), w_out, res, gate)


def _cumsum_lanes(x):
    n = x.shape[-1]
    lane = lax.broadcasted_iota(jnp.int32, x.shape, x.ndim - 1)
    s = 1
    while s < n:
        x = x + jnp.where(lane >= s, pltpu.roll(x, s, x.ndim - 1), 0.0)
        s *= 2
    return x


def _route_set(aff, cap):
    e, t = aff.shape
    bits = lax.bitcast_convert_type(aff, jnp.int32)

    def count_ge(thr):
        return jnp.sum(jnp.where(bits >= thr, 1.0, 0.0), axis=1, keepdims=True)

    def step(_, lohi):
        lo, hi = lohi
        mid = lo + (hi - lo) // 2
        ok = count_ge(mid) >= cap
        return jnp.where(ok, mid, lo), jnp.where(ok, hi, mid)

    lo0 = jnp.zeros((e, 1), jnp.int32)
    hi0 = jnp.full((e, 1), ONE_BITS + 1, jnp.int32)
    thr, _ = lax.fori_loop(0, 31, step, (lo0, hi0))
    gt = jnp.where(bits > thr, 1.0, 0.0)
    eq = jnp.where(bits == thr, 1.0, 0.0)
    need = cap - jnp.sum(gt, axis=1, keepdims=True)
    sel = gt + eq * jnp.where(_cumsum_lanes(eq) <= need, 1.0, 0.0)
    return sel, _cumsum_lanes(sel)


def _route_kernel(aff_ref, idx_ref, cnt_ref, tile_ref, pos_ref, *, sets):
    n_exp = aff_ref.shape[1]
    for t0, t, cap, col0 in sets:
        _, cnt = _route_set(aff_ref[0, :, t0:t0 + t], cap)
        cnt_ref[:, :t] = cnt
        nt = t // LANES
        ntp = max(nt, 16)
        jp = -(-cap // LANES) * LANES
        tile_ref[...] = jnp.full(tile_ref.shape, ROUTE_PAD_COUNT, F32)
        jrow = lax.broadcasted_iota(jnp.int32, (1, jp), 1).astype(F32)
        ksub = lax.broadcasted_iota(jnp.int32, (ntp, jp), 0)

        def per_expert(ei, c):
            row = cnt_ref[pl.ds(ei, 1), :t]
            for k in range(nt):
                tile_ref[k:k + 1, :] = row[:, k * LANES:(k + 1) * LANES]
            tiles = tile_ref[:ntp]
            full = jnp.sum(jnp.where(tiles[:, LANES - 1:] <= jrow, 1.0, 0.0), axis=0, keepdims=True)
            onehot = jnp.where(ksub == full.astype(jnp.int32), 1.0, 0.0).astype(BF16)
            bound = sum(_dot_tn(p, onehot) for p in _split3(tiles))
            inside = jnp.sum(jnp.where(bound <= jrow, 1.0, 0.0), axis=0, keepdims=True)
            pos = (LANES * full + inside).astype(jnp.int32) + t0
            pad = pos_ref.shape[1] - jp
            if pad:
                pos = jnp.concatenate([pos, jnp.zeros((1, pad), jnp.int32)], axis=1)
            pos_ref[pl.ds(ei, 1), :] = pos
            return c

        lax.fori_loop(0, n_exp, per_expert, 0)
        idx_ref[0, :, col0:col0 + cap] = pos_ref[:, :cap]


def route(aff, sets):
    b, e, n = aff.shape
    m = sum(s[2] for s in sets)
    tmax = max(s[1] for s in sets)
    cols, sets_r = 0, []
    for t0, t, cap in sets:
        sets_r.append((t0, t, cap, cols))
        cols += cap
    return pl.pallas_call(
        functools.partial(_route_kernel, sets=tuple(sets_r)),
        grid=(b,),
        in_specs=[pl.BlockSpec((1, e, n), lambda bi: (bi, 0, 0))],
        out_specs=pl.BlockSpec((1, e, m), lambda bi: (bi, 0, 0)),
        out_shape=jax.ShapeDtypeStruct((b, e, m), jnp.int32),
        scratch_shapes=[pltpu.VMEM((e, tmax), F32), pltpu.VMEM((max(tmax // LANES, 16), LANES), F32),
                        pltpu.VMEM((e, -(-max(s[2] for s in sets) // LANES) * LANES), jnp.int32)],
        compiler_params=_params("parallel"),
        name="route",
    )(aff)


def _expert_ffn_kernel(idx_ref, idxn_ref, hp_ref, afft_ref, g_ref, w1_ref, w3_ref, w2_ref, o_ref,
                       xw_ref, gw_ref, xs_ref, gate_ref, *, m, n_lat, rows_per_step):
    ei = pl.program_id(1)
    f = pl.program_id(2)
    slot = ei % 2

    def copy_row(src_idx_ref, dst_slot, j):
        i = src_idx_ref[0, 0, 0, j]
        xw_ref[dst_slot, pl.ds(j, 1), :] = hp_ref[0, pl.ds(i, 1), :]
        gw_ref[dst_slot, pl.ds(j, 1), :] = afft_ref[0, pl.ds(i, 1), :]

    @pl.when((f == 0) & (ei == 0))
    def _():
        def gather(g, c):
            base = pl.multiple_of(g * 8, 8)
            for r in range(8):
                copy_row(idx_ref, 0, base + r)
            return c

        lax.fori_loop(0, m // 8, gather, 0)

    @pl.when(f == 0)
    def _():
        w = xw_ref[slot]
        lo = lax.bitcast_convert_type(w << 16, F32).astype(BF16)
        hi = lax.bitcast_convert_type(w & jnp.uint32(0xFFFF0000), F32).astype(BF16)
        xs_ref[...] = jnp.concatenate([lo, hi], axis=1)
        lane = lax.broadcasted_iota(jnp.int32, gate_ref.shape, 1)
        gate = jnp.sum(jnp.where(lane == ei, gw_ref[slot], 0.0), axis=1, keepdims=True)
        gate_ref[...] = jnp.broadcast_to(gate, gate_ref.shape)
        o_ref[...] = jnp.zeros(o_ref.shape, F32)

    for r in range(rows_per_step):
        copy_row(idxn_ref, 1 - slot, f * rows_per_step + r)

    x = xs_ref[...]
    h1 = _dot(x, w1_ref[0].astype(BF16))
    h3 = _dot(x, w3_ref[0].astype(BF16))
    hid = (h1 * jax.nn.sigmoid(h1) * h3).astype(BF16)
    part = _dot(hid, w2_ref[0].astype(BF16))

    o_ref[0, 0] = o_ref[0, 0] + part

    @pl.when(f == pl.num_programs(2) - 1)
    def _():
        row = lax.broadcasted_iota(jnp.int32, (m, 1), 0)
        g = jnp.where(row < n_lat, g_ref[0, 1:2, :], g_ref[0, 0:1, :])
        o_ref[0, 0] = o_ref[0, 0] * gate_ref[:, 0:1] * g


def expert_ffn(idx, hp, afft, gate2, w1, w3, w2, layer, n_lat, tf=256):
    b, e, m = idx.shape
    n, dh = hp.shape[1:]
    d = 2 * dh
    ff = w1.shape[3]
    single = dict(pipeline_mode=pl.Buffered(1))
    nf = ff // tf
    assert m % nf == 0
    idx4 = idx.reshape(b, e, 1, m)
    return pl.pallas_call(
        functools.partial(_expert_ffn_kernel, m=m, n_lat=n_lat, rows_per_step=m // nf),
        grid=(b, e, nf),
        in_specs=[pl.BlockSpec((1, 1, 1, m), lambda bi, ei, f: (bi, ei, 0, 0), memory_space=pltpu.SMEM),
                  pl.BlockSpec((1, 1, 1, m), lambda bi, ei, f: (bi, jnp.minimum(ei + 1, e - 1), 0, 0),
                               memory_space=pltpu.SMEM),
                  pl.BlockSpec((1, n, dh), lambda bi, ei, f: (bi, 0, 0), **single),
                  pl.BlockSpec((1, n, LANES), lambda bi, ei, f: (bi, 0, 0), **single),
                  pl.BlockSpec((1, 2, d), lambda bi, ei, f: (bi, 0, 0)),
                  pl.BlockSpec((None, 1, d, tf), lambda bi, ei, f: (layer, ei, 0, f)),
                  pl.BlockSpec((None, 1, d, tf), lambda bi, ei, f: (layer, ei, 0, f)),
                  pl.BlockSpec((None, 1, tf, d), lambda bi, ei, f: (layer, ei, f, 0))],
        out_specs=pl.BlockSpec((1, 1, m, d), lambda bi, ei, f: (bi, ei, 0, 0)),
        out_shape=jax.ShapeDtypeStruct((b, e, m, d), F32),
        scratch_shapes=[pltpu.VMEM((2, m, dh), jnp.uint32), pltpu.VMEM((2, m, LANES), F32),
                        pltpu.VMEM((m, d), BF16), pltpu.VMEM((m, LANES), F32)],
        compiler_params=_params("arbitrary", "arbitrary", "arbitrary"),
        name="expert_ffn",
    )(idx4, idx4, hp, afft, gate2, w1, w3, w2)


def _scatter_kernel(idx_ref, y_ref, o_ref, *, m):
    ei = pl.program_id(1)

    @pl.when(ei == 0)
    def _():
        o_ref[...] = jnp.zeros(o_ref.shape, F32)

    def add_rows(g, c):
        base = pl.multiple_of(g * 8, 8)
        rows = [idx_ref[0, 0, 0, base + r] for r in range(8)]
        sums = [o_ref[0, pl.ds(i, 1), :] + y_ref[0, 0, pl.ds(base + r, 1), :] for r, i in enumerate(rows)]
        for i, v in zip(rows, sums):
            o_ref[0, pl.ds(i, 1), :] = v
        return c

    lax.fori_loop(0, m // 8, add_rows, 0)


def moe_scatter(idx, y, n):
    b, e, m = idx.shape
    d = y.shape[-1]
    return pl.pallas_call(
        functools.partial(_scatter_kernel, m=m),
        grid=(b, e),
        in_specs=[pl.BlockSpec((1, 1, 1, m), lambda bi, ei: (bi, ei, 0, 0), memory_space=pltpu.SMEM),
                  pl.BlockSpec((1, 1, m, d), lambda bi, ei: (bi, ei, 0, 0))],
        out_specs=pl.BlockSpec((1, n, d), lambda bi, ei: (bi, 0, 0), pipeline_mode=pl.Buffered(1)),
        out_shape=jax.ShapeDtypeStruct((b, n, d), F32),
        compiler_params=_params("parallel", "arbitrary"),
        name="moe_scatter",
    )(idx.reshape(b, e, 1, m), y)


def _lambda_init(layer):
    return 0.8 - 0.6 * math.exp(-0.3 * layer)


def kernel(x, c, ctx, c_ctx, ada_w, ada_b, norm_mix, norm_ffn, final_norm_w,
           da_wqkv, da_wo, da_lam_q1, da_lam_k1, da_lam_q2, da_lam_k2, da_subln,
           ssm_w_in, ssm_conv_w, ssm_conv_b, ssm_dt_bias, ssm_A_log, ssm_D, ssm_norm, ssm_w_out,
           na_wqkv, na_wo, na_rpb, router_w, exp_w1, exp_w3, exp_w2):
    b, seq, d = x.shape
    n_ctx = ctx.shape[1]
    n = n_ctx + seq
    depth = ada_w.shape[0]
    n_exp = router_w.shape[-1]

    cond_rows = jnp.zeros((8, d), F32).at[:b].set(c).at[b].set(c_ctx)
    mods = ada_modulation(cond_rows, ada_w, ada_b).reshape(depth, 8, 6, d)
    mods = jnp.stack([jnp.broadcast_to(mods[:, b:b + 1], (depth, b, 6, d)), mods[:, :b]], axis=2)

    xj = jnp.concatenate([ctx, x], axis=1)
    rope_tabs = rope_tables(n_ctx, seq, DA_HEAD_DIM ** -0.5 * math.log2(math.e))

    ia = ib = ic = 0
    moe_delta = None
    for layer in range(depth):
        need_ctx = layer < depth - 1
        sh_m, sc_m, g_m, sh_f, sc_f, g_f = [mods[layer, :, :, k] for k in range(6)]
        if moe_delta is None:
            h = norm_mod(xj, norm_mix[layer], sh_m, sc_m, n_ctx)
        else:
            h, xj = norm_mod(xj, norm_mix[layer], sh_m, sc_m, n_ctx, delta=moe_delta)
        kind = layer % N_MIXERS
        if kind == 0:
            qt, k, vt = da_qkv(h, da_wqkv[ia], rope_tabs, DA_KV_CHUNK)
            lam_params = (da_lam_q1[ia], da_lam_k1[ia], da_lam_q2[ia], da_lam_k2[ia])
            li = _lambda_init(layer)
            o_l = diff_attention(qt, k, vt, lam_params, da_subln[ia], li, n_ctx, seq, n, DA_Q_TILE)
            o_c = diff_attention(qt, k, vt, lam_params, da_subln[ia], li, 0, n_ctx, n_ctx, n_ctx)
            o = jnp.concatenate([o_c, o_l], axis=1)
            xj = matmul(o, da_wo[ia].astype(BF16), F32, res=xj, gate=g_m, n_ctx=n_ctx)
            ia += 1
        elif kind == 1:
            nh = ssm_A_log.shape[-1]
            di = nh * SSM_HEAD_DIM
            conv_dim = ssm_conv_w.shape[-1]
            w_in = ssm_w_in[ib].astype(BF16)
            z = matmul(h, w_in[:, :di], F32, tn=1024)
            xbc = matmul(h, w_in[:, di:di + conv_dim], F32, tn=1024)
            w_dt = jnp.zeros((d, LANES), BF16).at[:, :2 * nh].set(w_in[:, di + conv_dim:])
            dt_raw = matmul(h, w_dt, F32)
            xbc = conv_silu(xbc, ssm_conv_w[ib], ssm_conv_b[ib], n_ctx)
            expand = (jnp.arange(di)[None, :] // SSM_HEAD_DIM == jnp.arange(nh)[:, None]).astype(BF16)
            ys = [ssd_scan(xbc, dt_raw, ssm_dt_bias[ib], ssm_A_log[ib], expand, dirn, nh, n_ctx)
                  for dirn in range(2)]
            d_chan = jnp.repeat(ssm_D[ib, 0] + ssm_D[ib, 1], SSM_HEAD_DIM).reshape(1, di)
            xj = ssm_out(ys[0], ys[1], xbc, z, d_chan, ssm_norm[ib], ssm_w_out[ib].astype(BF16),
                         xj, g_m, n_ctx)
            ib += 1
        else:
            qkv = matmul(h, na_wqkv[ic].astype(BF16), BF16, tn=1024)
            bias_tab = na_bias_table(na_rpb[ic], seq // GRID_W)
            o_l = neighbourhood_attention(qkv, bias_tab, n_ctx)
            o_c = context_attention(qkv, n_ctx)
            o = jnp.concatenate([o_c, o_l], axis=1)
            xj = matmul(o, na_wo[ic].astype(BF16), F32, res=xj, gate=g_m, n_ctx=n_ctx)
            ic += 1

        hp, aff, afft = norm_mod(xj, norm_ffn[layer], sh_f, sc_f, n_ctx, router_w=router_w[layer])
        sets = [(n_ctx, seq, EC_CAPACITY_FACTOR * seq // n_exp)]
        if need_ctx:
            sets.append((0, n_ctx, EC_CAPACITY_FACTOR * n_ctx // n_exp))
        idx = route(aff, sets)
        y = expert_ffn(idx, hp, afft, g_f, exp_w1, exp_w3, exp_w2, layer, sets[0][2])
        moe_delta = moe_scatter(idx, y, n)

    return final_norm(xj, moe_delta, final_norm_w, n_ctx)
```

```python
import functools
import math

import jax
import jax.numpy as jnp
import numpy as np
from jax import lax
from jax.experimental import pallas as pl
from jax.experimental.pallas import tpu as pltpu

F32 = jnp.float32
BF16 = jnp.bfloat16

GRID_W = 64
NORM_EPS = 1e-6
ROPE_BASE = 10000.0
DA_HEADS = 8
DA_HEAD_DIM = 64
DA_KV_CHUNK = 768
DA_GROUP_LANES = 256
DA_Q_TILE = 512
DA_SCORE_BUFFERS = 4
DA_SUM_ROWS = 16
SSM_HEAD_DIM = 64
SSM_GROUPS = 4
SSM_STATE = 128
SSM_CHUNK = 128
NA_HEADS = 16
NA_ROWS = 8
NA_COLS = 16
N_MIXERS = 3
EC_CAPACITY_FACTOR = 2
MASK_VALUE = -1e30
ONE_BITS = 0x3F800000
ROUTE_PAD_COUNT = 1e9
LANES = 128
VMEM_LIMIT = 56 * 1024 * 1024


def _params(*sem):
    return pltpu.CompilerParams(dimension_semantics=sem, vmem_limit_bytes=VMEM_LIMIT)


def _split3(a):
    hi = a.astype(BF16)
    r1 = a - hi.astype(F32)
    mid = r1.astype(BF16)
    lo = (r1 - mid.astype(F32)).astype(BF16)
    return hi, mid, lo


def _dot(a, b):
    return jnp.dot(a, b, preferred_element_type=F32)


def _dot_nt(a, b):
    return lax.dot_general(a, b, (((1,), (1,)), ((), ())), preferred_element_type=F32)


def _dot_tn(a, b):
    return lax.dot_general(a, b, (((0,), (0,)), ((), ())), preferred_element_type=F32)


def _ada_kernel(c_ref, w_ref, b_ref, o_ref):
    c = c_ref[...]
    cond = c * jax.nn.sigmoid(c)
    o_ref[0] = jnp.dot(cond, w_ref[0], preferred_element_type=F32,
                       precision=lax.Precision.HIGHEST) + b_ref[0]


def ada_modulation(cond_rows, ada_w, ada_b):
    depth, d, n6 = ada_w.shape
    tn = 1536
    return pl.pallas_call(
        _ada_kernel,
        grid=(depth, n6 // tn),
        in_specs=[pl.BlockSpec((8, d), lambda l, j: (0, 0)),
                  pl.BlockSpec((1, d, tn), lambda l, j: (l, 0, j)),
                  pl.BlockSpec((1, 1, tn), lambda l, j: (l, 0, j))],
        out_specs=pl.BlockSpec((1, 8, tn), lambda l, j: (l, 0, j)),
        out_shape=jax.ShapeDtypeStruct((depth, 8, n6), F32),
        compiler_params=_params("parallel", "parallel"),
        name="ada_modulation",
    )(cond_rows, ada_w, ada_b.reshape(depth, 1, n6))


def _row_select(i, tm, n_ctx, vec2):
    row = i * tm + lax.broadcasted_iota(jnp.int32, (tm, 1), 0)
    return jnp.where(row >= n_ctx, vec2[1:2, :], vec2[0:1, :])


def _norm_mod_kernel(x_ref, w_ref, sh_ref, sc_ref, *rest, tm, n_ctx, n_exp, with_delta):
    i = pl.program_id(1)
    xf = x_ref[0]
    if with_delta:
        xf = xf + rest[0][0]
        rest = rest[1:]
    y = xf * lax.rsqrt(jnp.mean(xf * xf, axis=-1, keepdims=True) + NORM_EPS) * w_ref[...]
    sc = _row_select(i, tm, n_ctx, sc_ref[0])
    sh = _row_select(i, tm, n_ctx, sh_ref[0])
    h = y * (1.0 + sc) + sh
    if not n_exp:
        h_ref = rest[0]
        h_ref[0] = h.astype(h_ref.dtype)
        if with_delta:
            rest[1][0] = xf
        return
    rwt_ref, hp_ref, aff_ref, afft_ref = rest
    bits = lax.bitcast_convert_type(h.astype(BF16).astype(F32), jnp.uint32)
    dh = bits.shape[1] // 2
    hp_ref[0] = (bits[:, :dh] >> 16) | bits[:, dh:]
    logits = lax.dot_general(rwt_ref[...], h, (((1,), (1,)), ((), ())),
                             preferred_element_type=F32, precision=lax.Precision.HIGHEST)
    e = jnp.exp(logits - jnp.max(logits, axis=0, keepdims=True))
    aff = e / jnp.sum(e, axis=0, keepdims=True)
    aff_ref[0] = aff
    afft_ref[0] = jnp.concatenate([aff, jnp.zeros((LANES - n_exp, aff.shape[1]), F32)], axis=0).T


def norm_mod(x, w, shift, scale, n_ctx, router_w=None, delta=None, tm=768):
    b, n, d = x.shape
    n_exp = 0 if router_w is None else router_w.shape[1]
    with_delta = delta is not None
    assert not (n_exp and with_delta)
    row = pl.BlockSpec((1, tm, d), lambda bi, i: (bi, i, 0))
    in_specs = [row,
                pl.BlockSpec((1, d), lambda bi, i: (0, 0)),
                pl.BlockSpec((1, 2, d), lambda bi, i: (bi, 0, 0)),
                pl.BlockSpec((1, 2, d), lambda bi, i: (bi, 0, 0))]
    args = [x, w.reshape(1, d), shift, scale]
    if not n_exp:
        out_specs = [row]
        out_shape = [jax.ShapeDtypeStruct((b, n, d), BF16)]
        if with_delta:
            in_specs.append(row)
            args.append(delta)
            out_specs.append(row)
            out_shape.append(jax.ShapeDtypeStruct((b, n, d), F32))
    else:
        in_specs += [pl.BlockSpec((n_exp, d), lambda bi, i: (0, 0))]
        args += [router_w.T]
        out_specs = [pl.BlockSpec((1, tm, d // 2), lambda bi, i: (bi, i, 0)),
                     pl.BlockSpec((1, n_exp, tm), lambda bi, i: (bi, 0, i)),
                     pl.BlockSpec((1, tm, LANES), lambda bi, i: (bi, i, 0))]
        out_shape = [jax.ShapeDtypeStruct((b, n, d // 2), jnp.uint32),
                     jax.ShapeDtypeStruct((b, n_exp, n), F32),
                     jax.ShapeDtypeStruct((b, n, LANES), F32)]
    outs = pl.pallas_call(
        functools.partial(_norm_mod_kernel, tm=tm, n_ctx=n_ctx, n_exp=n_exp, with_delta=with_delta),
        grid=(b, n // tm),
        in_specs=in_specs, out_specs=out_specs, out_shape=out_shape,
        compiler_params=_params("parallel", "parallel"),
        name="norm_mod_router" if n_exp else "norm_mod",
    )(*args)
    return outs if (n_exp or with_delta) else outs[0]


def _final_norm_kernel(x_ref, d_ref, w_ref, o_ref):
    xf = x_ref[0] + d_ref[0]
    o_ref[0] = xf * lax.rsqrt(jnp.mean(xf * xf, axis=-1, keepdims=True) + NORM_EPS) * w_ref[...]


def final_norm(x, delta, w, n_ctx, tm=256):
    b, n, d = x.shape
    off = n_ctx // tm
    row = pl.BlockSpec((1, tm, d), lambda bi, i: (bi, i + off, 0))
    return pl.pallas_call(
        _final_norm_kernel,
        grid=(b, (n - n_ctx) // tm),
        in_specs=[row, row, pl.BlockSpec((1, d), lambda bi, i: (0, 0))],
        out_specs=pl.BlockSpec((1, tm, d), lambda bi, i: (bi, i, 0)),
        out_shape=jax.ShapeDtypeStruct((b, n - n_ctx, d), F32),
        compiler_params=_params("parallel", "parallel"),
        name="final_norm",
    )(x, delta, w.reshape(1, d))


def _matmul_kernel(a_ref, w_ref, *rest, tm, n_ctx, residual):
    if residual:
        res_ref, g_ref, o_ref = rest
    else:
        (o_ref,) = rest
    acc = _dot(a_ref[0], w_ref[...])
    if residual:
        g = _row_select(pl.program_id(1), tm, n_ctx, g_ref[0])
        acc = res_ref[0] + g * acc
    o_ref[0] = acc.astype(o_ref.dtype)


def matmul(a, w, out_dtype, res=None, gate=None, n_ctx=0, tm=768, tn=None):
    b, n, k = a.shape
    m = w.shape[1]
    tn = m if tn is None else tn
    residual = res is not None
    in_specs = [pl.BlockSpec((1, tm, k), lambda bi, i, j: (bi, i, 0)),
                pl.BlockSpec((k, tn), lambda bi, i, j: (0, j))]
    args = [a, w]
    if residual:
        in_specs += [pl.BlockSpec((1, tm, tn), lambda bi, i, j: (bi, i, j)),
                     pl.BlockSpec((1, 2, tn), lambda bi, i, j: (bi, 0, j))]
        args += [res, gate]
    return pl.pallas_call(
        functools.partial(_matmul_kernel, tm=tm, n_ctx=n_ctx, residual=residual),
        grid=(b, n // tm, m // tn),
        in_specs=in_specs,
        out_specs=pl.BlockSpec((1, tm, tn), lambda bi, i, j: (bi, i, j)),
        out_shape=jax.ShapeDtypeStruct((b, n, m), out_dtype),
        compiler_params=_params("parallel", "parallel", "arbitrary"),
        name="matmul_res" if residual else "matmul",
    )(*args)


def _rotate_pairs(x, axis):
    seg = lax.broadcasted_iota(jnp.int32, x.shape, axis) // 16
    n = x.shape[axis]
    return jnp.where(seg % 2 == 0, pltpu.roll(x, n - 16, axis), pltpu.roll(x, 16, axis))


def _da_qkv_kernel(h_ref, wqt_ref, wk_ref, wvt_ref, cq_ref, sq_ref, ck_ref, sk_ref,
                   qt_ref, k_ref, vt_ref, *, heads):
    h = h_ref[0]
    qt = _dot_nt(wqt_ref[...], h)
    cq, sq = cq_ref[...], sq_ref[...]
    for hd in range(heads):
        rs = slice(hd * LANES, (hd + 1) * LANES)
        x = qt[rs]
        qt_ref[0, rs, :] = (x * cq + _rotate_pairs(x, 0) * sq).astype(qt_ref.dtype)
    k = _dot(h, wk_ref[...])
    ck, sk = ck_ref[...], sk_ref[...]
    for hd in range(heads):
        cs = slice(hd * LANES, (hd + 1) * LANES)
        x = k[:, cs]
        k_ref[0, :, cs] = (x * ck + _rotate_pairs(x, 1) * sk).astype(k_ref.dtype)
    vt = _dot_nt(wvt_ref[...], h).astype(vt_ref.dtype)
    ones = jnp.ones((DA_SUM_ROWS, vt.shape[1]), vt_ref.dtype)
    for hd in range(heads):
        vt_ref[0, 0, hd, :LANES, :] = vt[hd * LANES:(hd + 1) * LANES]
        vt_ref[0, 0, hd, LANES:, :] = ones


def da_qkv(h, wqkv, tabs, tm):
    b, n, d = h.shape
    m = wqkv.shape[1] // 3
    cq_t, sq_t, ck, sk = tabs
    wqt = wqkv[:, :m].T.astype(BF16)
    wk = wqkv[:, m:2 * m].astype(BF16)
    wvt = wqkv[:, 2 * m:].T.astype(BF16)
    full = lambda r, c: pl.BlockSpec((r, c), lambda bi, i: (0, 0))
    return pl.pallas_call(
        functools.partial(_da_qkv_kernel, heads=m // LANES),
        grid=(b, n // tm),
        in_specs=[pl.BlockSpec((1, tm, d), lambda bi, i: (bi, i, 0)),
                  full(m, d), full(d, m), full(m, d),
                  pl.BlockSpec((LANES, tm), lambda bi, i: (0, i)),
                  pl.BlockSpec((LANES, tm), lambda bi, i: (0, i)),
                  pl.BlockSpec((tm, LANES), lambda bi, i: (i, 0)),
                  pl.BlockSpec((tm, LANES), lambda bi, i: (i, 0))],
        out_specs=[pl.BlockSpec((1, m, tm), lambda bi, i: (bi, 0, i)),
                   pl.BlockSpec((1, tm, m), lambda bi, i: (bi, i, 0)),
                   pl.BlockSpec((1, 1, m // LANES, LANES + DA_SUM_ROWS, tm), lambda bi, i: (bi, i, 0, 0, 0))],
        out_shape=[jax.ShapeDtypeStruct((b, m, n), BF16),
                   jax.ShapeDtypeStruct((b, n, m), BF16),
                   jax.ShapeDtypeStruct((b, n // tm, m // LANES, LANES + DA_SUM_ROWS, tm), BF16)],
        compiler_params=_params("parallel", "parallel"),
        name="da_qkv",
    )(h, wqt, wk, wvt, cq_t, sq_t, ck, sk)


def rope_tables(n_ctx, seq, q_scale):
    d = DA_HEAD_DIM
    t = jnp.arange(seq)
    rows = (t // GRID_W).astype(F32)
    cols = (t % GRID_W).astype(F32)
    quarter = d // 4
    freqs = ROPE_BASE ** (-jnp.arange(quarter, dtype=F32) / quarter)
    ang_r = rows[:, None] * freqs[None, :]
    ang_c = cols[:, None] * freqs[None, :]
    ang = jnp.concatenate([ang_r, ang_r, ang_c, ang_c], axis=-1)
    cos = jnp.cos(ang)
    sign = jnp.concatenate([-jnp.ones((quarter,), F32), jnp.ones((quarter,), F32)] * 2)
    sin = jnp.sin(ang) * sign[None, :]
    cos = jnp.concatenate([jnp.ones((n_ctx, d), F32), cos], axis=0)
    sin = jnp.concatenate([jnp.zeros((n_ctx, d), F32), sin], axis=0)
    cos = jnp.concatenate([cos, cos], axis=-1)
    sin = jnp.concatenate([sin, sin], axis=-1)
    return (cos * q_scale).T, (sin * q_scale).T, cos, sin


def _diff_attn_kernel(*refs, nq, tk, kw, nchunks, lam_init):
    qt_refs = refs[:nq]
    k_ref, vt_ref, lq1_ref, lk1_ref, lq2_ref, lk2_ref, sub_ref, o_ref, acc_ref, s_ref = refs[nq:]
    gw = DA_GROUP_LANES
    ngroups = 2 * nq
    sub = lax.broadcasted_iota(jnp.int32, (LANES, gw), 0)
    zero = jnp.zeros((LANES, gw), qt_refs[0].dtype)
    qg = []
    for g in range(ngroups):
        q = qt_refs[g // 2][0]
        qg.append(jnp.where((sub < DA_HEAD_DIM) if g % 2 == 0 else (sub >= DA_HEAD_DIM), q, zero))
    acc_ref[...] = jnp.zeros(acc_ref.shape, F32)
    half = kw // 2

    def col_reduce(x, op):
        return op(op(x.reshape(8, x.shape[0] // 8, x.shape[1]), axis=0), axis=0, keepdims=True)

    def scores(t):
        c, g = divmod(t, ngroups)
        mx = None
        for hf in range(2):
            k = k_ref[0, c * tk + hf * half:c * tk + (hf + 1) * half, :]
            s = _dot(k, qg[g])
            s_ref[t % DA_SCORE_BUFFERS, hf * half:(hf + 1) * half, :] = s
            hm = col_reduce(s, jnp.max)
            mx = hm if mx is None else jnp.maximum(mx, hm)
        return mx

    def softmax_pv(t, mx, ml):
        c, g = divmod(t, ngroups)
        m_old, l_old = ml
        vt = vt_ref[0, c, 0]
        m_new = jnp.maximum(m_old, mx)
        alpha = jnp.exp2(m_old - m_new)
        pb = jnp.exp2(s_ref[t % DA_SCORE_BUFFERS] - m_new).astype(BF16)
        pv = _dot(vt[:, :kw], pb)
        acc_ref[g] = alpha * acc_ref[g] + pv[:LANES]
        return m_new, alpha * l_old + pv[LANES:LANES + 1]

    nsteps = nchunks * ngroups
    ahead = DA_SCORE_BUFFERS - 1
    carry = [(jnp.full((1, gw), -jnp.inf, F32), jnp.zeros((1, gw), F32))] * ngroups
    maxima = {t: scores(t) for t in range(min(ahead, nsteps))}
    for t in range(nsteps):
        if t + ahead < nsteps:
            maxima[t + ahead] = scores(t + ahead)
        carry[t % ngroups] = softmax_pv(t, maxima.pop(t), carry[t % ngroups])
    lam = (jnp.exp(jnp.sum(lq1_ref[...] * lk1_ref[...], axis=-1, keepdims=True))
           - jnp.exp(jnp.sum(lq2_ref[...] * lk2_ref[...], axis=-1, keepdims=True)) + lam_init)
    for qh in range(ngroups // 2):
        od = acc_ref[2 * qh] / carry[2 * qh][1] - lam * (acc_ref[2 * qh + 1] / carry[2 * qh + 1][1])
        y = od * lax.rsqrt(jnp.mean(od * od, axis=0, keepdims=True) + NORM_EPS) * sub_ref[...]
        o_ref[0, qh * gw:(qh + 1) * gw, :] = (y * (1.0 - lam_init)).T.astype(o_ref.dtype)


def diff_attention(qt, k, vt, lam_params, subln, lam_init, q_start, n_q, n_k, tq):
    b, m, n = qt.shape
    h = m // LANES
    tk = vt.shape[-1]
    kw = min(tk, n_k)
    nchunks = n_k // kw
    gw = DA_GROUP_LANES
    nq = tq // gw
    assert q_start % gw == 0 and tq % gw == 0 and n_q % tq == 0
    qoff = q_start // gw
    vec = lambda a: a.reshape(1, -1).astype(F32)
    small = pl.BlockSpec((1, DA_HEAD_DIM), lambda bi, hi, i: (0, 0))
    q_specs = [pl.BlockSpec((1, LANES, gw), functools.partial(
        lambda bi, hi, i, j: (bi, hi, qoff + nq * i + j), j=j)) for j in range(nq)]
    return pl.pallas_call(
        functools.partial(_diff_attn_kernel, nq=nq, tk=tk, kw=kw, nchunks=nchunks, lam_init=lam_init),
        grid=(b, h, n_q // tq),
        in_specs=q_specs + [
                  pl.BlockSpec((1, n_k, LANES), lambda bi, hi, i: (bi, 0, hi)),
                  pl.BlockSpec((1, nchunks, 1, vt.shape[3], tk), lambda bi, hi, i: (bi, 0, hi, 0, 0)),
                  small, small, small, small,
                  pl.BlockSpec((LANES, 1), lambda bi, hi, i: (0, 0))],
        out_specs=pl.BlockSpec((1, tq, LANES), lambda bi, hi, i: (bi, i, hi)),
        out_shape=jax.ShapeDtypeStruct((b, n_q, m), BF16),
        scratch_shapes=[pltpu.VMEM((2 * tq // DA_GROUP_LANES, LANES, DA_GROUP_LANES), F32),
                        pltpu.VMEM((DA_SCORE_BUFFERS, kw, DA_GROUP_LANES), F32)],
        compiler_params=_params("parallel", "parallel", "arbitrary"),
        name="diff_attention",
    )(*([qt] * nq), k, vt, *[vec(p) for p in lam_params], subln.reshape(-1, 1).astype(F32))


def _pair_scores(qp, kp):
    lane = lax.broadcasted_iota(jnp.int32, qp.shape, 1)
    zero = jnp.zeros_like(qp)
    qs = jnp.concatenate([jnp.where(lane < 64, qp, zero), jnp.where(lane >= 64, qp, zero)], axis=0)
    return _dot_nt(qs, kp)


def _pair_softmax_pv(s, vp, bias, n_bias):
    tq = s.shape[0] // 2
    if n_bias:
        s_w = s[:, :n_bias] + bias
        s_c = s[:, n_bias:]
        m = jnp.maximum(jnp.max(s_w, axis=-1, keepdims=True), jnp.max(s_c, axis=-1, keepdims=True))
        p_w = jnp.exp(s_w - m)
        p_c = jnp.exp(s_c - m)
        l = jnp.sum(p_w, axis=-1, keepdims=True) + jnp.sum(p_c, axis=-1, keepdims=True)
        pv = _dot(p_w.astype(BF16), vp[:n_bias]) + _dot(p_c.astype(BF16), vp[n_bias:])
    else:
        m = jnp.max(s, axis=-1, keepdims=True)
        p = jnp.exp(s - m)
        l = jnp.sum(p, axis=-1, keepdims=True)
        pv = _dot(p.astype(BF16), vp)
    o = pv / l
    lane_o = lax.broadcasted_iota(jnp.int32, (tq, LANES), 1)
    return jnp.where(lane_o < 64, o[:tq], o[tq:])


def _pairs_attention(q_of, k_of, v_of, bias_of, n_bias, o_ref):
    npairs = NA_HEADS // 2
    s_next = _pair_scores(q_of(0), k_of(0))
    for hp in range(npairs):
        s_cur = s_next
        if hp + 1 < npairs:
            s_next = _pair_scores(q_of(hp + 1), k_of(hp + 1))
        o = _pair_softmax_pv(s_cur, v_of(hp), bias_of(hp), n_bias)
        o_ref[0, :, hp * LANES:(hp + 1) * LANES] = o.astype(o_ref.dtype)


def _na_kernel(*refs, wr, n_ctx, scale):
    q_ref = refs[0]
    k_refs = refs[1:1 + wr]
    v_refs = refs[1 + wr:1 + 2 * wr]
    kc_ref, vc_ref, bias_ref, o_ref, kbuf, vbuf = refs[1 + 2 * wr:]
    nw = wr * GRID_W
    for w in range(wr):
        kbuf[w * GRID_W:(w + 1) * GRID_W, :] = k_refs[w][0]
        vbuf[w * GRID_W:(w + 1) * GRID_W, :] = v_refs[w][0]
    kbuf[nw:nw + n_ctx, :] = kc_ref[0]
    vbuf[nw:nw + n_ctx, :] = vc_ref[0]
    cols = lambda hp: slice(hp * LANES, (hp + 1) * LANES)
    _pairs_attention(lambda hp: q_ref[0, :, cols(hp)] * scale,
                     lambda hp: kbuf[:, cols(hp)],
                     lambda hp: vbuf[:, cols(hp)],
                     lambda hp: bias_ref[0, 2 * hp:2 * hp + 2].reshape(2 * GRID_W, nw),
                     nw, o_ref)


def na_bias_table(rpb, rows):
    wr = min(NA_ROWS, rows)
    qcol = np.arange(GRID_W)
    col_start = np.clip(qcol - NA_COLS // 2, 0, GRID_W - NA_COLS)
    kc = np.arange(GRID_W)
    inside = (kc[None, :] >= col_start[:, None]) & (kc[None, :] < col_start[:, None] + NA_COLS)
    dc = np.clip(kc[None, :] - qcol[:, None] + NA_COLS - 1, 0, 2 * NA_COLS - 2)
    dr = np.arange(wr)[None, :] - np.arange(wr)[:, None] + NA_ROWS - 1
    sel_r = (dr[:, :, None] == np.arange(2 * NA_ROWS - 1)).astype(np.float32)
    sel_c = (dc[:, :, None] == np.arange(2 * NA_COLS - 1)).astype(np.float32)
    tab = jnp.einsum("cwa,hab,qkb->chqwk", sel_r, rpb.astype(F32), sel_c,
                     precision=lax.Precision.HIGHEST)
    tab = jnp.where(inside[None, None, :, None, :], tab, MASK_VALUE)
    return tab.reshape(wr, rpb.shape[0], GRID_W, wr * GRID_W)


def neighbourhood_attention(qkv, bias_tab, n_ctx):
    b, n, m3 = qkv.shape
    m = m3 // 3
    rows = (n - n_ctx) // GRID_W
    wr = min(NA_ROWS, rows)
    cb = n_ctx // GRID_W
    rs = lambda r: jnp.clip(r - wr // 2, 0, rows - wr)
    blk = lambda f: pl.BlockSpec((1, GRID_W, m), f)
    in_specs = [blk(lambda bi, r: (bi, cb + r, 0))]
    in_specs += [blk(functools.partial(lambda bi, r, w: (bi, cb + rs(r) + w, 1), w=w)) for w in range(wr)]
    in_specs += [blk(functools.partial(lambda bi, r, w: (bi, cb + rs(r) + w, 2), w=w)) for w in range(wr)]
    in_specs += [pl.BlockSpec((1, n_ctx, m), lambda bi, r: (bi, 0, 1)),
                 pl.BlockSpec((1, n_ctx, m), lambda bi, r: (bi, 0, 2)),
                 pl.BlockSpec((1, NA_HEADS, GRID_W, wr * GRID_W), lambda bi, r: (r - rs(r), 0, 0, 0))]
    nk = wr * GRID_W + n_ctx
    return pl.pallas_call(
        functools.partial(_na_kernel, wr=wr, n_ctx=n_ctx, scale=(m // NA_HEADS) ** -0.5),
        grid=(b, rows),
        in_specs=in_specs,
        out_specs=pl.BlockSpec((1, GRID_W, m), lambda bi, r: (bi, r, 0)),
        out_shape=jax.ShapeDtypeStruct((b, n - n_ctx, m), BF16),
        scratch_shapes=[pltpu.VMEM((nk, m), BF16), pltpu.VMEM((nk, m), BF16)],
        compiler_params=_params("parallel", "arbitrary"),
        name="neighbourhood_attention",
    )(*([qkv] * (1 + 2 * wr + 2)), bias_tab)


def _ctx_attn_kernel(q_ref, k_ref, v_ref, o_ref, *, scale):
    cols = lambda hp: slice(hp * LANES, (hp + 1) * LANES)
    _pairs_attention(lambda hp: q_ref[0, :, cols(hp)] * scale,
                     lambda hp: k_ref[0, :, cols(hp)],
                     lambda hp: v_ref[0, :, cols(hp)],
                     lambda hp: None, 0, o_ref)


def context_attention(qkv, n_ctx):
    b, n, m3 = qkv.shape
    m = m3 // 3
    return pl.pallas_call(
        functools.partial(_ctx_attn_kernel, scale=(m // NA_HEADS) ** -0.5),
        grid=(b,),
        in_specs=[pl.BlockSpec((1, n_ctx, m), lambda bi: (bi, 0, 0)),
                  pl.BlockSpec((1, n_ctx, m), lambda bi: (bi, 0, 1)),
                  pl.BlockSpec((1, n_ctx, m), lambda bi: (bi, 0, 2))],
        out_specs=pl.BlockSpec((1, n_ctx, m), lambda bi: (bi, 0, 0)),
        out_shape=jax.ShapeDtypeStruct((b, n_ctx, m), BF16),
        compiler_params=_params("parallel"),
        name="context_attention",
    )(qkv, qkv, qkv)


def _conv_silu_kernel(x_ref, w_ref, b_ref, o_ref, *, n, n_ctx):
    x = x_ref[0]
    row = lax.broadcasted_iota(jnp.int32, x.shape, 0)
    xm = jnp.where((row == 0) | (row == n_ctx), 0.0, pltpu.roll(x, 1, 0))
    xp = jnp.where((row == n_ctx - 1) | (row == n - 1), 0.0, pltpu.roll(x, n - 1, 0))
    y = w_ref[0:1, :] * xm + w_ref[1:2, :] * x + w_ref[2:3, :] * xp + b_ref[...]
    o_ref[0] = y * jax.nn.sigmoid(y)


def conv_silu(x, w, bias, n_ctx):
    b, n, c = x.shape
    return pl.pallas_call(
        functools.partial(_conv_silu_kernel, n=n, n_ctx=n_ctx),
        grid=(b, c // LANES),
        in_specs=[pl.BlockSpec((1, n, LANES), lambda bi, j: (bi, 0, j)),
                  pl.BlockSpec((3, LANES), lambda bi, j: (0, j)),
                  pl.BlockSpec((1, LANES), lambda bi, j: (0, j))],
        out_specs=pl.BlockSpec((1, n, LANES), lambda bi, j: (bi, 0, j)),
        out_shape=jax.ShapeDtypeStruct((b, n, c), F32),
        compiler_params=_params("parallel", "parallel"),
        name="conv_silu",
    )(x, w, bias.reshape(1, c))


def _softplus(x):
    return jnp.maximum(x, 0.0) + jnp.log1p(jnp.exp(-jnp.abs(x)))


def _ssd_kernel(x_ref, b_ref, c_ref, dt_ref, dtb_ref, alog_ref, e_ref, y_ref, state_ref,
                *, direction, nheads):
    L = SSM_CHUNK
    P = SSM_HEAD_DIM
    G = SSM_GROUPS
    hpg = nheads // G
    reverse = direction == 1

    @pl.when(pl.program_id(1) == 0)
    def _():
        state_ref[...] = jnp.zeros(state_ref.shape, F32)

    li = lax.broadcasted_iota(jnp.int32, (L, L), 0)
    si = lax.broadcasted_iota(jnp.int32, (L, L), 1)
    allowed = (si >= li) if reverse else (si <= li)
    tri = jnp.where(allowed, 1.0, 0.0).astype(BF16)
    tri_t = jnp.where((li >= si) if reverse else (li <= si), 1.0, 0.0).astype(BF16)

    dt_raw = dt_ref[0][:, direction * nheads:(direction + 1) * nheads]
    dt = _softplus(dt_raw + dtb_ref[...])
    a = dt * (-jnp.exp(alog_ref[...]))
    a3 = _split3(a)
    acs = sum(_dot(tri, p) for p in a3)
    acs_t = sum(_dot_tn(p, tri_t) for p in a3)
    expand = e_ref[...]
    acs_x = sum(_dot(p, expand) for p in _split3(acs))
    dt_x = sum(_dot(p, expand) for p in _split3(dt))
    end = 0 if reverse else L - 1
    total_x = acs_x[end:end + 1, :]

    xd = x_ref[0] * dt_x
    xd_b = xd.astype(BF16)
    x_state = (xd * jnp.exp(total_x - acs_x)).astype(BF16)
    exp_acs_x = jnp.exp(acs_x)
    chunk_decay_x = jnp.exp(total_x)
    lane = lax.broadcasted_iota(jnp.int32, (L, LANES), 1)

    for g in range(G):
        gs = slice(g * SSM_STATE, (g + 1) * SSM_STATE)
        hs = slice(g * hpg * P, (g + 1) * hpg * P)
        cg = c_ref[0][:, gs].astype(BF16)
        bg = b_ref[0][:, gs].astype(BF16)
        cb = _dot_nt(cg, bg)
        h_prev = state_ref[:, hs]
        y_off = _dot(cg, h_prev.astype(BF16)) * exp_acs_x[:, hs]
        state_ref[:, hs] = h_prev * chunk_decay_x[:, hs] + _dot_tn(bg, x_state[:, hs])
        for rp in range(hpg // 2):
            mats = []
            for r in (2 * rp, 2 * rp + 1):
                hidx = g * hpg + r
                seg = acs[:, hidx:hidx + 1] - acs_t[hidx:hidx + 1, :]
                mats.append((cb * jnp.exp(jnp.where(allowed, seg, -jnp.inf))).astype(BF16))
            ps = slice((g * hpg + 2 * rp) * P, (g * hpg + 2 * rp + 2) * P)
            yd = _dot(jnp.concatenate(mats, axis=0), xd_b[:, ps])
            y_ref[0, :, ps] = jnp.where(lane < P, yd[:L], yd[L:]) + y_off[:, 2 * rp * P:(2 * rp + 2) * P]


def ssd_scan(xbc, dt_raw, dt_bias, a_log, expand, direction, nheads, n_ctx):
    b, n, _ = xbc.shape
    d_inner = nheads * SSM_HEAD_DIM
    gn = SSM_GROUPS * SSM_STATE
    L = SSM_CHUNK
    nc = n // L
    ncc = n_ctx // L
    if direction == 0:
        cidx = lambda c: c
    else:
        cidx = lambda c: jnp.where(c < ncc, ncc - 1 - c, nc - 1 - (c - ncc))
    return pl.pallas_call(
        functools.partial(_ssd_kernel, direction=direction, nheads=nheads),
        grid=(b, nc),
        in_specs=[pl.BlockSpec((1, L, d_inner), lambda bi, c: (bi, cidx(c), 0)),
                  pl.BlockSpec((1, L, gn), lambda bi, c: (bi, cidx(c), d_inner // gn)),
                  pl.BlockSpec((1, L, gn), lambda bi, c: (bi, cidx(c), d_inner // gn + 1)),
                  pl.BlockSpec((1, L, LANES), lambda bi, c: (bi, cidx(c), 0)),
                  pl.BlockSpec((1, nheads), lambda bi, c: (0, 0)),
                  pl.BlockSpec((1, nheads), lambda bi, c: (0, 0)),
                  pl.BlockSpec((nheads, d_inner), lambda bi, c: (0, 0))],
        out_specs=pl.BlockSpec((1, L, d_inner), lambda bi, c: (bi, cidx(c), 0)),
        out_shape=jax.ShapeDtypeStruct((b, n, d_inner), F32),
        scratch_shapes=[pltpu.VMEM((SSM_STATE, d_inner), F32)],
        compiler_params=_params("parallel", "arbitrary"),
        name="ssd_scan_%d" % direction,
    )(xbc, xbc, xbc, dt_raw, dt_bias[direction:direction + 1], a_log[direction:direction + 1], expand)


def _ssm_out_kernel(y0_ref, y1_ref, x_ref, z_ref, d_ref, nw_ref, w_ref, res_ref, g_ref, o_ref,
                    *, tm, n_ctx):
    z = z_ref[0]
    u = (y0_ref[0] + y1_ref[0] + d_ref[...] * x_ref[0]) * (z * jax.nn.sigmoid(z))
    un = u * lax.rsqrt(jnp.mean(u * u, axis=-1, keepdims=True) + NORM_EPS) * nw_ref[...]
    g = _row_select(pl.program_id(1), tm, n_ctx, g_ref[0])
    o_ref[0] = res_ref[0] + g * _dot(un.astype(BF16), w_ref[...])


def ssm_out(y0, y1, xbc, z, d_chan, norm_w, w_out, res, gate, n_ctx, tm=256):
    b, n, di = y0.shape
    d = w_out.shape[1]
    row = lambda c: pl.BlockSpec((1, tm, c), lambda bi, i: (bi, i, 0))
    vec = lambda c: pl.BlockSpec((1, c), lambda bi, i: (0, 0))
    return pl.pallas_call(
        functools.partial(_ssm_out_kernel, tm=tm, n_ctx=n_ctx),
        grid=(b, n // tm),
        in_specs=[row(di), row(di), row(di), row(di), vec(di), vec(di),
                  pl.BlockSpec((di, d), lambda bi, i: (0, 0)), row(d),
                  pl.BlockSpec((1, 2, d), lambda bi, i: (bi, 0, 0))],
        out_specs=row(d),
        out_shape=jax.ShapeDtypeStruct((b, n, d), F32),
        compiler_params=_params("parallel", "parallel"),
        name="ssm_out",
    )(y0, y1, xbc, z, d_chan, norm_w.reshape(1, di), w_out, res, gate)


def _cumsum_lanes(x):
    n = x.shape[-1]
    lane = lax.broadcasted_iota(jnp.int32, x.shape, x.ndim - 1)
    s = 1
    while s < n:
        x = x + jnp.where(lane >= s, pltpu.roll(x, s, x.ndim - 1), 0.0)
        s *= 2
    return x


def _route_set(aff, cap):
    e, t = aff.shape
    bits = lax.bitcast_convert_type(aff, jnp.int32)

    def count_ge(thr):
        return jnp.sum(jnp.where(bits >= thr, 1.0, 0.0), axis=1, keepdims=True)

    def step(_, lohi):
        lo, hi = lohi
        mid = lo + (hi - lo) // 2
        ok = count_ge(mid) >= cap
        return jnp.where(ok, mid, lo), jnp.where(ok, hi, mid)

    lo0 = jnp.zeros((e, 1), jnp.int32)
    hi0 = jnp.full((e, 1), ONE_BITS + 1, jnp.int32)
    thr, _ = lax.fori_loop(0, 31, step, (lo0, hi0))
    gt = jnp.where(bits > thr, 1.0, 0.0)
    eq = jnp.where(bits == thr, 1.0, 0.0)
    need = cap - jnp.sum(gt, axis=1, keepdims=True)
    sel = gt + eq * jnp.where(_cumsum_lanes(eq) <= need, 1.0, 0.0)
    return sel, _cumsum_lanes(sel)


def _route_kernel(aff_ref, idx_ref, cnt_ref, tile_ref, pos_ref, *, sets):
    n_exp = aff_ref.shape[1]
    for t0, t, cap, col0 in sets:
        _, cnt = _route_set(aff_ref[0, :, t0:t0 + t], cap)
        cnt_ref[:, :t] = cnt
        nt = t // LANES
        ntp = max(nt, 16)
        jp = -(-cap // LANES) * LANES
        tile_ref[...] = jnp.full(tile_ref.shape, ROUTE_PAD_COUNT, F32)
        jrow = lax.broadcasted_iota(jnp.int32, (1, jp), 1).astype(F32)
        ksub = lax.broadcasted_iota(jnp.int32, (ntp, jp), 0)

        def per_expert(ei, c):
            row = cnt_ref[pl.ds(ei, 1), :t]
            for k in range(nt):
                tile_ref[k:k + 1, :] = row[:, k * LANES:(k + 1) * LANES]
            tiles = tile_ref[:ntp]
            full = jnp.sum(jnp.where(tiles[:, LANES - 1:] <= jrow, 1.0, 0.0), axis=0, keepdims=True)
            onehot = jnp.where(ksub == full.astype(jnp.int32), 1.0, 0.0).astype(BF16)
            bound = sum(_dot_tn(p, onehot) for p in _split3(tiles))
            inside = jnp.sum(jnp.where(bound <= jrow, 1.0, 0.0), axis=0, keepdims=True)
            pos = (LANES * full + inside).astype(jnp.int32) + t0
            pad = pos_ref.shape[1] - jp
            if pad:
                pos = jnp.concatenate([pos, jnp.zeros((1, pad), jnp.int32)], axis=1)
            pos_ref[pl.ds(ei, 1), :] = pos
            return c

        lax.fori_loop(0, n_exp, per_expert, 0)
        idx_ref[0, :, col0:col0 + cap] = pos_ref[:, :cap]


def route(aff, sets):
    b, e, n = aff.shape
    m = sum(s[2] for s in sets)
    tmax = max(s[1] for s in sets)
    cols, sets_r = 0, []
    for t0, t, cap in sets:
        sets_r.append((t0, t, cap, cols))
        cols += cap
    return pl.pallas_call(
        functools.partial(_route_kernel, sets=tuple(sets_r)),
        grid=(b,),
        in_specs=[pl.BlockSpec((1, e, n), lambda bi: (bi, 0, 0))],
        out_specs=pl.BlockSpec((1, e, m), lambda bi: (bi, 0, 0)),
        out_shape=jax.ShapeDtypeStruct((b, e, m), jnp.int32),
        scratch_shapes=[pltpu.VMEM((e, tmax), F32), pltpu.VMEM((max(tmax // LANES, 16), LANES), F32),
                        pltpu.VMEM((e, -(-max(s[2] for s in sets) // LANES) * LANES), jnp.int32)],
        compiler_params=_params("parallel"),
        name="route",
    )(aff)


def _expert_ffn_kernel(idx_ref, idxn_ref, hp_ref, afft_ref, g_ref, w1_ref, w3_ref, w2_ref, o_ref,
                       xw_ref, gw_ref, xs_ref, gate_ref, *, m, n_lat, rows_per_step):
    ei = pl.program_id(1)
    f = pl.program_id(2)
    slot = ei % 2

    def copy_row(src_idx_ref, dst_slot, j):
        i = src_idx_ref[0, 0, 0, j]
        xw_ref[dst_slot, pl.ds(j, 1), :] = hp_ref[0, pl.ds(i, 1), :]
        gw_ref[dst_slot, pl.ds(j, 1), :] = afft_ref[0, pl.ds(i, 1), :]

    @pl.when((f == 0) & (ei == 0))
    def _():
        def gather(g, c):
            base = pl.multiple_of(g * 8, 8)
            for r in range(8):
                copy_row(idx_ref, 0, base + r)
            return c

        lax.fori_loop(0, m // 8, gather, 0)

    @pl.when(f == 0)
    def _():
        w = xw_ref[slot]
        lo = lax.bitcast_convert_type(w << 16, F32).astype(BF16)
        hi = lax.bitcast_convert_type(w & jnp.uint32(0xFFFF0000), F32).astype(BF16)
        xs_ref[...] = jnp.concatenate([lo, hi], axis=1)
        lane = lax.broadcasted_iota(jnp.int32, gate_ref.shape, 1)
        gate = jnp.sum(jnp.where(lane == ei, gw_ref[slot], 0.0), axis=1, keepdims=True)
        gate_ref[...] = jnp.broadcast_to(gate, gate_ref.shape)
        o_ref[...] = jnp.zeros(o_ref.shape, F32)

    for r in range(rows_per_step):
        copy_row(idxn_ref, 1 - slot, f * rows_per_step + r)

    x = xs_ref[...]
    h1 = _dot(x, w1_ref[0].astype(BF16))
    h3 = _dot(x, w3_ref[0].astype(BF16))
    hid = (h1 * jax.nn.sigmoid(h1) * h3).astype(BF16)
    part = _dot(hid, w2_ref[0].astype(BF16))

    o_ref[0, 0] = o_ref[0, 0] + part

    @pl.when(f == pl.num_programs(2) - 1)
    def _():
        row = lax.broadcasted_iota(jnp.int32, (m, 1), 0)
        g = jnp.where(row < n_lat, g_ref[0, 1:2, :], g_ref[0, 0:1, :])
        o_ref[0, 0] = o_ref[0, 0] * gate_ref[:, 0:1] * g


def expert_ffn(idx, hp, afft, gate2, w1, w3, w2, layer, n_lat, tf=256):
    b, e, m = idx.shape
    n, dh = hp.shape[1:]
    d = 2 * dh
    ff = w1.shape[3]
    single = dict(pipeline_mode=pl.Buffered(1))
    nf = ff // tf
    assert m % nf == 0
    idx4 = idx.reshape(b, e, 1, m)
    return pl.pallas_call(
        functools.partial(_expert_ffn_kernel, m=m, n_lat=n_lat, rows_per_step=m // nf),
        grid=(b, e, nf),
        in_specs=[pl.BlockSpec((1, 1, 1, m), lambda bi, ei, f: (bi, ei, 0, 0), memory_space=pltpu.SMEM),
                  pl.BlockSpec((1, 1, 1, m), lambda bi, ei, f: (bi, jnp.minimum(ei + 1, e - 1), 0, 0),
                               memory_space=pltpu.SMEM),
                  pl.BlockSpec((1, n, dh), lambda bi, ei, f: (bi, 0, 0), **single),
                  pl.BlockSpec((1, n, LANES), lambda bi, ei, f: (bi, 0, 0), **single),
                  pl.BlockSpec((1, 2, d), lambda bi, ei, f: (bi, 0, 0)),
                  pl.BlockSpec((None, 1, d, tf), lambda bi, ei, f: (layer, ei, 0, f)),
                  pl.BlockSpec((None, 1, d, tf), lambda bi, ei, f: (layer, ei, 0, f)),
                  pl.BlockSpec((None, 1, tf, d), lambda bi, ei, f: (layer, ei, f, 0))],
        out_specs=pl.BlockSpec((1, 1, m, d), lambda bi, ei, f: (bi, ei, 0, 0)),
        out_shape=jax.ShapeDtypeStruct((b, e, m, d), F32),
        scratch_shapes=[pltpu.VMEM((2, m, dh), jnp.uint32), pltpu.VMEM((2, m, LANES), F32),
                        pltpu.VMEM((m, d), BF16), pltpu.VMEM((m, LANES), F32)],
        compiler_params=_params("arbitrary", "arbitrary", "arbitrary"),
        name="expert_ffn",
    )(idx4, idx4, hp, afft, gate2, w1, w3, w2)


def _scatter_kernel(idx_ref, y_ref, o_ref, *, m):
    ei = pl.program_id(1)

    @pl.when(ei == 0)
    def _():
        o_ref[...] = jnp.zeros(o_ref.shape, F32)

    def add_rows(g, c):
        base = pl.multiple_of(g * 8, 8)
        rows = [idx_ref[0, 0, 0, base + r] for r in range(8)]
        sums = [o_ref[0, pl.ds(i, 1), :] + y_ref[0, 0, pl.ds(base + r, 1), :] for r, i in enumerate(rows)]
        for i, v in zip(rows, sums):
            o_ref[0, pl.ds(i, 1), :] = v
        return c

    lax.fori_loop(0, m // 8, add_rows, 0)


def moe_scatter(idx, y, n):
    b, e, m = idx.shape
    d = y.shape[-1]
    return pl.pallas_call(
        functools.partial(_scatter_kernel, m=m),
        grid=(b, e),
        in_specs=[pl.BlockSpec((1, 1, 1, m), lambda bi, ei: (bi, ei, 0, 0), memory_space=pltpu.SMEM),
                  pl.BlockSpec((1, 1, m, d), lambda bi, ei: (bi, ei, 0, 0))],
        out_specs=pl.BlockSpec((1, n, d), lambda bi, ei: (bi, 0, 0), pipeline_mode=pl.Buffered(1)),
        out_shape=jax.ShapeDtypeStruct((b, n, d), F32),
        compiler_params=_params("parallel", "arbitrary"),
        name="moe_scatter",
    )(idx.reshape(b, e, 1, m), y)


def _lambda_init(layer):
    return 0.8 - 0.6 * math.exp(-0.3 * layer)


def kernel(x, c, ctx, c_ctx, ada_w, ada_b, norm_mix, norm_ffn, final_norm_w,
           da_wqkv, da_wo, da_lam_q1, da_lam_k1, da_lam_q2, da_lam_k2, da_subln,
           ssm_w_in, ssm_conv_w, ssm_conv_b, ssm_dt_bias, ssm_A_log, ssm_D, ssm_norm, ssm_w_out,
           na_wqkv, na_wo, na_rpb, router_w, exp_w1, exp_w3, exp_w2):
    b, seq, d = x.shape
    n_ctx = ctx.shape[1]
    n = n_ctx + seq
    depth = ada_w.shape[0]
    n_exp = router_w.shape[-1]

    cond_rows = jnp.zeros((8, d), F32).at[:b].set(c).at[b].set(c_ctx)
    mods = ada_modulation(cond_rows, ada_w, ada_b).reshape(depth, 8, 6, d)
    mods = jnp.stack([jnp.broadcast_to(mods[:, b:b + 1], (depth, b, 6, d)), mods[:, :b]], axis=2)

    xj = jnp.concatenate([ctx, x], axis=1)
    rope_tabs = rope_tables(n_ctx, seq, DA_HEAD_DIM ** -0.5 * math.log2(math.e))

    ia = ib = ic = 0
    moe_delta = None
    for layer in range(depth):
        need_ctx = layer < depth - 1
        sh_m, sc_m, g_m, sh_f, sc_f, g_f = [mods[layer, :, :, k] for k in range(6)]
        if moe_delta is None:
            h = norm_mod(xj, norm_mix[layer], sh_m, sc_m, n_ctx)
        else:
            h, xj = norm_mod(xj, norm_mix[layer], sh_m, sc_m, n_ctx, delta=moe_delta)
        kind = layer % N_MIXERS
        if kind == 0:
            qt, k, vt = da_qkv(h, da_wqkv[ia], rope_tabs, DA_KV_CHUNK)
            lam_params = (da_lam_q1[ia], da_lam_k1[ia], da_lam_q2[ia], da_lam_k2[ia])
            li = _lambda_init(layer)
            o_l = diff_attention(qt, k, vt, lam_params, da_subln[ia], li, n_ctx, seq, n, DA_Q_TILE)
            o_c = diff_attention(qt, k, vt, lam_params, da_subln[ia], li, 0, n_ctx, n_ctx, n_ctx)
            o = jnp.concatenate([o_c, o_l], axis=1)
            xj = matmul(o, da_wo[ia].astype(BF16), F32, res=xj, gate=g_m, n_ctx=n_ctx)
            ia += 1
        elif kind == 1:
            nh = ssm_A_log.shape[-1]
            di = nh * SSM_HEAD_DIM
            conv_dim = ssm_conv_w.shape[-1]
            w_in = ssm_w_in[ib].astype(BF16)
            z = matmul(h, w_in[:, :di], F32, tn=1024)
            xbc = matmul(h, w_in[:, di:di + conv_dim], F32, tn=1024)
            w_dt = jnp.zeros((d, LANES), BF16).at[:, :2 * nh].set(w_in[:, di + conv_dim:])
            dt_raw = matmul(h, w_dt, F32)
            xbc = conv_silu(xbc, ssm_conv_w[ib], ssm_conv_b[ib], n_ctx)
            expand = (jnp.arange(di)[None, :] // SSM_HEAD_DIM == jnp.arange(nh)[:, None]).astype(BF16)
            ys = [ssd_scan(xbc, dt_raw, ssm_dt_bias[ib], ssm_A_log[ib], expand, dirn, nh, n_ctx)
                  for dirn in range(2)]
            d_chan = jnp.repeat(ssm_D[ib, 0] + ssm_D[ib, 1], SSM_HEAD_DIM).reshape(1, di)
            xj = ssm_out(ys[0], ys[1], xbc, z, d_chan, ssm_norm[ib], ssm_w_out[ib].astype(BF16),
                         xj, g_m, n_ctx)
            ib += 1
        else:
            qkv = matmul(h, na_wqkv[ic].astype(BF16), BF16, tn=1024)
            bias_tab = na_bias_table(na_rpb[ic], seq // GRID_W)
            o_l = neighbourhood_attention(qkv, bias_tab, n_ctx)
            o_c = context_attention(qkv, n_ctx)
            o = jnp.concatenate([o_c, o_l], axis=1)
            xj = matmul(o, na_wo[ic].astype(BF16), F32, res=xj, gate=g_m, n_ctx=n_ctx)
            ic += 1

        hp, aff, afft = norm_mod(xj, norm_ffn[layer], sh_f, sc_f, n_ctx, router_w=router_w[layer])
        sets = [(n_ctx, seq, EC_CAPACITY_FACTOR * seq // n_exp)]
        if need_ctx:
            sets.append((0, n_ctx, EC_CAPACITY_FACTOR * n_ctx // n_exp))
        idx = route(aff, sets)
        y = expert_ffn(idx, hp, afft, g_f, exp_w1, exp_w3, exp_w2, layer, sets[0][2])
        moe_delta = moe_scatter(idx, y, n)

    return final_norm(xj, moe_delta, final_norm_w, n_ctx)
```
